```python
import math
import jax, jax.numpy as jnp
from jax import lax
import numpy as np

D_MODEL = 1024
BATCH = 4
SEQ = 4096
DEPTH = 4
DEC_BATCH = 32
DEC_SEQ = 1
PAST_LEN = 8192
PAGE_SIZE = 128

A_HEADS = 8
A_HEAD_DIM = 64
A_WIDTH = A_HEADS * A_HEAD_DIM
MOBA_BLOCK = 256
MOBA_TOPK = 3
MOBA_QCHUNK = 16
HG_HEADS = 4
HG_DK = 128
HG_DV = 128
HG_KW = HG_HEADS * HG_DK
HG_VW = HG_HEADS * HG_DV
HG_CHUNK = 64
RET_HEADS = 8
RET_DK = 64
RET_DV = 64
RET_KW = RET_HEADS * RET_DK
RET_VW = RET_HEADS * RET_DV
RET_CHUNK = 64
RET_ROPE_BASE = 10000.0
D_FF = 2816
RMS_EPS = 1e-6
IN_SIZES = (A_WIDTH, A_WIDTH, A_WIDTH, HG_KW, HG_KW, HG_VW, HG_VW, RET_KW, RET_KW, RET_VW, RET_VW, D_MODEL, D_MODEL, D_MODEL)
IN_COLS = sum(IN_SIZES)

kernel_name = 'moba_hgrn2_retention_macaron_step'


def _rmsnorm(x, gain=None):
    xf = x.astype(jnp.float32)
    y = xf * lax.rsqrt(jnp.mean(xf * xf, axis=-1, keepdims=True) + RMS_EPS)
    if gain is not None:
        y = y * gain.astype(jnp.float32)
    return y.astype(x.dtype)


def _swiglu_half(x, norm_g, w_in, w_out):
    h = _rmsnorm(x, norm_g)
    gate, up = jnp.split(h @ w_in, 2, axis=-1)
    return x + 0.5 * ((jax.nn.silu(gate) * up) @ w_out)


def _to_chunks(a, c):
    b, t, h, d = a.shape
    return a.astype(jnp.float32).reshape(b, t // c, c, h, d).transpose(1, 0, 3, 2, 4)


def _from_chunks(o):
    n, b, h, c, d = o.shape
    return o.transpose(1, 0, 3, 2, 4).reshape(b, n * c, h, d)


def _rotary(x, pos):
    d = x.shape[-1]
    theta = 1.0 / (RET_ROPE_BASE ** jnp.linspace(0.0, 1.0, d // 2, dtype=jnp.float32))
    ang = pos.astype(jnp.float32)[:, None] * jnp.repeat(theta, 2)[None, :]
    cos = jnp.cos(ang)[None, :, None, :]
    sin = jnp.sin(ang)[None, :, None, :]
    xf = x.astype(jnp.float32)
    rot = jnp.stack([-xf[..., 1::2], xf[..., 0::2]], axis=-1).reshape(xf.shape)
    return (xf * cos + rot * sin).astype(x.dtype)


def _moba_attention(q, k, v, pos0):
    B, T, H, dh = q.shape
    L = k.shape[1]
    nb = -(-L // MOBA_BLOCK)
    pad = nb * MOBA_BLOCK - L
    kb = jnp.pad(k, ((0, 0), (0, pad), (0, 0), (0, 0))).reshape(B, nb, MOBA_BLOCK, H, dh).transpose(0, 3, 1, 2, 4)
    vb = jnp.pad(v, ((0, 0), (0, pad), (0, 0), (0, 0))).reshape(B, nb, MOBA_BLOCK, H, dh).transpose(0, 3, 1, 2, 4)
    kmean = jnp.mean(kb.astype(jnp.float32), axis=3)
    n_sel = min(MOBA_TOPK, nb)
    qc = math.gcd(T, MOBA_QCHUNK)
    n_chunks = T // qc
    scale = dh ** -0.5
    bi = jnp.arange(B)[:, None, None, None]
    hi = jnp.arange(H)[None, :, None, None]
    blk_ids = jnp.arange(nb, dtype=jnp.int32)
    offs = jnp.arange(MOBA_BLOCK, dtype=jnp.int32)

    def chunk(args):
        qq, pos = args
        qh = qq.transpose(0, 2, 1, 3)
        blk_q = pos // MOBA_BLOCK
        gate = jnp.einsum('bhqd,bhnd->bhqn', qh.astype(jnp.float32), kmean)
        gate = jnp.where(blk_ids[None, :] < blk_q[:, None], gate, -jnp.inf)
        top_val, top_idx = lax.top_k(gate, n_sel)
        own = jnp.broadcast_to(blk_q[None, None, :, None], (B, H, qc, 1))
        idx = jnp.concatenate([top_idx.astype(jnp.int32), own], axis=-1)
        ok = jnp.concatenate([jnp.isfinite(top_val), jnp.ones((B, H, qc, 1), bool)], axis=-1)
        kg = kb[bi, hi, idx]
        vg = vb[bi, hi, idx]
        s = jnp.einsum('bhqd,bhqnkd->bhqnk', qh, kg).astype(jnp.float32) * scale
        kpos = idx[..., None] * MOBA_BLOCK + offs
        mask = ok[..., None] & (kpos <= pos[None, None, :, None, None])
        s = jnp.where(mask, s, -jnp.inf)
        p = jax.nn.softmax(s.reshape(B, H, qc, -1), axis=-1).reshape(s.shape)
        o = jnp.einsum('bhqnk,bhqnkd->bhqd', p.astype(vg.dtype), vg)
        return o.transpose(0, 2, 1, 3)

    q_chunks = q.reshape(B, n_chunks, qc, H, dh).transpose(1, 0, 2, 3, 4)
    pos_chunks = (pos0 + jnp.arange(T, dtype=jnp.int32)).reshape(n_chunks, qc)
    o = lax.map(chunk, (q_chunks, pos_chunks))
    return o.transpose(1, 0, 2, 3, 4).reshape(B, T, H, dh)


def _hgrn2_scan(q, k, v, logf, s0):
    T = q.shape[1]
    C = math.gcd(T, HG_CHUNK)
    tri = jnp.tril(jnp.ones((C, C), bool))[:, :, None]

    def step(S, inp):
        qc, kc, vc, lf = inp
        b = jnp.cumsum(lf, axis=2)
        o_inter = jnp.einsum('bhtk,bhkv->bhtv', qc * jnp.exp(b), S)
        diff = b[:, :, :, None, :] - b[:, :, None, :, :]
        decay = jnp.exp(jnp.where(tri, diff, -jnp.inf))
        a = jnp.einsum('bhtk,bhsk,bhtsk->bhts', qc, kc, decay)
        o = o_inter + jnp.einsum('bhts,bhsv->bhtv', a, vc)
        b_last = b[:, :, -1:, :]
        S = jnp.exp(b_last[:, :, 0, :])[..., None] * S + jnp.einsum('bhsk,bhsv->bhkv', kc * jnp.exp(b_last - b), vc)
        return S, o

    S, o = lax.scan(step, s0.astype(jnp.float32), (_to_chunks(q, C), _to_chunks(k, C), _to_chunks(v, C), _to_chunks(logf, C)))
    return _from_chunks(o), S


def _retention_scan(q, k, v, s0):
    T, H = q.shape[1], q.shape[2]
    C = math.gcd(T, RET_CHUNK)
    lg = jnp.log1p(-jnp.exp2(-5.0 - jnp.arange(H, dtype=jnp.float32)))
    t = jnp.arange(C, dtype=jnp.float32)
    q_decay = jnp.exp((t + 1.0)[None, :] * lg[:, None])
    rel = t[:, None] - t[None, :]
    intra = jnp.where(rel[None] >= 0, jnp.exp(jnp.maximum(rel, 0.0)[None] * lg[:, None, None]), 0.0)
    k_decay = jnp.exp((C - 1.0 - t)[None, :] * lg[:, None])
    chunk_decay = jnp.exp(C * lg)

    def step(S, inp):
        qc, kc, vc = inp
        o = jnp.einsum('bhtk,bhkv->bhtv', qc * q_decay[:, :, None], S)
        o = o + jnp.einsum('bhts,bhsv->bhtv', jnp.einsum('bhtk,bhsk->bhts', qc, kc) * intra, vc)
        S = chunk_decay[:, None, None] * S + jnp.einsum('bhsk,bhsv->bhkv', kc * k_decay[:, :, None], vc)
        return S, o

    S, o = lax.scan(step, s0.astype(jnp.float32), (_to_chunks(q, C), _to_chunks(k, C), _to_chunks(v, C)))
    return _from_chunks(o), S


def _mixer(h, pos0, k_past, v_past, s_hg, s_ret, lb, w_in, q_gain, k_gain, hg_gain, wb_a, wb_b, wb_c, w_out):
    B, T, _ = h.shape
    f32 = jnp.float32
    splits = np.cumsum(IN_SIZES)[:-1].tolist()
    aq, ak, av, hq, hf, hin, hg, rq, rk, rv, rg, ga, gb, gc = jnp.split(h @ w_in, splits, axis=-1)
    pos = pos0 + jnp.arange(T, dtype=jnp.int32)
    aq = _rmsnorm(aq.reshape(B, T, A_HEADS, A_HEAD_DIM), q_gain)
    ak = _rmsnorm(ak.reshape(B, T, A_HEADS, A_HEAD_DIM), k_gain)
    av = av.reshape(B, T, A_HEADS, A_HEAD_DIM)
    if k_past is None:
        k_full, v_full = ak, av
    else:
        k_full = jnp.concatenate([k_past.astype(ak.dtype), ak], axis=1)
        v_full = jnp.concatenate([v_past.astype(av.dtype), av], axis=1)
    o_a = _moba_attention(aq, k_full, v_full, pos0).reshape(B, T, A_WIDTH)
    fr = hf.astype(f32)
    logf = jnp.logaddexp(jnp.log(lb), jnp.log1p(-lb) + jax.nn.log_sigmoid(fr))
    kk = (1.0 - lb) * jax.nn.sigmoid(-fr)
    o_b, s_hg_new = _hgrn2_scan(hq.reshape(B, T, HG_HEADS, HG_DK), kk.reshape(B, T, HG_HEADS, HG_DK), hin.reshape(B, T, HG_HEADS, HG_DV), logf.reshape(B, T, HG_HEADS, HG_DK), s_hg)
    o_b = (_rmsnorm(o_b.astype(h.dtype), hg_gain) * jax.nn.silu(hg.reshape(B, T, HG_HEADS, HG_DV))).reshape(B, T, HG_VW)
    rq_ = _rotary(rq.reshape(B, T, RET_HEADS, RET_DK), pos)
    rk_ = _rotary(rk.reshape(B, T, RET_HEADS, RET_DK), pos) * (RET_DK ** -0.5)
    o_c, s_ret_new = _retention_scan(rq_, rk_, rv.reshape(B, T, RET_HEADS, RET_DV), s_ret)
    o_c = (_rmsnorm(o_c.astype(h.dtype)) * jax.nn.silu(rg.reshape(B, T, RET_HEADS, RET_DV))).reshape(B, T, RET_VW)
    merged = jax.nn.sigmoid(ga) * (o_a @ wb_a) + jax.nn.sigmoid(gb) * (o_b @ wb_b) + jax.nn.sigmoid(gc) * (o_c @ wb_c)
    return merged @ w_out, ak, av, s_hg_new, s_ret_new


def _layer(x, pos0, k_past, v_past, s_hg, s_ret, lb, ffn1_norm, ffn1_w_in, ffn1_w_out, mix_norm, w_in, q_gain, k_gain, hg_gain, wb_a, wb_b, wb_c, w_out, ffn2_norm, ffn2_w_in, ffn2_w_out):
    x = _swiglu_half(x, ffn1_norm, ffn1_w_in, ffn1_w_out)
    y, k_new, v_new, s_hg_new, s_ret_new = _mixer(_rmsnorm(x, mix_norm), pos0, k_past, v_past, s_hg, s_ret, lb, w_in, q_gain, k_gain, hg_gain, wb_a, wb_b, wb_c, w_out)
    x = x + y
    x = _swiglu_half(x, ffn2_norm, ffn2_w_in, ffn2_w_out)
    return x, k_new, v_new, s_hg_new, s_ret_new


def setup_inputs(seed: int = 0) -> dict:
    key = jax.random.key(seed)
    ks = jax.random.split(key, 24)
    f32 = jnp.float32
    n_pages = PAST_LEN // PAGE_SIZE
    n_pool = (5 * DEC_BATCH * n_pages + 3) // 4

    def nrm(k, shape, scale=1.0):
        return scale * jax.random.normal(k, shape, f32)

    def gain(k, shape):
        return 1.0 + 0.02 * jax.random.normal(k, shape, f32)

    page_table = jax.random.permutation(ks[6], n_pool)[: DEC_BATCH * n_pages].reshape(DEC_BATCH, n_pages).astype(jnp.int32)
    return {
        'x_prompt': nrm(ks[0], (BATCH, SEQ, D_MODEL)),
        'x_sample': nrm(ks[1], (DEC_BATCH, DEC_SEQ, D_MODEL)),
        'cache_k': nrm(ks[2], (DEPTH, n_pool, PAGE_SIZE, A_HEADS, A_HEAD_DIM)),
        'cache_v': nrm(ks[3], (DEPTH, n_pool, PAGE_SIZE, A_HEADS, A_HEAD_DIM)),
        'state_hgrn': nrm(ks[4], (DEPTH, DEC_BATCH, HG_HEADS, HG_DK, HG_DV)),
        'state_ret': nrm(ks[5], (DEPTH, DEC_BATCH, RET_HEADS, RET_DK, RET_DV)),
        'page_table': page_table,
        'ffn1_norm': gain(ks[7], (DEPTH, D_MODEL)),
        'ffn1_w_in': nrm(ks[8], (DEPTH, D_MODEL, 2 * D_FF), D_MODEL ** -0.5),
        'ffn1_w_out': nrm(ks[9], (DEPTH, D_FF, D_MODEL), D_FF ** -0.5),
        'mix_norm': gain(ks[10], (DEPTH, D_MODEL)),
        'w_in': nrm(ks[11], (DEPTH, D_MODEL, IN_COLS), D_MODEL ** -0.5),
        'moba_q_gain': gain(ks[12], (DEPTH, A_HEAD_DIM)),
        'moba_k_gain': gain(ks[13], (DEPTH, A_HEAD_DIM)),
        'hgrn_lb': nrm(ks[14], (DEPTH, HG_KW), 0.1),
        'hgrn_o_gain': gain(ks[15], (DEPTH, HG_DV)),
        'w_branch_a': nrm(ks[16], (DEPTH, A_WIDTH, D_MODEL), A_WIDTH ** -0.5),
        'w_branch_b': nrm(ks[17], (DEPTH, HG_VW, D_MODEL), HG_VW ** -0.5),
        'w_branch_c': nrm(ks[18], (DEPTH, RET_VW, D_MODEL), RET_VW ** -0.5),
        'w_out': nrm(ks[19], (DEPTH, D_MODEL, D_MODEL), D_MODEL ** -0.5),
        'ffn2_norm': gain(ks[20], (DEPTH, D_MODEL)),
        'ffn2_w_in': nrm(ks[21], (DEPTH, D_MODEL, 2 * D_FF), D_MODEL ** -0.5),
        'ffn2_w_out': nrm(ks[22], (DEPTH, D_FF, D_MODEL), D_FF ** -0.5),
    }


def reference(x_prompt, x_sample, cache_k, cache_v, state_hgrn, state_ret, page_table, ffn1_norm, ffn1_w_in, ffn1_w_out, mix_norm, w_in, moba_q_gain, moba_k_gain, hgrn_lb, hgrn_o_gain, w_branch_a, w_branch_b, w_branch_c, w_out, ffn2_norm, ffn2_w_in, ffn2_w_out):
    f32 = jnp.float32
    lb_cum = jnp.cumsum(jax.nn.softmax(hgrn_lb.astype(f32), axis=0), axis=0)
    lb_all = lb_cum - lb_cum[:1]
    n_pages = page_table.shape[1]
    past_len = n_pages * PAGE_SIZE
    bp = x_prompt.shape[0]
    db = x_sample.shape[0]
    xp, xs = x_prompt, x_sample
    kp_l, vp_l, ks_l, vs_l, hp_l, hs_l, rp_l, rs_l = [], [], [], [], [], [], [], []
    for l in range(DEPTH):
        w = (ffn1_norm[l], ffn1_w_in[l], ffn1_w_out[l], mix_norm[l], w_in[l], moba_q_gain[l], moba_k_gain[l], hgrn_o_gain[l], w_branch_a[l], w_branch_b[l], w_branch_c[l], w_out[l], ffn2_norm[l], ffn2_w_in[l], ffn2_w_out[l])
        xp, kp, vp, hp, rp = _layer(xp, 0, None, None, jnp.zeros((bp, HG_HEADS, HG_DK, HG_DV), f32), jnp.zeros((bp, RET_HEADS, RET_DK, RET_DV), f32), lb_all[l], *w)
        k_past = cache_k[l, page_table].reshape(db, past_len, A_HEADS, A_HEAD_DIM)
        v_past = cache_v[l, page_table].reshape(db, past_len, A_HEADS, A_HEAD_DIM)
        xs, kS, vS, hS, rS = _layer(xs, past_len, k_past, v_past, state_hgrn[l], state_ret[l], lb_all[l], *w)
        kp_l.append(kp); vp_l.append(vp); ks_l.append(kS); vs_l.append(vS)
        hp_l.append(hp); hs_l.append(hS); rp_l.append(rp); rs_l.append(rS)
    return (xp, xs, jnp.stack(kp_l), jnp.stack(vp_l), jnp.stack(ks_l), jnp.stack(vs_l), jnp.stack(hp_l), jnp.stack(hs_l), jnp.stack(rp_l), jnp.stack(rs_l))
```

```python
import functools
import math

import numpy as np
import jax
import jax.numpy as jnp
from jax import lax
from jax.experimental import pallas as pl
from jax.experimental.pallas import tpu as pltpu

F32 = jnp.float32
BF16 = jnp.bfloat16

D_MODEL = 1024
D_FF = 2816
DEPTH = 4
RMS_EPS = 1e-6
A_HEADS = 8
A_HEAD_DIM = 64
MOBA_BLOCK = 256
MOBA_TOPK = 3
HG_HEADS = 4
HG_D = 128
RET_HEADS = 8
RET_D = 64
RET_ROPE_BASE = 10000.0
PAGE_SIZE = 128
GROUP = 512

P_Q, P_K, P_V, P_HQ, P_LOGF, P_KK, P_HV, P_HG, P_RQ, P_RK, P_RV, P_RG, P_GATES = range(13)
P_GROUPS = 18

HG_CHUNK = 64
RET_CHUNK = 256

VMEM_LIMIT = 56 * 1024 * 1024


def _dot(a, b):
    return jnp.dot(a, b, preferred_element_type=F32)


def _dot_nt(a, b):
    return lax.dot_general(a, b, (((1,), (1,)), ((), ())), preferred_element_type=F32)


def _split_bf16(a):
    hi = a.astype(BF16)
    lo = (a - hi.astype(F32)).astype(BF16)
    return hi, lo


def _params(*sem):
    return pltpu.CompilerParams(dimension_semantics=sem, vmem_limit_bytes=VMEM_LIMIT)


def _const_spec(shape):
    nd = len(shape)
    return pl.BlockSpec(shape, lambda *_: (0,) * nd)


def _ffn_body(x_ref, g_ref, w1_ref, w2_ref, o_ref, acc_ref, *, ff_chunk):
    x = x_ref[...]
    ms = jnp.mean(x * x, axis=-1, keepdims=True)
    h = (x * lax.rsqrt(ms + RMS_EPS) * g_ref[...]).astype(BF16)
    for c in range(D_FF // ff_chunk):
        lo = c * ff_chunk
        gate = _dot(h, w1_ref[:, lo:lo + ff_chunk])
        up = _dot(h, w1_ref[:, D_FF + lo:D_FF + lo + ff_chunk])
        act = (gate * jax.nn.sigmoid(gate) * up).astype(BF16)
        part = _dot(act, w2_ref[lo:lo + ff_chunk, :])
        if c == 0:
            acc_ref[...] = part
        else:
            acc_ref[...] += part
    o_ref[...] = x + 0.5 * acc_ref[...]


def _ffn(x, gain, w1, w2, *, tm):
    n = x.shape[0]
    return pl.pallas_call(
        functools.partial(_ffn_body, ff_chunk=256),
        grid=(n // tm,),
        in_specs=[
            pl.BlockSpec((tm, D_MODEL), lambda i: (i, 0)),
            _const_spec((1, D_MODEL)),
            pl.BlockSpec((D_MODEL, 2 * D_FF), lambda i: (0, 0), pipeline_mode=pl.Buffered(1)),
            pl.BlockSpec((D_FF, D_MODEL), lambda i: (0, 0), pipeline_mode=pl.Buffered(1)),
        ],
        out_specs=pl.BlockSpec((tm, D_MODEL), lambda i: (i, 0)),
        out_shape=jax.ShapeDtypeStruct((n, D_MODEL), F32),
        scratch_shapes=[pltpu.VMEM((tm, D_MODEL), F32)],
        compiler_params=_params("parallel"),
        name="ffn",
    )(x, gain, w1, w2)


def _proj_body(x_ref, g_ref, w_ref, qg_ref, kg_ref, lb_ref, cos_ref, sa_ref, sb_ref, bd_ref,
               o_ref, h_ref, acc_ref, kk_ref):
    j = pl.program_id(1)

    @pl.when(j == 0)
    def _():
        x = x_ref[...]
        ms = jnp.mean(x * x, axis=-1, keepdims=True)
        h_ref[...] = (x * lax.rsqrt(ms + RMS_EPS) * g_ref[...]).astype(BF16)

    @pl.when(j != P_KK)
    def _():
        acc_ref[...] = _dot(h_ref[...], w_ref[...])

    def headnorm(a, gain_row):
        ss = _dot((a * a).astype(BF16), bd_ref[...])
        return a * lax.rsqrt(ss * (1.0 / A_HEAD_DIM) + RMS_EPS) * gain_row

    def rotary(a, scale):
        cos, sa, sb = cos_ref[...], sa_ref[...], sb_ref[...]
        for s in range(GROUP // 128):
            xs = a[:, 128 * s:128 * s + 128]
            nxt = pltpu.roll(xs, 127, 1)
            prv = pltpu.roll(xs, 1, 1)
            r = xs * cos + nxt * sa + prv * sb
            o_ref[:, 128 * s:128 * s + 128] = r * scale if scale != 1.0 else r

    @pl.when(j == P_Q)
    def _():
        o_ref[...] = headnorm(acc_ref[...], qg_ref[...])

    @pl.when(j == P_K)
    def _():
        o_ref[...] = headnorm(acc_ref[...], kg_ref[...])

    @pl.when((j == P_V) | (j == P_HQ) | (j == P_HV) | (j == P_RV))
    def _():
        o_ref[...] = acc_ref[...]

    @pl.when(j == P_LOGF)
    def _():
        fr = acc_ref[...]
        lb = lb_ref[...]
        log_sig = jnp.minimum(fr, 0.0) - jnp.log1p(jnp.exp(-jnp.abs(fr)))
        a = jnp.log(lb)
        c = jnp.log1p(-lb) + log_sig
        o_ref[...] = jnp.maximum(a, c) + jnp.log1p(jnp.exp(-jnp.abs(a - c)))
        kk_ref[...] = (1.0 - lb) * jax.nn.sigmoid(-fr)

    @pl.when(j == P_KK)
    def _():
        o_ref[...] = kk_ref[...]

    @pl.when((j == P_HG) | (j == P_RG))
    def _():
        a = acc_ref[...]
        o_ref[...] = a * jax.nn.sigmoid(a)

    @pl.when(j == P_RQ)
    def _():
        rotary(acc_ref[...], 1.0)

    @pl.when(j == P_RK)
    def _():
        rotary(acc_ref[...], RET_D ** -0.5)

    @pl.when(j >= P_GATES)
    def _():
        o_ref[...] = jax.nn.sigmoid(acc_ref[...])


def _proj(x, gain, w, qg, kg, lb, cos, sa, sb, bd, *, tm):
    n = x.shape[0]
    n_tab = cos.shape[0] // tm
    tab = pl.BlockSpec((tm, 128), lambda i, j: (i % n_tab, 0))
    return pl.pallas_call(
        _proj_body,
        grid=(n // tm, P_GROUPS),
        in_specs=[
            pl.BlockSpec((tm, D_MODEL), lambda i, j: (i, 0)),
            _const_spec((1, D_MODEL)),
            pl.BlockSpec((D_MODEL, GROUP), lambda i, j: (0, jnp.where(j >= P_KK, j - 1, j))),
            _const_spec((1, GROUP)),
            _const_spec((1, GROUP)),
            _const_spec((1, GROUP)),
            tab, tab, tab,
            _const_spec((GROUP, GROUP)),
        ],
        out_specs=pl.BlockSpec((tm, GROUP), lambda i, j: (i, j)),
        out_shape=jax.ShapeDtypeStruct((n, P_GROUPS * GROUP), F32),
        scratch_shapes=[
            pltpu.VMEM((tm, D_MODEL), BF16),
            pltpu.VMEM((tm, GROUP), F32),
            pltpu.VMEM((tm, GROUP), F32),
        ],
        compiler_params=_params("parallel", "arbitrary"),
        name="proj",
    )(x, gain, w, qg, kg, lb, cos, sa, sb, bd)


def _merge_body(oa_ref, ob_ref, oc_ref, ga_ref, gb_ref, gc_ref, x_ref, wa_ref, wb_ref, wc_ref, wo_ref, o_ref):
    m = ga_ref[...] * _dot(oa_ref[...].astype(BF16), wa_ref[...])
    m = m + gb_ref[...] * _dot(ob_ref[...].astype(BF16), wb_ref[...])
    m = m + gc_ref[...] * _dot(oc_ref[...].astype(BF16), wc_ref[...])
    o_ref[...] = x_ref[...] + _dot(m.astype(BF16), wo_ref[...])


def _merge(oa, ob, oc, p, x, wa, wb, wc, wo, *, tm):
    n = x.shape[0]
    row = lambda i: (i, 0)
    g0 = P_GATES // 2
    return pl.pallas_call(
        _merge_body,
        grid=(n // tm,),
        in_specs=[
            pl.BlockSpec((tm, GROUP), row),
            pl.BlockSpec((tm, GROUP), row),
            pl.BlockSpec((tm, GROUP), row),
            pl.BlockSpec((tm, D_MODEL), lambda i: (i, g0)),
            pl.BlockSpec((tm, D_MODEL), lambda i: (i, g0 + 1)),
            pl.BlockSpec((tm, D_MODEL), lambda i: (i, g0 + 2)),
            pl.BlockSpec((tm, D_MODEL), row),
            _const_spec((GROUP, D_MODEL)),
            _const_spec((GROUP, D_MODEL)),
            _const_spec((GROUP, D_MODEL)),
            _const_spec((D_MODEL, D_MODEL)),
        ],
        out_specs=pl.BlockSpec((tm, D_MODEL), row),
        out_shape=jax.ShapeDtypeStruct((n, D_MODEL), F32),
        compiler_params=_params("parallel"),
        name="merge",
    )(oa, ob, oc, p, p, p, x, wa, wb, wc, wo)


def _moba_body(q_ref, k_ref, v_ref, o_ref, k16, vt16, kmt, sel_ref, ot_ref, *, nb):
    i = pl.program_id(1)
    blk = MOBA_BLOCK

    @pl.when(i == 0)
    def _():
        lane = lax.broadcasted_iota(jnp.int32, (A_HEADS, GROUP), 1)
        hrow = lax.broadcasted_iota(jnp.int32, (A_HEADS, GROUP), 0)
        headmask = (lax.shift_right_logical(lane, 6) == hrow).astype(F32)

        def init(jj, c):
            r = pl.multiple_of(jj * blk, blk)
            kb = k_ref[pl.ds(r, blk), :]
            k16[jj] = kb.astype(BF16)
            km = jnp.sum(kb, axis=0, keepdims=True) * (1.0 / blk)
            kmt[pl.ds(pl.multiple_of(jj * A_HEADS, A_HEADS), A_HEADS), :] = km * headmask
            vt16[jj] = v_ref[pl.ds(r, blk), :].T.astype(BF16)
            return c

        lax.fori_loop(0, nb, init, 0)

    q = q_ref[...]
    kh, kl = _split_bf16(kmt[...])
    qh, ql = _split_bf16(q)
    gt = _dot_nt(kh, qh) + _dot_nt(kh, ql) + _dot_nt(kl, qh)
    neg_inf = jnp.float32(-jnp.inf)
    g = [jnp.where(j < i, gt[A_HEADS * j:A_HEADS * (j + 1), :], neg_inf) for j in range(nb)]
    for j in range(nb):
        rank = jnp.zeros((A_HEADS, blk), F32)
        for jp in range(nb):
            if jp == j:
                continue
            beats = (g[jp] >= g[j]) if jp < j else (g[jp] > g[j])
            rank = rank + beats.astype(F32)
        sel = (rank < MOBA_TOPK) & (jnp.abs(g[j]) < jnp.inf)
        sel_ref[j] = sel.astype(F32)

    qs = (q * (A_HEAD_DIM ** -0.5)).astype(BF16)
    lane128 = lax.broadcasted_iota(jnp.int32, (blk, 128), 1)
    kidx = lax.broadcasted_iota(jnp.int32, (blk, blk), 0)
    qidx = lax.broadcasted_iota(jnp.int32, (blk, blk), 1)
    causal = kidx <= qidx
    masked = jnp.float32(-1e30)
    for h in range(A_HEADS):
        p2 = h // 2
        cols = slice(128 * p2, 128 * p2 + 128)
        in_head = (lane128 >= 64) if h % 2 else (lane128 < 64)
        qm = jnp.where(in_head, qs[:, cols], jnp.zeros_like(qs[:, cols]))
        rows = slice(A_HEAD_DIM * h, A_HEAD_DIM * (h + 1))

        st = jnp.where(causal, _dot_nt(k16[i, :, cols], qm), masked)
        m0 = jnp.max(st, axis=0, keepdims=True)
        pt = jnp.exp(st - m0)
        l0 = jnp.sum(pt, axis=0, keepdims=True)
        acc0 = _dot(vt16[i, rows, :], pt.astype(BF16))

        def body(j, carry, cols=cols, rows=rows, qm=qm, h=h):
            m, l, acc = carry
            st = _dot_nt(k16[j, :, cols], qm)
            st = jnp.where(sel_ref[j, h:h + 1, :] > 0.0, st, masked)
            m_new = jnp.maximum(m, jnp.max(st, axis=0, keepdims=True))
            alpha = jnp.exp(m - m_new)
            pt = jnp.exp(st - m_new)
            l = alpha * l + jnp.sum(pt, axis=0, keepdims=True)
            acc = alpha * acc + _dot(vt16[j, rows, :], pt.astype(BF16))
            return m_new, l, acc

        _, l, acc = lax.fori_loop(0, i, body, (m0, l0, acc0))
        ot_ref[rows, :] = acc / l

    o_ref[...] = ot_ref[...].T.astype(BF16)


def _moba_prompt(p, *, batch, seq):
    nb = seq // MOBA_BLOCK
    n = batch * seq
    return pl.pallas_call(
        functools.partial(_moba_body, nb=nb),
        grid=(batch, nb),
        in_specs=[
            pl.BlockSpec((MOBA_BLOCK, GROUP), lambda b, i: (b * nb + i, P_Q)),
            pl.BlockSpec((seq, GROUP), lambda b, i: (b, P_K), pipeline_mode=pl.Buffered(1)),
            pl.BlockSpec((seq, GROUP), lambda b, i: (b, P_V), pipeline_mode=pl.Buffered(1)),
        ],
        out_specs=pl.BlockSpec((MOBA_BLOCK, GROUP), lambda b, i: (b * nb + i, 0)),
        out_shape=jax.ShapeDtypeStruct((n, GROUP), BF16),
        scratch_shapes=[
            pltpu.VMEM((nb, MOBA_BLOCK, GROUP), BF16),
            pltpu.VMEM((nb, GROUP, MOBA_BLOCK), BF16),
            pltpu.VMEM((nb * A_HEADS, GROUP), F32),
            pltpu.VMEM((nb, A_HEADS, MOBA_BLOCK), F32),
            pltpu.VMEM((GROUP, MOBA_BLOCK), F32),
        ],
        compiler_params=_params("parallel", "arbitrary"),
        name="moba_prompt",
    )(p, p, p)


def _hgrn_tables():
    c = HG_CHUNK
    t = np.arange(c)[:, None]
    u = np.arange(c)[None, :]
    mats = [(u <= t), (u > t)]
    masks = [np.eye(c, dtype=bool)]
    m = c // 2
    while m >= 1:
        b0 = (t // (2 * m)) * (2 * m)
        right = t >= b0 + m
        mats.append(np.where(right, (u >= b0 + m) & (u <= t), (u > t) & (u <= b0 + m - 1)))
        s = u
        masks.append(((s // (2 * m)) == (t // (2 * m))) & right & (s < b0 + m))
        m //= 2
    return (np.concatenate(mats, 0).astype(np.float32), np.stack(masks).astype(np.float32))


def _hgrn_body(q_ref, lf_ref, kk_ref, v_ref, sg_ref, gain_ref, mall_ref, masks_ref, s0_ref,
               o_ref, so_ref, st_ref, ob_ref, *, n_chunks, nt):
    t = pl.program_id(1)
    c = HG_CHUNK
    n_levels = masks_ref.shape[0] - 1

    @pl.when(t == 0)
    def _():
        st_ref[...] = s0_ref[0]

    def chunk(ci, carry):
        r = pl.multiple_of(ci * c, c)
        lf = lf_ref[pl.ds(r, c), :]
        hi, lo = _split_bf16(lf)
        mall = mall_ref[...]
        e_all = _dot(mall, hi) + _dot(mall, lo)
        q = q_ref[pl.ds(r, c), :]
        k = kk_ref[pl.ds(r, c), :]
        v = v_ref[pl.ds(r, c), :]
        for h in range(HG_HEADS):
            sl = slice(HG_D * h, HG_D * (h + 1))
            qh, kh, vh = q[:, sl], k[:, sl], v[:, sl]
            e_cum = e_all[0:c, sl]
            e_rev = e_all[c:2 * c, sl]
            s_t = st_ref[h]
            o = _dot_nt((qh * jnp.exp(e_cum)).astype(BF16), s_t.astype(BF16))
            a = masks_ref[0] * _dot_nt(qh.astype(BF16), kh.astype(BF16))
            for lv in range(n_levels):
                e = jnp.exp(e_all[(2 + lv) * c:(3 + lv) * c, sl])
                a = a + masks_ref[lv + 1] * _dot_nt((qh * e).astype(BF16), (kh * e).astype(BF16))
            o = o + _dot(a.astype(BF16), vh.astype(BF16))
            ob_ref[pl.ds(r, c), sl] = o
            kd = (kh * jnp.exp(e_rev)).astype(BF16)
            st_ref[h] = jnp.exp(e_cum[c - 1:c, :]) * s_t + _dot(vh.T.astype(BF16), kd)
        return carry

    lax.fori_loop(0, n_chunks, chunk, 0)

    for h in range(HG_HEADS):
        sl = slice(HG_D * h, HG_D * (h + 1))
        x = ob_ref[:, sl]
        ms = jnp.mean(x * x, axis=-1, keepdims=True)
        o_ref[:, sl] = (x * lax.rsqrt(ms + RMS_EPS) * gain_ref[...] * sg_ref[:, sl]).astype(BF16)

    @pl.when(t == nt - 1)
    def _():
        for h in range(HG_HEADS):
            so_ref[0, h] = st_ref[h].T


def _hgrn_prompt(p, gain, s0t, *, batch, seq, tc):
    nt = seq // tc
    n = batch * seq
    mall, masks = _hgrn_tables()
    mall = jnp.asarray(mall, BF16)
    masks = jnp.asarray(masks, F32)
    col = lambda g: pl.BlockSpec((tc, GROUP), lambda b, t: (b * nt + t, g))
    st_spec = pl.BlockSpec((1, HG_HEADS, HG_D, HG_D), lambda b, t: (b, 0, 0, 0))
    return pl.pallas_call(
        functools.partial(_hgrn_body, n_chunks=tc // HG_CHUNK, nt=nt),
        grid=(batch, nt),
        in_specs=[col(P_HQ), col(P_LOGF), col(P_KK), col(P_HV), col(P_HG),
                  _const_spec((1, HG_D)), _const_spec(mall.shape), _const_spec(masks.shape), st_spec],
        out_specs=[pl.BlockSpec((tc, GROUP), lambda b, t: (b * nt + t, 0)), st_spec],
        out_shape=[jax.ShapeDtypeStruct((n, GROUP), BF16),
                   jax.ShapeDtypeStruct((batch, HG_HEADS, HG_D, HG_D), F32)],
        scratch_shapes=[pltpu.VMEM((HG_HEADS, HG_D, HG_D), F32), pltpu.VMEM((tc, GROUP), F32)],
        compiler_params=_params("parallel", "arbitrary"),
        name="hgrn_prompt",
    )(p, p, p, p, p, gain, mall, masks, s0t)


def _ret_tables():
    c = RET_CHUNK
    lg = jnp.log1p(-jnp.exp2(-5.0 - jnp.arange(RET_HEADS, dtype=F32)))
    t = jnp.arange(c, dtype=F32)
    rel = t[:, None] - t[None, :]
    intra = jnp.where(rel[None] >= 0, jnp.exp(jnp.maximum(rel, 0.0)[None] * lg[:, None, None]), 0.0)
    lane_lg = jnp.repeat(lg, RET_D).reshape(RET_HEADS // 2, 1, 128)
    qdec = jnp.exp((t + 1.0)[None, :, None] * lane_lg)
    kdec = jnp.exp((c - 1.0 - t)[None, :, None] * lane_lg)
    cdec = jnp.exp(c * lane_lg)
    return intra, qdec, kdec, cdec


def _ret_body(q_ref, k_ref, v_ref, sg_ref, intra_ref, qdec_ref, kdec_ref, cdec_ref, o_ref, so_ref, s_ref, *, nt):
    t = pl.program_id(1)

    @pl.when(t == 0)
    def _():
        s_ref[...] = jnp.zeros_like(s_ref)

    c = RET_CHUNK
    lane = lax.broadcasted_iota(jnp.int32, (c, 128), 1)
    low = lane < RET_D
    r128 = lax.broadcasted_iota(jnp.int32, (128, 128), 0)
    c128 = lax.broadcasted_iota(jnp.int32, (128, 128), 1)
    same_head = (r128 < RET_D) == (c128 < RET_D)
    for p2 in range(RET_HEADS // 2):
        cols = slice(128 * p2, 128 * p2 + 128)
        qp, kp, vp = q_ref[:, cols], k_ref[:, cols], v_ref[:, cols]
        kb, vb = kp.astype(BF16), vp.astype(BF16)
        s_bd = s_ref[p2]
        o = _dot((qp * qdec_ref[p2]).astype(BF16), s_bd.astype(BF16))
        for hh in range(2):
            in_head = low if hh == 0 else ~low
            qm = jnp.where(in_head, qp, 0.0).astype(BF16)
            a = _dot_nt(qm, kb) * intra_ref[2 * p2 + hh]
            o = o + jnp.where(in_head, _dot(a.astype(BF16), vb), 0.0)
        upd = _dot((kp * kdec_ref[p2]).T.astype(BF16), vb)
        s_ref[p2] = cdec_ref[p2] * s_bd + jnp.where(same_head, upd, 0.0)
        sq = o * o
        s_lo = jnp.sum(jnp.where(low, sq, 0.0), axis=-1, keepdims=True)
        s_hi = jnp.sum(jnp.where(low, 0.0, sq), axis=-1, keepdims=True)
        ms = jnp.where(low, s_lo, s_hi) * (1.0 / RET_D)
        o_ref[:, cols] = (o * lax.rsqrt(ms + RMS_EPS) * sg_ref[:, cols]).astype(BF16)

    @pl.when(t == nt - 1)
    def _():
        so_ref[0] = s_ref[...]


def _ret_prompt(p, *, batch, seq):
    c = RET_CHUNK
    nt = seq // c
    n = batch * seq
    intra, qdec, kdec, cdec = _ret_tables()
    col = lambda g: pl.BlockSpec((c, GROUP), lambda b, t: (b * nt + t, g))
    npair = RET_HEADS // 2
    return pl.pallas_call(
        functools.partial(_ret_body, nt=nt),
        grid=(batch, nt),
        in_specs=[col(P_RQ), col(P_RK), col(P_RV), col(P_RG),
                  _const_spec(intra.shape), _const_spec(qdec.shape), _const_spec(kdec.shape),
                  _const_spec(cdec.shape)],
        out_specs=[pl.BlockSpec((c, GROUP), lambda b, t: (b * nt + t, 0)),
                   pl.BlockSpec((1, npair, 128, 128), lambda b, t: (b, 0, 0, 0))],
        out_shape=[jax.ShapeDtypeStruct((n, GROUP), BF16),
                   jax.ShapeDtypeStruct((batch, npair, 128, 128), F32)],
        scratch_shapes=[pltpu.VMEM((npair, 128, 128), F32)],
        compiler_params=_params("parallel", "arbitrary"),
        name="ret_prompt",
    )(p, p, p, p, intra, qdec, kdec, cdec)


def _unpair_ret_state(s_bd):
    b = s_bd.shape[0]
    lo = s_bd[:, :, :RET_D, :RET_D]
    hi = s_bd[:, :, RET_D:, RET_D:]
    return jnp.stack([lo, hi], axis=2).reshape(b, RET_HEADS, RET_D, RET_D)


SAMPLE_PAGES_PER_STEP = 16


def _pages_feature_major(cache):
    d, n_pool = cache.shape[:2]
    return jnp.transpose(cache, (0, 1, 3, 4, 2)).reshape(d, n_pool, GROUP, PAGE_SIZE)


def _page_specs(layer):
    def spec(r):
        return pl.BlockSpec((None, None, GROUP, PAGE_SIZE),
                            lambda b, g, pt: (layer, pt[b, g * SAMPLE_PAGES_PER_STEP + r], 0, 0))
    return [spec(r) for r in range(SAMPLE_PAGES_PER_STEP)]


def _head_rows(row):
    lane = lax.broadcasted_iota(jnp.int32, (A_HEADS, GROUP), 1)
    hrow = lax.broadcasted_iota(jnp.int32, (A_HEADS, GROUP), 0)
    return jnp.where(lax.shift_right_logical(lane, 6) == hrow, jnp.broadcast_to(row, (A_HEADS, GROUP)), 0.0)


def _moba_scores_body(pt_ref, q_ref, kn_ref, *rest, n_steps, n_blocks):
    pages = rest[:SAMPLE_PAGES_PER_STEP]
    prob_ref, pself_ref, s_ref = rest[SAMPLE_PAGES_PER_STEP:]
    g = pl.program_id(1)
    qm = _head_rows(q_ref[...])
    q_hi, q_lo = _split_bf16(qm)
    q_both = jnp.concatenate([q_hi.astype(F32), q_lo.astype(F32)], axis=0).astype(BF16)
    per_blk = MOBA_BLOCK // PAGE_SIZE
    scale = A_HEAD_DIM ** -0.5
    for r in range(SAMPLE_PAGES_PER_STEP):
        k_hi, k_lo = _split_bf16(pages[r][...])
        s2 = _dot(q_both, k_hi)
        s_ref[g * SAMPLE_PAGES_PER_STEP + r] = s2[0:A_HEADS] + s2[A_HEADS:] + _dot(q_hi, k_lo)

    @pl.when(g == n_steps - 1)
    def _():
        jidx = lax.broadcasted_iota(jnp.int32, (A_HEADS, n_blocks), 1)
        gt = jnp.zeros((A_HEADS, n_blocks), F32)
        for j in range(n_blocks):
            tot = s_ref[per_blk * j]
            for r in range(1, per_blk):
                tot = tot + s_ref[per_blk * j + r]
            gj = jnp.sum(tot, axis=-1, keepdims=True) * (1.0 / MOBA_BLOCK)
            gt = jnp.where(jidx == j, gj, gt)
        rank = jnp.zeros((A_HEADS, n_blocks), F32)
        for jp in range(n_blocks):
            cj = gt[:, jp:jp + 1]
            beats = (cj > gt) | ((cj == gt) & (jidx > jp))
            rank = rank + beats.astype(F32)
        sel = ((rank < MOBA_TOPK) & (jnp.abs(gt) < jnp.inf)).astype(F32)
        s_self = jnp.sum(qm * scale * kn_ref[...], axis=-1, keepdims=True)
        masked = jnp.float32(-1e30)
        n_pages = n_blocks * per_blk
        m = s_self
        for pg in range(n_pages):
            keep = sel[:, pg // per_blk:pg // per_blk + 1] > 0.0
            m = jnp.maximum(m, jnp.max(jnp.where(keep, s_ref[pg] * scale, masked), axis=-1, keepdims=True))
        e_self = jnp.exp(s_self - m)
        l = e_self
        for pg in range(n_pages):
            keep = sel[:, pg // per_blk:pg // per_blk + 1] > 0.0
            e = jnp.exp(jnp.where(keep, s_ref[pg] * scale, masked) - m)
            prob_ref[0, pg] = e
            l = l + jnp.sum(e, axis=-1, keepdims=True)
        inv = 1.0 / l
        for pg in range(n_pages):
            prob_ref[0, pg] = prob_ref[0, pg] * inv
        pself_ref[0] = jnp.broadcast_to(e_self * inv, (A_HEADS, 128))


def _moba_sample_scores(page_table, p3, cache_k, *, layer):
    db, n_pages = page_table.shape
    n_steps = n_pages // SAMPLE_PAGES_PER_STEP
    n_blocks = n_pages * PAGE_SIZE // MOBA_BLOCK
    row = lambda g_: pl.BlockSpec((None, 1, GROUP), lambda b, g, pt: (b, 0, g_))
    grid_spec = pltpu.PrefetchScalarGridSpec(
        num_scalar_prefetch=1,
        grid=(db, n_steps),
        in_specs=[row(P_Q), row(P_K)] + _page_specs(layer),
        out_specs=[pl.BlockSpec((1, n_pages, A_HEADS, PAGE_SIZE), lambda b, g, pt: (b, 0, 0, 0)),
                   pl.BlockSpec((1, A_HEADS, 128), lambda b, g, pt: (b, 0, 0))],
        scratch_shapes=[pltpu.VMEM((n_pages, A_HEADS, PAGE_SIZE), F32)],
    )
    return pl.pallas_call(
        functools.partial(_moba_scores_body, n_steps=n_steps, n_blocks=n_blocks),
        grid_spec=grid_spec,
        out_shape=[jax.ShapeDtypeStruct((db, n_pages, A_HEADS, PAGE_SIZE), F32),
                   jax.ShapeDtypeStruct((db, A_HEADS, 128), F32)],
        compiler_params=_params("parallel", "arbitrary"),
        name="moba_sample_scores",
    )(page_table, p3, p3, *([cache_k] * SAMPLE_PAGES_PER_STEP))


def _moba_out_body(pt_ref, prob_ref, pself_ref, vn_ref, *rest, n_steps):
    pages = rest[:SAMPLE_PAGES_PER_STEP]
    o_ref, acc_ref = rest[SAMPLE_PAGES_PER_STEP:]
    g = pl.program_id(1)

    @pl.when(g == 0)
    def _():
        acc_ref[...] = jnp.zeros_like(acc_ref)

    acc = acc_ref[...]
    for r in range(SAMPLE_PAGES_PER_STEP):
        pr = prob_ref[0, g * SAMPLE_PAGES_PER_STEP + r]
        acc = acc + _dot_nt(pr.astype(BF16), pages[r][...].astype(BF16))
    acc_ref[...] = acc

    @pl.when(g == n_steps - 1)
    def _():
        full = acc_ref[...] + pself_ref[0][:, 0:1] * vn_ref[...]
        lane = lax.broadcasted_iota(jnp.int32, (A_HEADS, GROUP), 1)
        hrow = lax.broadcasted_iota(jnp.int32, (A_HEADS, GROUP), 0)
        diag = jnp.where(lax.shift_right_logical(lane, 6) == hrow, full, 0.0)
        o_ref[...] = jnp.sum(diag, axis=0, keepdims=True)


def _moba_sample_out(page_table, probs, p_self, p3, cache_v, *, layer):
    db, n_pages = page_table.shape
    n_steps = n_pages // SAMPLE_PAGES_PER_STEP
    grid_spec = pltpu.PrefetchScalarGridSpec(
        num_scalar_prefetch=1,
        grid=(db, n_steps),
        in_specs=[pl.BlockSpec((1, n_pages, A_HEADS, PAGE_SIZE), lambda b, g, pt: (b, 0, 0, 0)),
                  pl.BlockSpec((1, A_HEADS, 128), lambda b, g, pt: (b, 0, 0)),
                  pl.BlockSpec((None, 1, GROUP), lambda b, g, pt: (b, 0, P_V))] + _page_specs(layer),
        out_specs=pl.BlockSpec((None, 1, GROUP), lambda b, g, pt: (b, 0, 0)),
        scratch_shapes=[pltpu.VMEM((A_HEADS, GROUP), F32)],
    )
    return pl.pallas_call(
        functools.partial(_moba_out_body, n_steps=n_steps),
        grid_spec=grid_spec,
        out_shape=jax.ShapeDtypeStruct((db, 1, GROUP), F32),
        compiler_params=_params("parallel", "arbitrary"),
        name="moba_sample_out",
    )(page_table, probs, p_self, p3, *([cache_v] * SAMPLE_PAGES_PER_STEP)).reshape(db, GROUP)


def _col_bcast(row):
    return jnp.broadcast_to(row, (128, 128)).T


def _hgrn_step_body(q_ref, lf_ref, kk_ref, v_ref, sg_ref, gain_ref, s_ref, o_ref, so_ref):
    row = slice(None)
    for h in range(HG_HEADS):
        sl = slice(HG_D * h, HG_D * (h + 1))
        f_col = _col_bcast(jnp.exp(lf_ref[row, sl]))
        k_col = _col_bcast(kk_ref[row, sl])
        q_col = _col_bcast(q_ref[row, sl])
        s_new = f_col * s_ref[0, h] + k_col * v_ref[row, sl]
        so_ref[0, h] = s_new
        o = jnp.sum(q_col * s_new, axis=0, keepdims=True)
        ms = jnp.mean(o * o, axis=-1, keepdims=True)
        o_ref[row, sl] = o * lax.rsqrt(ms + RMS_EPS) * gain_ref[...] * sg_ref[row, sl]


def _hgrn_step(p3, gain, state, *, layer):
    db = p3.shape[0]
    col = lambda g_: pl.BlockSpec((None, 1, GROUP), lambda b: (b, 0, g_))
    st_shape = (1, HG_HEADS, HG_D, HG_D)
    o, s = pl.pallas_call(
        _hgrn_step_body,
        grid=(db,),
        in_specs=[col(P_HQ), col(P_LOGF), col(P_KK), col(P_HV), col(P_HG), _const_spec((1, HG_D)),
                  pl.BlockSpec((None,) + st_shape, lambda b: (layer, b, 0, 0, 0))],
        out_specs=[pl.BlockSpec((None, 1, GROUP), lambda b: (b, 0, 0)),
                   pl.BlockSpec(st_shape, lambda b: (b, 0, 0, 0))],
        out_shape=[jax.ShapeDtypeStruct((db, 1, GROUP), F32),
                   jax.ShapeDtypeStruct((db, HG_HEADS, HG_D, HG_D), F32)],
        compiler_params=_params("parallel"),
        name="hgrn_step",
    )(p3, p3, p3, p3, p3, gain, state)
    return o.reshape(db, GROUP), s


def _ret_step_body(q_ref, k_ref, v_ref, sg_ref, s_ref, o_ref, so_ref):
    row = slice(None)
    gammas = np.exp(np.log1p(-np.exp2(-5.0 - np.arange(RET_HEADS, dtype=np.float32))).astype(np.float32))
    outs = []
    for p2 in range(RET_HEADS // 2):
        cols = slice(128 * p2, 128 * p2 + 128)
        k_col = _col_bcast(k_ref[row, cols])
        q_col = _col_bcast(q_ref[row, cols])
        vrow = v_ref[row, cols]
        for hh in range(2):
            h = 2 * p2 + hh
            rs = slice(RET_D * hh, RET_D * (hh + 1))
            s_new = float(gammas[h]) * s_ref[0, h] + k_col[rs, 0:RET_D] * vrow[:, rs]
            so_ref[0, h] = s_new
            o = jnp.sum(q_col[rs, 0:RET_D] * s_new, axis=0, keepdims=True)
            ms = jnp.mean(o * o, axis=-1, keepdims=True)
            outs.append(o * lax.rsqrt(ms + RMS_EPS))
    o_ref[row, :] = jnp.concatenate(outs, axis=1) * sg_ref[row, :]


def _ret_step(p3, state, *, layer):
    db = p3.shape[0]
    col = lambda g_: pl.BlockSpec((None, 1, GROUP), lambda b: (b, 0, g_))
    st_shape = (1, RET_HEADS, RET_D, RET_D)
    o, s = pl.pallas_call(
        _ret_step_body,
        grid=(db,),
        in_specs=[col(P_RQ), col(P_RK), col(P_RV), col(P_RG),
                  pl.BlockSpec((None,) + st_shape, lambda b: (layer, b, 0, 0, 0))],
        out_specs=[pl.BlockSpec((None, 1, GROUP), lambda b: (b, 0, 0)),
                   pl.BlockSpec(st_shape, lambda b: (b, 0, 0, 0))],
        out_shape=[jax.ShapeDtypeStruct((db, 1, GROUP), F32),
                   jax.ShapeDtypeStruct((db, RET_HEADS, RET_D, RET_D), F32)],
        compiler_params=_params("parallel"),
        name="ret_step",
    )(p3, p3, p3, p3, state)
    return o.reshape(db, GROUP), s


def _rotary_tables(pos):
    theta = 1.0 / (RET_ROPE_BASE ** jnp.linspace(0.0, 1.0, RET_D // 2, dtype=F32))
    ang = pos.astype(F32)[:, None] * jnp.repeat(theta, 2)[None, :]
    ang = jnp.tile(ang, (1, 128 // RET_D))
    cos, sin = jnp.cos(ang), jnp.sin(ang)
    even = (jnp.arange(128) % 2 == 0)[None, :]
    return cos, jnp.where(even, -sin, 0.0), jnp.where(even, 0.0, sin)


def _head_block_diag():
    r = np.arange(GROUP)
    return jnp.asarray((r[:, None] // A_HEAD_DIM) == (r[None, :] // A_HEAD_DIM), BF16)


def _layer_weights(l, lb_all, ffn1_norm, ffn1_w_in, ffn1_w_out, mix_norm, w_in, moba_q_gain, moba_k_gain,
                   hgrn_o_gain, w_branch_a, w_branch_b, w_branch_c, w_out, ffn2_norm, ffn2_w_in, ffn2_w_out):
    row = lambda a: a.reshape(1, -1).astype(F32)
    return dict(
        f1g=row(ffn1_norm[l]), f1a=ffn1_w_in[l].astype(BF16), f1b=ffn1_w_out[l].astype(BF16),
        mg=row(mix_norm[l]), w_in=w_in[l].astype(BF16),
        qg=row(jnp.tile(moba_q_gain[l], A_HEADS)), kg=row(jnp.tile(moba_k_gain[l], A_HEADS)),
        lb=row(lb_all[l]), hg=row(hgrn_o_gain[l]),
        wa=w_branch_a[l].astype(BF16), wb=w_branch_b[l].astype(BF16), wc=w_branch_c[l].astype(BF16),
        wo=w_out[l].astype(BF16),
        f2g=row(ffn2_norm[l]), f2a=ffn2_w_in[l].astype(BF16), f2b=ffn2_w_out[l].astype(BF16),
    )


def _prompt_layer(x, w, tabs, bd, *, batch, seq):
    x = _ffn(x, w["f1g"], w["f1a"], w["f1b"], tm=512)
    p = _proj(x, w["mg"], w["w_in"], w["qg"], w["kg"], w["lb"], *tabs, bd, tm=1024)
    oa = _moba_prompt(p, batch=batch, seq=seq)
    s0t = jnp.zeros((batch, HG_HEADS, HG_D, HG_D), F32)
    ob, s_hg = _hgrn_prompt(p, w["hg"], s0t, batch=batch, seq=seq, tc=256)
    oc, s_ret = _ret_prompt(p, batch=batch, seq=seq)
    x = _merge(oa, ob, oc, p, x, w["wa"], w["wb"], w["wc"], w["wo"], tm=512)
    x = _ffn(x, w["f2g"], w["f2a"], w["f2b"], tm=512)
    k_new = p[:, P_K * GROUP:(P_K + 1) * GROUP].reshape(batch, seq, A_HEADS, A_HEAD_DIM)
    v_new = p[:, P_V * GROUP:(P_V + 1) * GROUP].reshape(batch, seq, A_HEADS, A_HEAD_DIM)
    return x, k_new, v_new, s_hg, _unpair_ret_state(s_ret)


def _sample_layer(x, w, tabs, bd, l, page_table, cache_k, cache_v, state_hgrn, state_ret):
    db = x.shape[0]
    x = _ffn(x, w["f1g"], w["f1a"], w["f1b"], tm=db)
    p = _proj(x, w["mg"], w["w_in"], w["qg"], w["kg"], w["lb"], *tabs, bd, tm=db)
    p3 = p.reshape(db, 1, P_GROUPS * GROUP)
    probs, p_self = _moba_sample_scores(page_table, p3, cache_k, layer=l)
    oa = _moba_sample_out(page_table, probs, p_self, p3, cache_v, layer=l)
    ob, s_hg = _hgrn_step(p3, w["hg"], state_hgrn, layer=l)
    oc, s_ret = _ret_step(p3, state_ret, layer=l)
    x = _merge(oa, ob, oc, p, x, w["wa"], w["wb"], w["wc"], w["wo"], tm=db)
    x = _ffn(x, w["f2g"], w["f2a"], w["f2b"], tm=db)
    k_new = p[:, P_K * GROUP:(P_K + 1) * GROUP].reshape(db, 1, A_HEADS, A_HEAD_DIM)
    v_new = p[:, P_V * GROUP:(P_V + 1) * GROUP].reshape(db, 1, A_HEADS, A_HEAD_DIM)
    return x, k_new, v_new, s_hg, s_ret


def kernel(x_prompt, x_sample, cache_k, cache_v, state_hgrn, state_ret, page_table, ffn1_norm, ffn1_w_in, ffn1_w_out, mix_norm, w_in, moba_q_gain, moba_k_gain, hgrn_lb, hgrn_o_gain, w_branch_a, w_branch_b, w_branch_c, w_out, ffn2_norm, ffn2_w_in, ffn2_w_out):
    lb_cum = jnp.cumsum(jax.nn.softmax(hgrn_lb.astype(F32), axis=0), axis=0)
    lb_all = lb_cum - lb_cum[:1]
    bp, seq, _ = x_prompt.shape
    db, dec_seq, _ = x_sample.shape
    assert dec_seq == 1
    n_pages = page_table.shape[1]
    past_len = n_pages * PAGE_SIZE
    assert past_len % MOBA_BLOCK == 0 and n_pages % SAMPLE_PAGES_PER_STEP == 0
    tabs_p = _rotary_tables(jnp.arange(seq))
    tabs_s = _rotary_tables(jnp.full((db,), past_len, jnp.int32))
    bd = _head_block_diag()
    ck = _pages_feature_major(cache_k)
    cv = _pages_feature_major(cache_v)
    xp = x_prompt.reshape(bp * seq, D_MODEL)
    xs = x_sample.reshape(db, D_MODEL)
    cols = [[] for _ in range(8)]
    for l in range(DEPTH):
        w = _layer_weights(l, lb_all, ffn1_norm, ffn1_w_in, ffn1_w_out, mix_norm, w_in, moba_q_gain, moba_k_gain, hgrn_o_gain, w_branch_a, w_branch_b, w_branch_c, w_out, ffn2_norm, ffn2_w_in, ffn2_w_out)
        xp, kp, vp, hp, rp = _prompt_layer(xp, w, tabs_p, bd, batch=bp, seq=seq)
        xs, ks, vs, hs, rs = _sample_layer(xs, w, tabs_s, bd, l, page_table, ck, cv, state_hgrn, state_ret)
        for c, a in zip(cols, (kp, vp, ks, vs, hp, hs, rp, rs)):
            c.append(a)
    return (xp.reshape(bp, seq, D_MODEL), xs.reshape(db, 1, D_MODEL)) + tuple(jnp.stack(c) for c in cols)
```

```python
import functools
import math

import numpy as np
import jax
import jax.numpy as jnp
from jax import lax
from jax.experimental import pallas as pl
from jax.experimental.pallas import tpu as pltpu

F32 = jnp.float32
BF16 = jnp.bfloat16

D_MODEL = 1024
D_FF = 2816
DEPTH = 4
RMS_EPS = 1e-6
A_HEADS = 8
A_HEAD_DIM = 64
MOBA_BLOCK = 256
MOBA_TOPK = 3
HG_HEADS = 4
HG_D = 128
RET_HEADS = 8
RET_D = 64
RET_ROPE_BASE = 10000.0
PAGE_SIZE = 128
GROUP = 512

P_Q, P_K, P_V, P_HQ, P_LOGF, P_KK, P_HV, P_HG, P_RQ, P_RK, P_RV, P_RG, P_GATES = range(13)
P_GROUPS = 18

HG_CHUNK = 64
RET_CHUNK = 256

VMEM_LIMIT = 56 * 1024 * 1024


def _dot(a, b):
    return jnp.dot(a, b, preferred_element_type=F32)


def _dot_nt(a, b):
    return lax.dot_general(a, b, (((1,), (1,)), ((), ())), preferred_element_type=F32)


def _split_bf16(a):
    hi = a.astype(BF16)
    lo = (a - hi.astype(F32)).astype(BF16)
    return hi, lo


def _params(*sem):
    return pltpu.CompilerParams(dimension_semantics=sem, vmem_limit_bytes=VMEM_LIMIT)


def _const_spec(shape):
    nd = len(shape)
    return pl.BlockSpec(shape, lambda *_: (0,) * nd)


def _ffn_body(x_ref, g_ref, w1_ref, w2_ref, o_ref, acc_ref, *, ff_chunk):
    x = x_ref[...]
    ms = jnp.mean(x * x, axis=-1, keepdims=True)
    h = (x * lax.rsqrt(ms + RMS_EPS) * g_ref[...]).astype(BF16)
    for c in range(D_FF // ff_chunk):
        lo = c * ff_chunk
        gate = _dot(h, w1_ref[:, lo:lo + ff_chunk])
        up = _dot(h, w1_ref[:, D_FF + lo:D_FF + lo + ff_chunk])
        act = (gate * jax.nn.sigmoid(gate) * up).astype(BF16)
        part = _dot(act, w2_ref[lo:lo + ff_chunk, :])
        if c == 0:
            acc_ref[...] = part
        else:
            acc_ref[...] += part
    o_ref[...] = x + 0.5 * acc_ref[...]


def _ffn(x, gain, w1, w2, *, tm):
    n = x.shape[0]
    return pl.pallas_call(
        functools.partial(_ffn_body, ff_chunk=256),
        grid=(n // tm,),
        in_specs=[
            pl.BlockSpec((tm, D_MODEL), lambda i: (i, 0)),
            _const_spec((1, D_MODEL)),
            pl.BlockSpec((D_MODEL, 2 * D_FF), lambda i: (0, 0), pipeline_mode=pl.Buffered(1)),
            pl.BlockSpec((D_FF, D_MODEL), lambda i: (0, 0), pipeline_mode=pl.Buffered(1)),
        ],
        out_specs=pl.BlockSpec((tm, D_MODEL), lambda i: (i, 0)),
        out_shape=jax.ShapeDtypeStruct((n, D_MODEL), F32),
        scratch_shapes=[pltpu.VMEM((tm, D_MODEL), F32)],
        compiler_params=_params("parallel"),
        name="ffn",
    )(x, gain, w1, w2)


def _proj_body(x_ref, g_ref, w_ref, qg_ref, kg_ref, lb_ref, cos_ref, sa_ref, sb_ref, bd_ref, o_ref):
    x = x_ref[...]
    ms = jnp.mean(x * x, axis=-1, keepdims=True)
    h = (x * lax.rsqrt(ms + RMS_EPS) * g_ref[...]).astype(BF16)

    def mm(wg):
        return _dot(h, w_ref[:, wg * GROUP:(wg + 1) * GROUP])

    def put(g, val):
        o_ref[:, g * GROUP:(g + 1) * GROUP] = val

    def headnorm(a, gain_row):
        ss = _dot((a * a).astype(BF16), bd_ref[...])
        return a * lax.rsqrt(ss * (1.0 / A_HEAD_DIM) + RMS_EPS) * gain_row

    def rotary(g, a, scale):
        cos, sa, sb = cos_ref[...], sa_ref[...], sb_ref[...]
        for s in range(GROUP // 128):
            xs = a[:, 128 * s:128 * s + 128]
            nxt = pltpu.roll(xs, 127, 1)
            prv = pltpu.roll(xs, 1, 1)
            r = xs * cos + nxt * sa + prv * sb
            o_ref[:, g * GROUP + 128 * s:g * GROUP + 128 * s + 128] = r * scale if scale != 1.0 else r

    put(P_Q, headnorm(mm(0), qg_ref[...]))
    put(P_K, headnorm(mm(1), kg_ref[...]))
    put(P_V, mm(2))
    put(P_HQ, mm(3))

    fr = mm(4)
    lb = lb_ref[...]
    log_sig = jnp.minimum(fr, 0.0) - jnp.log1p(jnp.exp(-jnp.abs(fr)))
    a = jnp.log(lb)
    c = jnp.log1p(-lb) + log_sig
    put(P_LOGF, jnp.maximum(a, c) + jnp.log1p(jnp.exp(-jnp.abs(a - c))))
    put(P_KK, (1.0 - lb) * jax.nn.sigmoid(-fr))

    put(P_HV, mm(5))
    hg = mm(6)
    put(P_HG, hg * jax.nn.sigmoid(hg))
    rotary(P_RQ, mm(7), 1.0)
    rotary(P_RK, mm(8), RET_D ** -0.5)
    put(P_RV, mm(9))
    rg = mm(10)
    put(P_RG, rg * jax.nn.sigmoid(rg))
    for t in range(P_GROUPS - P_GATES):
        put(P_GATES + t, jax.nn.sigmoid(mm(11 + t)))


def _proj(x, gain, w, qg, kg, lb, cos, sa, sb, bd, *, tm):
    n = x.shape[0]
    n_tab = cos.shape[0] // tm
    tab = pl.BlockSpec((tm, 128), lambda i: (i % n_tab, 0))
    return pl.pallas_call(
        _proj_body,
        grid=(n // tm,),
        in_specs=[
            pl.BlockSpec((tm, D_MODEL), lambda i: (i, 0)),
            _const_spec((1, D_MODEL)),
            pl.BlockSpec(w.shape, lambda i: (0, 0), pipeline_mode=pl.Buffered(1)),
            _const_spec((1, GROUP)),
            _const_spec((1, GROUP)),
            _const_spec((1, GROUP)),
            tab, tab, tab,
            _const_spec((GROUP, GROUP)),
        ],
        out_specs=pl.BlockSpec((tm, P_GROUPS * GROUP), lambda i: (i, 0)),
        out_shape=jax.ShapeDtypeStruct((n, P_GROUPS * GROUP), F32),
        compiler_params=_params("parallel"),
        name="proj",
    )(x, gain, w, qg, kg, lb, cos, sa, sb, bd)


def _merge_body(oa_ref, ob_ref, oc_ref, ga_ref, gb_ref, gc_ref, x_ref, wa_ref, wb_ref, wc_ref, wo_ref, o_ref):
    m = ga_ref[...] * _dot(oa_ref[...].astype(BF16), wa_ref[...])
    m = m + gb_ref[...] * _dot(ob_ref[...].astype(BF16), wb_ref[...])
    m = m + gc_ref[...] * _dot(oc_ref[...].astype(BF16), wc_ref[...])
    o_ref[...] = x_ref[...] + _dot(m.astype(BF16), wo_ref[...])


def _merge(oa, ob, oc, p, x, wa, wb, wc, wo, *, tm):
    n = x.shape[0]
    row = lambda i: (i, 0)
    g0 = P_GATES // 2
    return pl.pallas_call(
        _merge_body,
        grid=(n // tm,),
        in_specs=[
            pl.BlockSpec((tm, GROUP), row),
            pl.BlockSpec((tm, GROUP), row),
            pl.BlockSpec((tm, GROUP), row),
            pl.BlockSpec((tm, D_MODEL), lambda i: (i, g0)),
            pl.BlockSpec((tm, D_MODEL), lambda i: (i, g0 + 1)),
            pl.BlockSpec((tm, D_MODEL), lambda i: (i, g0 + 2)),
            pl.BlockSpec((tm, D_MODEL), row),
            _const_spec((GROUP, D_MODEL)),
            _const_spec((GROUP, D_MODEL)),
            _const_spec((GROUP, D_MODEL)),
            _const_spec((D_MODEL, D_MODEL)),
        ],
        out_specs=pl.BlockSpec((tm, D_MODEL), row),
        out_shape=jax.ShapeDtypeStruct((n, D_MODEL), F32),
        compiler_params=_params("parallel"),
        name="merge",
    )(oa, ob, oc, p, p, p, x, wa, wb, wc, wo)


def _moba_body(q_ref, k_ref, v_ref, o_ref, k16, vt16, kmt, sel_ref, acc_ref, qm_ref, m_ref, l_ref, s_ref, *, nb):
    i = pl.program_id(1)
    blk = MOBA_BLOCK

    @pl.when(i == 0)
    def _():
        lane = lax.broadcasted_iota(jnp.int32, (A_HEADS, GROUP), 1)
        hrow = lax.broadcasted_iota(jnp.int32, (A_HEADS, GROUP), 0)
        headmask = (lax.shift_right_logical(lane, 6) == hrow).astype(F32)

        def init(jj, c):
            r = pl.multiple_of(jj * blk, blk)
            kb = k_ref[pl.ds(r, blk), :]
            k16[jj] = kb.astype(BF16)
            km = jnp.sum(kb, axis=0, keepdims=True) * (1.0 / blk)
            kmt[pl.ds(pl.multiple_of(jj * A_HEADS, A_HEADS), A_HEADS), :] = km * headmask
            vt16[jj] = v_ref[pl.ds(r, blk), :].T.astype(BF16)
            return c

        lax.fori_loop(0, nb, init, 0)

    q = q_ref[...]
    kh, kl = _split_bf16(kmt[...])
    qh, ql = _split_bf16(q)
    gt = _dot_nt(kh, qh) + _dot_nt(kh, ql) + _dot_nt(kl, qh)
    neg_inf = jnp.float32(-jnp.inf)
    g = [jnp.where(j < i, gt[A_HEADS * j:A_HEADS * (j + 1), :], neg_inf) for j in range(nb)]
    for j in range(nb):
        rank = jnp.zeros((A_HEADS, blk), F32)
        for jp in range(nb):
            if jp == j:
                continue
            beats = (g[jp] >= g[j]) if jp < j else (g[jp] > g[j])
            rank = rank + beats.astype(F32)
        sel = (rank < MOBA_TOPK) & (jnp.abs(g[j]) < jnp.inf)
        sel_ref[j] = sel.astype(F32)

    qs = (q * (A_HEAD_DIM ** -0.5)).astype(BF16)
    lane128 = lax.broadcasted_iota(jnp.int32, (blk, 128), 1)
    for p2 in range(A_HEADS // 2):
        qp = qs[:, 128 * p2:128 * p2 + 128]
        zero = jnp.zeros_like(qp)
        qm_ref[p2, 0:blk, :] = jnp.where(lane128 < A_HEAD_DIM, qp, zero)
        qm_ref[p2, blk:2 * blk, :] = jnp.where(lane128 < A_HEAD_DIM, zero, qp)

    masked = jnp.float32(-1e30)
    m_ref[...] = jnp.full(m_ref.shape, masked, F32)
    l_ref[...] = jnp.zeros_like(l_ref)
    acc_ref[...] = jnp.zeros_like(acc_ref)

    def block_step(j, keep_fn):
        for p2 in range(A_HEADS // 2):
            s_ref[p2] = _dot_nt(k16[j, :, 128 * p2:128 * p2 + 128], qm_ref[p2])
        for h in range(A_HEADS):
            rows = slice(A_HEAD_DIM * h, A_HEAD_DIM * (h + 1))
            st = jnp.where(keep_fn(h), s_ref[h // 2, :, blk * (h % 2):blk * (h % 2 + 1)], masked)
            m_old = m_ref[h:h + 1, :]
            m_new = jnp.maximum(m_old, jnp.max(st, axis=0, keepdims=True))
            alpha = jnp.exp(m_old - m_new)
            pt = jnp.exp(st - m_new)
            l_ref[h:h + 1, :] = alpha * l_ref[h:h + 1, :] + jnp.sum(pt, axis=0, keepdims=True)
            m_ref[h:h + 1, :] = m_new
            acc_ref[rows, :] = alpha * acc_ref[rows, :] + _dot(vt16[j, rows, :], pt.astype(BF16))

    kidx = lax.broadcasted_iota(jnp.int32, (blk, blk), 0)
    qidx = lax.broadcasted_iota(jnp.int32, (blk, blk), 1)
    causal = kidx <= qidx
    block_step(i, lambda h: causal)

    def body(j, c):
        block_step(j, lambda h: sel_ref[j, h:h + 1, :] > 0.0)
        return c

    lax.fori_loop(0, i, body, 0)

    for h in range(A_HEADS):
        rows = slice(A_HEAD_DIM * h, A_HEAD_DIM * (h + 1))
        acc_ref[rows, :] = acc_ref[rows, :] / l_ref[h:h + 1, :]
    o_ref[...] = acc_ref[...].T.astype(BF16)


def _moba_prompt(p, *, batch, seq):
    nb = seq // MOBA_BLOCK
    n = batch * seq
    return pl.pallas_call(
        functools.partial(_moba_body, nb=nb),
        grid=(batch, nb),
        in_specs=[
            pl.BlockSpec((MOBA_BLOCK, GROUP), lambda b, i: (b * nb + i, P_Q)),
            pl.BlockSpec((seq, GROUP), lambda b, i: (b, P_K), pipeline_mode=pl.Buffered(1)),
            pl.BlockSpec((seq, GROUP), lambda b, i: (b, P_V), pipeline_mode=pl.Buffered(1)),
        ],
        out_specs=pl.BlockSpec((MOBA_BLOCK, GROUP), lambda b, i: (b * nb + i, 0)),
        out_shape=jax.ShapeDtypeStruct((n, GROUP), BF16),
        scratch_shapes=[
            pltpu.VMEM((nb, MOBA_BLOCK, GROUP), BF16),
            pltpu.VMEM((nb, GROUP, MOBA_BLOCK), BF16),
            pltpu.VMEM((nb * A_HEADS, GROUP), F32),
            pltpu.VMEM((nb, A_HEADS, MOBA_BLOCK), F32),
            pltpu.VMEM((GROUP, MOBA_BLOCK), F32),
            pltpu.VMEM((A_HEADS // 2, 2 * MOBA_BLOCK, 128), BF16),
            pltpu.VMEM((A_HEADS, MOBA_BLOCK), F32),
            pltpu.VMEM((A_HEADS, MOBA_BLOCK), F32),
            pltpu.VMEM((A_HEADS // 2, MOBA_BLOCK, 2 * MOBA_BLOCK), F32),
        ],
        compiler_params=_params("parallel", "arbitrary"),
        name="moba_prompt",
    )(p, p, p)


def _hgrn_tables():
    c = HG_CHUNK
    t = np.arange(c)[:, None]
    u = np.arange(c)[None, :]
    mats = [(u <= t), (u > t)]
    masks = [np.eye(c, dtype=bool)]
    m = c // 2
    while m >= 1:
        b0 = (t // (2 * m)) * (2 * m)
        right = t >= b0 + m
        mats.append(np.where(right, (u >= b0 + m) & (u <= t), (u > t) & (u <= b0 + m - 1)))
        s = u
        masks.append(((s // (2 * m)) == (t // (2 * m))) & right & (s < b0 + m))
        m //= 2
    return (np.concatenate(mats, 0).astype(np.float32), np.stack(masks).astype(np.float32))


def _hgrn_body(q_ref, lf_ref, kk_ref, v_ref, sg_ref, gain_ref, mall_ref, masks_ref, s0_ref,
               o_ref, so_ref, st_ref, ob_ref, *, n_chunks, nt):
    t = pl.program_id(1)
    c = HG_CHUNK
    n_levels = masks_ref.shape[0] - 1

    @pl.when(t == 0)
    def _():
        st_ref[...] = s0_ref[0]

    def chunk(ci):
        r = ci * c
        lf = lf_ref[pl.ds(r, c), :]
        hi, lo = _split_bf16(lf)
        mall = mall_ref[...]
        e_all = _dot(mall, hi) + _dot(mall, lo)
        q = q_ref[pl.ds(r, c), :]
        k = kk_ref[pl.ds(r, c), :]
        v = v_ref[pl.ds(r, c), :]
        for h in range(HG_HEADS):
            sl = slice(HG_D * h, HG_D * (h + 1))
            qh, kh, vh = q[:, sl], k[:, sl], v[:, sl]
            e_cum = e_all[0:c, sl]
            e_rev = e_all[c:2 * c, sl]
            s_t = st_ref[h]
            o = _dot_nt((qh * jnp.exp(e_cum)).astype(BF16), s_t.astype(BF16))
            a = masks_ref[0] * _dot_nt(qh.astype(BF16), kh.astype(BF16))
            for lv in range(n_levels):
                e = jnp.exp(e_all[(2 + lv) * c:(3 + lv) * c, sl])
                a = a + masks_ref[lv + 1] * _dot_nt((qh * e).astype(BF16), (kh * e).astype(BF16))
            o = o + _dot(a.astype(BF16), vh.astype(BF16))
            ob_ref[pl.ds(r, c), sl] = o
            kd = (kh * jnp.exp(e_rev)).astype(BF16)
            st_ref[h] = jnp.exp(e_cum[c - 1:c, :]) * s_t + _dot(vh.T.astype(BF16), kd)

    for ci in range(n_chunks):
        chunk(ci)

    for h in range(HG_HEADS):
        sl = slice(HG_D * h, HG_D * (h + 1))
        x = ob_ref[:, sl]
        ms = jnp.mean(x * x, axis=-1, keepdims=True)
        o_ref[:, sl] = (x * lax.rsqrt(ms + RMS_EPS) * gain_ref[...] * sg_ref[:, sl]).astype(BF16)

    @pl.when(t == nt - 1)
    def _():
        for h in range(HG_HEADS):
            so_ref[0, h] = st_ref[h].T


def _hgrn_prompt(p, gain, s0t, *, batch, seq, tc):
    nt = seq // tc
    n = batch * seq
    mall, masks = _hgrn_tables()
    mall = jnp.asarray(mall, BF16)
    masks = jnp.asarray(masks, F32)
    col = lambda g: pl.BlockSpec((tc, GROUP), lambda b, t: (b * nt + t, g))
    st_spec = pl.BlockSpec((1, HG_HEADS, HG_D, HG_D), lambda b, t: (b, 0, 0, 0))
    return pl.pallas_call(
        functools.partial(_hgrn_body, n_chunks=tc // HG_CHUNK, nt=nt),
        grid=(batch, nt),
        in_specs=[col(P_HQ), col(P_LOGF), col(P_KK), col(P_HV), col(P_HG),
                  _const_spec((1, HG_D)), _const_spec(mall.shape), _const_spec(masks.shape), st_spec],
        out_specs=[pl.BlockSpec((tc, GROUP), lambda b, t: (b * nt + t, 0)), st_spec],
        out_shape=[jax.ShapeDtypeStruct((n, GROUP), BF16),
                   jax.ShapeDtypeStruct((batch, HG_HEADS, HG_D, HG_D), F32)],
        scratch_shapes=[pltpu.VMEM((HG_HEADS, HG_D, HG_D), F32), pltpu.VMEM((tc, GROUP), F32)],
        compiler_params=_params("parallel", "arbitrary"),
        name="hgrn_prompt",
    )(p, p, p, p, p, gain, mall, masks, s0t)


def _ret_tables():
    c = RET_CHUNK
    f32 = np.float32
    lg = np.log1p(-np.exp2(-5.0 - np.arange(RET_HEADS, dtype=f32))).astype(f32)
    t = np.arange(c, dtype=f32)
    rel = t[:, None] - t[None, :]
    intra = np.where(rel[None] >= 0, np.exp(np.maximum(rel, 0.0)[None] * lg[:, None, None]), 0.0).astype(f32)
    lane_lg = np.repeat(lg, RET_D).reshape(RET_HEADS // 2, 1, 128)
    qdec = np.exp((t + 1.0)[None, :, None] * lane_lg).astype(f32)
    kdec = np.exp((c - 1.0 - t)[None, :, None] * lane_lg).astype(f32)
    cdec = np.exp(f32(c) * lane_lg).astype(f32)
    return tuple(jnp.asarray(a) for a in (intra, qdec, kdec, cdec))


def _ret_body(q_ref, k_ref, v_ref, sg_ref, intra_ref, qdec_ref, kdec_ref, cdec_ref, o_ref, so_ref, s_ref, *, nt):
    t = pl.program_id(1)

    @pl.when(t == 0)
    def _():
        s_ref[...] = jnp.zeros_like(s_ref)

    c = RET_CHUNK
    lane = lax.broadcasted_iota(jnp.int32, (c, 128), 1)
    low = lane < RET_D
    r128 = lax.broadcasted_iota(jnp.int32, (128, 128), 0)
    c128 = lax.broadcasted_iota(jnp.int32, (128, 128), 1)
    same_head = (r128 < RET_D) == (c128 < RET_D)
    for p2 in range(RET_HEADS // 2):
        cols = slice(128 * p2, 128 * p2 + 128)
        qp, kp, vp = q_ref[:, cols], k_ref[:, cols], v_ref[:, cols]
        kb, vb = kp.astype(BF16), vp.astype(BF16)
        s_bd = s_ref[p2]
        o = _dot((qp * qdec_ref[p2]).astype(BF16), s_bd.astype(BF16))
        for hh in range(2):
            in_head = low if hh == 0 else ~low
            qm = jnp.where(in_head, qp, 0.0).astype(BF16)
            a = _dot_nt(qm, kb) * intra_ref[2 * p2 + hh]
            o = o + jnp.where(in_head, _dot(a.astype(BF16), vb), 0.0)
        upd = _dot((kp * kdec_ref[p2]).T.astype(BF16), vb)
        s_ref[p2] = cdec_ref[p2] * s_bd + jnp.where(same_head, upd, 0.0)
        sq = o * o
        s_lo = jnp.sum(jnp.where(low, sq, 0.0), axis=-1, keepdims=True)
        s_hi = jnp.sum(jnp.where(low, 0.0, sq), axis=-1, keepdims=True)
        ms = jnp.where(low, s_lo, s_hi) * (1.0 / RET_D)
        o_ref[:, cols] = (o * lax.rsqrt(ms + RMS_EPS) * sg_ref[:, cols]).astype(BF16)

    @pl.when(t == nt - 1)
    def _():
        so_ref[0] = s_ref[...]


def _ret_prompt(p, *, batch, seq):
    c = RET_CHUNK
    nt = seq // c
    n = batch * seq
    intra, qdec, kdec, cdec = _ret_tables()
    col = lambda g: pl.BlockSpec((c, GROUP), lambda b, t: (b * nt + t, g))
    npair = RET_HEADS // 2
    return pl.pallas_call(
        functools.partial(_ret_body, nt=nt),
        grid=(batch, nt),
        in_specs=[col(P_RQ), col(P_RK), col(P_RV), col(P_RG),
                  _const_spec(intra.shape), _const_spec(qdec.shape), _const_spec(kdec.shape),
                  _const_spec(cdec.shape)],
        out_specs=[pl.BlockSpec((c, GROUP), lambda b, t: (b * nt + t, 0)),
                   pl.BlockSpec((1, npair, 128, 128), lambda b, t: (b, 0, 0, 0))],
        out_shape=[jax.ShapeDtypeStruct((n, GROUP), BF16),
                   jax.ShapeDtypeStruct((batch, npair, 128, 128), F32)],
        scratch_shapes=[pltpu.VMEM((npair, 128, 128), F32)],
        compiler_params=_params("parallel", "arbitrary"),
        name="ret_prompt",
    )(p, p, p, p, intra, qdec, kdec, cdec)


def _unpair_ret_state(s_bd):
    b = s_bd.shape[0]
    lo = s_bd[:, :, :RET_D, :RET_D]
    hi = s_bd[:, :, RET_D:, RET_D:]
    return jnp.stack([lo, hi], axis=2).reshape(b, RET_HEADS, RET_D, RET_D)


SAMPLE_PAGES_PER_STEP = 16


def _pages_feature_major(cache):
    d, n_pool = cache.shape[:2]
    return jnp.transpose(cache, (0, 1, 3, 4, 2)).reshape(d, n_pool, GROUP, PAGE_SIZE)


def _page_specs(layer):
    def spec(r):
        return pl.BlockSpec((None, None, GROUP, PAGE_SIZE),
                            lambda b, g, pt: (layer, pt[b, g * SAMPLE_PAGES_PER_STEP + r], 0, 0))
    return [spec(r) for r in range(SAMPLE_PAGES_PER_STEP)]


def _head_rows(row):
    lane = lax.broadcasted_iota(jnp.int32, (A_HEADS, GROUP), 1)
    hrow = lax.broadcasted_iota(jnp.int32, (A_HEADS, GROUP), 0)
    return jnp.where(lax.shift_right_logical(lane, 6) == hrow, jnp.broadcast_to(row, (A_HEADS, GROUP)), 0.0)


def _moba_scores_body(pt_ref, q_ref, kn_ref, *rest, n_steps, n_blocks):
    pages = rest[:SAMPLE_PAGES_PER_STEP]
    prob_ref, pself_ref, s_ref = rest[SAMPLE_PAGES_PER_STEP:]
    g = pl.program_id(1)
    qm = _head_rows(q_ref[...])
    q_hi, q_lo = _split_bf16(qm)
    q_both = jnp.concatenate([q_hi.astype(F32), q_lo.astype(F32)], axis=0).astype(BF16)
    per_blk = MOBA_BLOCK // PAGE_SIZE
    scale = A_HEAD_DIM ** -0.5
    for r in range(SAMPLE_PAGES_PER_STEP):
        k_hi, k_lo = _split_bf16(pages[r][...])
        s2 = _dot(q_both, k_hi)
        s_ref[g * SAMPLE_PAGES_PER_STEP + r] = s2[0:A_HEADS] + s2[A_HEADS:] + _dot(q_hi, k_lo)

    @pl.when(g == n_steps - 1)
    def _():
        jidx = lax.broadcasted_iota(jnp.int32, (A_HEADS, n_blocks), 1)
        gt = jnp.zeros((A_HEADS, n_blocks), F32)
        for j in range(n_blocks):
            tot = s_ref[per_blk * j]
            for r in range(1, per_blk):
                tot = tot + s_ref[per_blk * j + r]
            gj = jnp.sum(tot, axis=-1, keepdims=True) * (1.0 / MOBA_BLOCK)
            gt = jnp.where(jidx == j, gj, gt)
        rank = jnp.zeros((A_HEADS, n_blocks), F32)
        for jp in range(n_blocks):
            cj = gt[:, jp:jp + 1]
            beats = (cj > gt) | ((cj == gt) & (jidx > jp))
            rank = rank + beats.astype(F32)
        sel = ((rank < MOBA_TOPK) & (jnp.abs(gt) < jnp.inf)).astype(F32)
        s_self = jnp.sum(qm * scale * kn_ref[...], axis=-1, keepdims=True)
        masked = jnp.float32(-1e30)
        n_pages = n_blocks * per_blk
        m = s_self
        for pg in range(n_pages):
            keep = sel[:, pg // per_blk:pg // per_blk + 1] > 0.0
            m = jnp.maximum(m, jnp.max(jnp.where(keep, s_ref[pg] * scale, masked), axis=-1, keepdims=True))
        e_self = jnp.exp(s_self - m)
        l = e_self
        for pg in range(n_pages):
            keep = sel[:, pg // per_blk:pg // per_blk + 1] > 0.0
            e = jnp.exp(jnp.where(keep, s_ref[pg] * scale, masked) - m)
            prob_ref[0, pg] = e
            l = l + jnp.sum(e, axis=-1, keepdims=True)
        inv = 1.0 / l
        for pg in range(n_pages):
            prob_ref[0, pg] = prob_ref[0, pg] * inv
        pself_ref[0] = jnp.broadcast_to(e_self * inv, (A_HEADS, 128))


def _moba_sample_scores(page_table, p3, cache_k, *, layer):
    db, n_pages = page_table.shape
    n_steps = n_pages // SAMPLE_PAGES_PER_STEP
    n_blocks = n_pages * PAGE_SIZE // MOBA_BLOCK
    row = lambda g_: pl.BlockSpec((None, 1, GROUP), lambda b, g, pt: (b, 0, g_))
    grid_spec = pltpu.PrefetchScalarGridSpec(
        num_scalar_prefetch=1,
        grid=(db, n_steps),
        in_specs=[row(P_Q), row(P_K)] + _page_specs(layer),
        out_specs=[pl.BlockSpec((1, n_pages, A_HEADS, PAGE_SIZE), lambda b, g, pt: (b, 0, 0, 0)),
                   pl.BlockSpec((1, A_HEADS, 128), lambda b, g, pt: (b, 0, 0))],
        scratch_shapes=[pltpu.VMEM((n_pages, A_HEADS, PAGE_SIZE), F32)],
    )
    return pl.pallas_call(
        functools.partial(_moba_scores_body, n_steps=n_steps, n_blocks=n_blocks),
        grid_spec=grid_spec,
        out_shape=[jax.ShapeDtypeStruct((db, n_pages, A_HEADS, PAGE_SIZE), F32),
                   jax.ShapeDtypeStruct((db, A_HEADS, 128), F32)],
        compiler_params=_params("parallel", "arbitrary"),
        name="moba_sample_scores",
    )(page_table, p3, p3, *([cache_k] * SAMPLE_PAGES_PER_STEP))


def _moba_out_body(pt_ref, prob_ref, pself_ref, vn_ref, *rest, n_steps):
    pages = rest[:SAMPLE_PAGES_PER_STEP]
    o_ref, acc_ref = rest[SAMPLE_PAGES_PER_STEP:]
    g = pl.program_id(1)

    @pl.when(g == 0)
    def _():
        acc_ref[...] = jnp.zeros_like(acc_ref)

    acc = acc_ref[...]
    for r in range(SAMPLE_PAGES_PER_STEP):
        pr = prob_ref[0, g * SAMPLE_PAGES_PER_STEP + r]
        acc = acc + _dot_nt(pr.astype(BF16), pages[r][...].astype(BF16))
    acc_ref[...] = acc

    @pl.when(g == n_steps - 1)
    def _():
        full = acc_ref[...] + pself_ref[0][:, 0:1] * vn_ref[...]
        lane = lax.broadcasted_iota(jnp.int32, (A_HEADS, GROUP), 1)
        hrow = lax.broadcasted_iota(jnp.int32, (A_HEADS, GROUP), 0)
        diag = jnp.where(lax.shift_right_logical(lane, 6) == hrow, full, 0.0)
        o_ref[...] = jnp.sum(diag, axis=0, keepdims=True)


def _moba_sample_out(page_table, probs, p_self, p3, cache_v, *, layer):
    db, n_pages = page_table.shape
    n_steps = n_pages // SAMPLE_PAGES_PER_STEP
    grid_spec = pltpu.PrefetchScalarGridSpec(
        num_scalar_prefetch=1,
        grid=(db, n_steps),
        in_specs=[pl.BlockSpec((1, n_pages, A_HEADS, PAGE_SIZE), lambda b, g, pt: (b, 0, 0, 0)),
                  pl.BlockSpec((1, A_HEADS, 128), lambda b, g, pt: (b, 0, 0)),
                  pl.BlockSpec((None, 1, GROUP), lambda b, g, pt: (b, 0, P_V))] + _page_specs(layer),
        out_specs=pl.BlockSpec((None, 1, GROUP), lambda b, g, pt: (b, 0, 0)),
        scratch_shapes=[pltpu.VMEM((A_HEADS, GROUP), F32)],
    )
    return pl.pallas_call(
        functools.partial(_moba_out_body, n_steps=n_steps),
        grid_spec=grid_spec,
        out_shape=jax.ShapeDtypeStruct((db, 1, GROUP), F32),
        compiler_params=_params("parallel", "arbitrary"),
        name="moba_sample_out",
    )(page_table, probs, p_self, p3, *([cache_v] * SAMPLE_PAGES_PER_STEP)).reshape(db, GROUP)


def _col_bcast(row):
    return jnp.broadcast_to(row, (128, 128)).T


def _hgrn_step_body(q_ref, lf_ref, kk_ref, v_ref, sg_ref, gain_ref, s_ref, o_ref, so_ref):
    row = slice(None)
    for h in range(HG_HEADS):
        sl = slice(HG_D * h, HG_D * (h + 1))
        f_col = _col_bcast(jnp.exp(lf_ref[row, sl]))
        k_col = _col_bcast(kk_ref[row, sl])
        q_col = _col_bcast(q_ref[row, sl])
        s_new = f_col * s_ref[0, h] + k_col * v_ref[row, sl]
        so_ref[0, h] = s_new
        o = jnp.sum(q_col * s_new, axis=0, keepdims=True)
        ms = jnp.mean(o * o, axis=-1, keepdims=True)
        o_ref[row, sl] = o * lax.rsqrt(ms + RMS_EPS) * gain_ref[...] * sg_ref[row, sl]


def _hgrn_step(p3, gain, state, *, layer):
    db = p3.shape[0]
    col = lambda g_: pl.BlockSpec((None, 1, GROUP), lambda b: (b, 0, g_))
    st_shape = (1, HG_HEADS, HG_D, HG_D)
    o, s = pl.pallas_call(
        _hgrn_step_body,
        grid=(db,),
        in_specs=[col(P_HQ), col(P_LOGF), col(P_KK), col(P_HV), col(P_HG), _const_spec((1, HG_D)),
                  pl.BlockSpec((None,) + st_shape, lambda b: (layer, b, 0, 0, 0))],
        out_specs=[pl.BlockSpec((None, 1, GROUP), lambda b: (b, 0, 0)),
                   pl.BlockSpec(st_shape, lambda b: (b, 0, 0, 0))],
        out_shape=[jax.ShapeDtypeStruct((db, 1, GROUP), F32),
                   jax.ShapeDtypeStruct((db, HG_HEADS, HG_D, HG_D), F32)],
        compiler_params=_params("parallel"),
        name="hgrn_step",
    )(p3, p3, p3, p3, p3, gain, state)
    return o.reshape(db, GROUP), s


def _ret_step_body(q_ref, k_ref, v_ref, sg_ref, s_ref, o_ref, so_ref):
    row = slice(None)
    gammas = np.exp(np.log1p(-np.exp2(-5.0 - np.arange(RET_HEADS, dtype=np.float32))).astype(np.float32))
    outs = []
    for p2 in range(RET_HEADS // 2):
        cols = slice(128 * p2, 128 * p2 + 128)
        k_col = _col_bcast(k_ref[row, cols])
        q_col = _col_bcast(q_ref[row, cols])
        vrow = v_ref[row, cols]
        for hh in range(2):
            h = 2 * p2 + hh
            rs = slice(RET_D * hh, RET_D * (hh + 1))
            s_new = float(gammas[h]) * s_ref[0, h] + k_col[rs, 0:RET_D] * vrow[:, rs]
            so_ref[0, h] = s_new
            o = jnp.sum(q_col[rs, 0:RET_D] * s_new, axis=0, keepdims=True)
            ms = jnp.mean(o * o, axis=-1, keepdims=True)
            outs.append(o * lax.rsqrt(ms + RMS_EPS))
    o_ref[row, :] = jnp.concatenate(outs, axis=1) * sg_ref[row, :]


def _ret_step(p3, state, *, layer):
    db = p3.shape[0]
    col = lambda g_: pl.BlockSpec((None, 1, GROUP), lambda b: (b, 0, g_))
    st_shape = (1, RET_HEADS, RET_D, RET_D)
    o, s = pl.pallas_call(
        _ret_step_body,
        grid=(db,),
        in_specs=[col(P_RQ), col(P_RK), col(P_RV), col(P_RG),
                  pl.BlockSpec((None,) + st_shape, lambda b: (layer, b, 0, 0, 0))],
        out_specs=[pl.BlockSpec((None, 1, GROUP), lambda b: (b, 0, 0)),
                   pl.BlockSpec(st_shape, lambda b: (b, 0, 0, 0))],
        out_shape=[jax.ShapeDtypeStruct((db, 1, GROUP), F32),
                   jax.ShapeDtypeStruct((db, RET_HEADS, RET_D, RET_D), F32)],
        compiler_params=_params("parallel"),
        name="ret_step",
    )(p3, p3, p3, p3, state)
    return o.reshape(db, GROUP), s


def _rotary_tables(pos):
    f32 = np.float32
    theta = (f32(1.0) / (f32(RET_ROPE_BASE) ** np.linspace(0.0, 1.0, RET_D // 2, dtype=f32))).astype(f32)
    ang = (pos.astype(f32)[:, None] * np.repeat(theta, 2)[None, :]).astype(f32)
    ang = np.tile(ang, (1, 128 // RET_D))
    cos, sin = np.cos(ang).astype(f32), np.sin(ang).astype(f32)
    even = (np.arange(128) % 2 == 0)[None, :]
    return tuple(jnp.asarray(a) for a in (cos, np.where(even, -sin, f32(0)), np.where(even, f32(0), sin)))


def _head_block_diag():
    r = np.arange(GROUP)
    return jnp.asarray((r[:, None] // A_HEAD_DIM) == (r[None, :] // A_HEAD_DIM), BF16)


def _layer_weights(l, lb_all, ffn1_norm, ffn1_w_in, ffn1_w_out, mix_norm, w_in, moba_q_gain, moba_k_gain,
                   hgrn_o_gain, w_branch_a, w_branch_b, w_branch_c, w_out, ffn2_norm, ffn2_w_in, ffn2_w_out):
    row = lambda a: a.reshape(1, -1).astype(F32)
    return dict(
        f1g=row(ffn1_norm[l]), f1a=ffn1_w_in[l].astype(BF16), f1b=ffn1_w_out[l].astype(BF16),
        mg=row(mix_norm[l]), w_in=w_in[l].astype(BF16),
        qg=row(jnp.tile(moba_q_gain[l], A_HEADS)), kg=row(jnp.tile(moba_k_gain[l], A_HEADS)),
        lb=row(lb_all[l]), hg=row(hgrn_o_gain[l]),
        wa=w_branch_a[l].astype(BF16), wb=w_branch_b[l].astype(BF16), wc=w_branch_c[l].astype(BF16),
        wo=w_out[l].astype(BF16),
        f2g=row(ffn2_norm[l]), f2a=ffn2_w_in[l].astype(BF16), f2b=ffn2_w_out[l].astype(BF16),
    )


def _prompt_layer(x, w, tabs, bd, *, batch, seq):
    x = _ffn(x, w["f1g"], w["f1a"], w["f1b"], tm=512)
    p = _proj(x, w["mg"], w["w_in"], w["qg"], w["kg"], w["lb"], *tabs, bd, tm=256)
    oa = _moba_prompt(p, batch=batch, seq=seq)
    s0t = jnp.zeros((batch, HG_HEADS, HG_D, HG_D), F32)
    ob, s_hg = _hgrn_prompt(p, w["hg"], s0t, batch=batch, seq=seq, tc=256)
    oc, s_ret = _ret_prompt(p, batch=batch, seq=seq)
    x = _merge(oa, ob, oc, p, x, w["wa"], w["wb"], w["wc"], w["wo"], tm=512)
    x = _ffn(x, w["f2g"], w["f2a"], w["f2b"], tm=512)
    k_new = p[:, P_K * GROUP:(P_K + 1) * GROUP].reshape(batch, seq, A_HEADS, A_HEAD_DIM)
    v_new = p[:, P_V * GROUP:(P_V + 1) * GROUP].reshape(batch, seq, A_HEADS, A_HEAD_DIM)
    return x, k_new, v_new, s_hg, _unpair_ret_state(s_ret)


def _sample_layer(x, w, tabs, bd, l, page_table, cache_k, cache_v, state_hgrn, state_ret):
    db = x.shape[0]
    x = _ffn(x, w["f1g"], w["f1a"], w["f1b"], tm=db)
    p = _proj(x, w["mg"], w["w_in"], w["qg"], w["kg"], w["lb"], *tabs, bd, tm=db)
    p3 = p.reshape(db, 1, P_GROUPS * GROUP)
    probs, p_self = _moba_sample_scores(page_table, p3, cache_k, layer=l)
    oa = _moba_sample_out(page_table, probs, p_self, p3, cache_v, layer=l)
    ob, s_hg = _hgrn_step(p3, w["hg"], state_hgrn, layer=l)
    oc, s_ret = _ret_step(p3, state_ret, layer=l)
    x = _merge(oa, ob, oc, p, x, w["wa"], w["wb"], w["wc"], w["wo"], tm=db)
    x = _ffn(x, w["f2g"], w["f2a"], w["f2b"], tm=db)
    k_new = p[:, P_K * GROUP:(P_K + 1) * GROUP].reshape(db, 1, A_HEADS, A_HEAD_DIM)
    v_new = p[:, P_V * GROUP:(P_V + 1) * GROUP].reshape(db, 1, A_HEADS, A_HEAD_DIM)
    return x, k_new, v_new, s_hg, s_ret


def kernel(x_prompt, x_sample, cache_k, cache_v, state_hgrn, state_ret, page_table, ffn1_norm, ffn1_w_in, ffn1_w_out, mix_norm, w_in, moba_q_gain, moba_k_gain, hgrn_lb, hgrn_o_gain, w_branch_a, w_branch_b, w_branch_c, w_out, ffn2_norm, ffn2_w_in, ffn2_w_out):
    lb_cum = jnp.cumsum(jax.nn.softmax(hgrn_lb.astype(F32), axis=0), axis=0)
    lb_all = lb_cum - lb_cum[:1]
    bp, seq, _ = x_prompt.shape
    db, dec_seq, _ = x_sample.shape
    assert dec_seq == 1
    n_pages = page_table.shape[1]
    past_len = n_pages * PAGE_SIZE
    assert past_len % MOBA_BLOCK == 0 and n_pages % SAMPLE_PAGES_PER_STEP == 0
    tabs_p = _rotary_tables(np.arange(seq))
    tabs_s = _rotary_tables(np.full((db,), past_len))
    bd = _head_block_diag()
    ck = _pages_feature_major(cache_k)
    cv = _pages_feature_major(cache_v)
    xp = x_prompt.reshape(bp * seq, D_MODEL)
    xs = x_sample.reshape(db, D_MODEL)
    cols = [[] for _ in range(8)]
    for l in range(DEPTH):
        w = _layer_weights(l, lb_all, ffn1_norm, ffn1_w_in, ffn1_w_out, mix_norm, w_in, moba_q_gain, moba_k_gain, hgrn_o_gain, w_branch_a, w_branch_b, w_branch_c, w_out, ffn2_norm, ffn2_w_in, ffn2_w_out)
        xp, kp, vp, hp, rp = _prompt_layer(xp, w, tabs_p, bd, batch=bp, seq=seq)
        xs, ks, vs, hs, rs = _sample_layer(xs, w, tabs_s, bd, l, page_table, ck, cv, state_hgrn, state_ret)
        for c, a in zip(cols, (kp, vp, ks, vs, hp, hs, rp, rs)):
            c.append(a)
    return (xp.reshape(bp, seq, D_MODEL), xs.reshape(db, 1, D_MODEL)) + tuple(jnp.stack(c) for c in cols)
```

```python
import functools
import math

import numpy as np
import jax
import jax.numpy as jnp
from jax import lax
from jax.experimental import pallas as pl
from jax.experimental.pallas import tpu as pltpu

F32 = jnp.float32
BF16 = jnp.bfloat16

D_MODEL = 1024
D_FF = 2816
DEPTH = 4
RMS_EPS = 1e-6
A_HEADS = 8
A_HEAD_DIM = 64
MOBA_BLOCK = 256
MOBA_TOPK = 3
HG_HEADS = 4
HG_D = 128
RET_HEADS = 8
RET_D = 64
RET_ROPE_BASE = 10000.0
PAGE_SIZE = 128
GROUP = 512

P_Q, P_K, P_V, P_HQ, P_LOGF, P_KK, P_HV, P_HG, P_RQ, P_RK, P_RV, P_RG, P_GATES = range(13)
P_GROUPS = 18

HG_CHUNK = 64
HG_SUBLANE_LEVEL = 8
RET_CHUNK = 256

VMEM_LIMIT = 56 * 1024 * 1024


def _dot(a, b):
    return jnp.dot(a, b, preferred_element_type=F32)


def _dot_nt(a, b):
    return lax.dot_general(a, b, (((1,), (1,)), ((), ())), preferred_element_type=F32)


def _split_bf16(a):
    hi = a.astype(BF16)
    lo = (a - hi.astype(F32)).astype(BF16)
    return hi, lo


def _params(*sem):
    return pltpu.CompilerParams(dimension_semantics=sem, vmem_limit_bytes=VMEM_LIMIT)


def _const_spec(shape):
    nd = len(shape)
    return pl.BlockSpec(shape, lambda *_: (0,) * nd)


def _ffn_body(x_ref, g_ref, w1_ref, w2_ref, o_ref, acc_ref, *, ff_chunk):
    x = x_ref[...]
    ms = jnp.mean(x * x, axis=-1, keepdims=True)
    h = (x * lax.rsqrt(ms + RMS_EPS) * g_ref[...]).astype(BF16)
    for c in range(D_FF // ff_chunk):
        lo = c * ff_chunk
        gate = _dot(h, w1_ref[:, lo:lo + ff_chunk])
        up = _dot(h, w1_ref[:, D_FF + lo:D_FF + lo + ff_chunk])
        act = (gate * jax.nn.sigmoid(gate) * up).astype(BF16)
        part = _dot(act, w2_ref[lo:lo + ff_chunk, :])
        if c == 0:
            acc_ref[...] = part
        else:
            acc_ref[...] += part
    o_ref[...] = x + 0.5 * acc_ref[...]


def _ffn(x, gain, w1, w2, *, tm):
    n = x.shape[0]
    return pl.pallas_call(
        functools.partial(_ffn_body, ff_chunk=256),
        grid=(n // tm,),
        in_specs=[
            pl.BlockSpec((tm, D_MODEL), lambda i: (i, 0)),
            _const_spec((1, D_MODEL)),
            pl.BlockSpec((D_MODEL, 2 * D_FF), lambda i: (0, 0), pipeline_mode=pl.Buffered(1)),
            pl.BlockSpec((D_FF, D_MODEL), lambda i: (0, 0), pipeline_mode=pl.Buffered(1)),
        ],
        out_specs=pl.BlockSpec((tm, D_MODEL), lambda i: (i, 0)),
        out_shape=jax.ShapeDtypeStruct((n, D_MODEL), F32),
        scratch_shapes=[pltpu.VMEM((tm, D_MODEL), F32)],
        compiler_params=_params("parallel"),
        name="ffn",
    )(x, gain, w1, w2)


def _proj_body(x_ref, g_ref, w_ref, qg_ref, kg_ref, lb_ref, cos_ref, sa_ref, sb_ref, bd_ref, o_ref, *kv_t_refs):
    x = x_ref[...]
    ms = jnp.mean(x * x, axis=-1, keepdims=True)
    h = (x * lax.rsqrt(ms + RMS_EPS) * g_ref[...]).astype(BF16)

    def mm(wg):
        return _dot(h, w_ref[:, wg * GROUP:(wg + 1) * GROUP])

    def put(g, val):
        o_ref[:, g * GROUP:(g + 1) * GROUP] = val

    def headnorm(a, gain_row):
        ss = _dot((a * a).astype(BF16), bd_ref[...])
        return a * lax.rsqrt(ss * (1.0 / A_HEAD_DIM) + RMS_EPS) * gain_row

    def rotary(g, a, scale):
        cos, sa, sb = cos_ref[...], sa_ref[...], sb_ref[...]
        for s in range(GROUP // 128):
            xs = a[:, 128 * s:128 * s + 128]
            nxt = pltpu.roll(xs, 127, 1)
            prv = pltpu.roll(xs, 1, 1)
            r = xs * cos + nxt * sa + prv * sb
            o_ref[:, g * GROUP + 128 * s:g * GROUP + 128 * s + 128] = r * scale if scale != 1.0 else r

    put(P_Q, headnorm(mm(0), qg_ref[...]))
    k_n = headnorm(mm(1), kg_ref[...])
    v = mm(2)
    put(P_K, k_n)
    put(P_V, v)
    if kv_t_refs:
        kv_t_refs[0][...] = k_n.T
        kv_t_refs[1][...] = v.T
    put(P_HQ, mm(3))

    fr = mm(4)
    lb = lb_ref[...]
    log_sig = jnp.minimum(fr, 0.0) - jnp.log1p(jnp.exp(-jnp.abs(fr)))
    a = jnp.log(lb)
    c = jnp.log1p(-lb) + log_sig
    put(P_LOGF, jnp.maximum(a, c) + jnp.log1p(jnp.exp(-jnp.abs(a - c))))
    put(P_KK, (1.0 - lb) * jax.nn.sigmoid(-fr))

    put(P_HV, mm(5))
    hg = mm(6)
    put(P_HG, hg * jax.nn.sigmoid(hg))
    rotary(P_RQ, mm(7), 1.0)
    rotary(P_RK, mm(8), RET_D ** -0.5)
    put(P_RV, mm(9))
    rg = mm(10)
    put(P_RG, rg * jax.nn.sigmoid(rg))
    for t in range(P_GROUPS - P_GATES):
        put(P_GATES + t, jax.nn.sigmoid(mm(11 + t)))


def _proj(x, gain, w, qg, kg, lb, cos, sa, sb, bd, *, tm, kv_t_batch=0):
    n = x.shape[0]
    n_tab = cos.shape[0] // tm
    tab = pl.BlockSpec((tm, 128), lambda i: (i % n_tab, 0))
    out_specs = [pl.BlockSpec((tm, P_GROUPS * GROUP), lambda i: (i, 0))]
    out_shape = [jax.ShapeDtypeStruct((n, P_GROUPS * GROUP), F32)]
    if kv_t_batch:
        seq = n // kv_t_batch
        nt = seq // tm
        for _ in range(2):
            out_specs.append(pl.BlockSpec((None, GROUP, tm), lambda i: (i // nt, 0, i % nt)))
            out_shape.append(jax.ShapeDtypeStruct((kv_t_batch, GROUP, seq), F32))
    return pl.pallas_call(
        _proj_body,
        grid=(n // tm,),
        in_specs=[
            pl.BlockSpec((tm, D_MODEL), lambda i: (i, 0)),
            _const_spec((1, D_MODEL)),
            pl.BlockSpec(w.shape, lambda i: (0, 0), pipeline_mode=pl.Buffered(1)),
            _const_spec((1, GROUP)),
            _const_spec((1, GROUP)),
            _const_spec((1, GROUP)),
            tab, tab, tab,
            _const_spec((GROUP, GROUP)),
        ],
        out_specs=out_specs,
        out_shape=out_shape,
        compiler_params=_params("parallel"),
        name="proj",
    )(x, gain, w, qg, kg, lb, cos, sa, sb, bd)


def _merge_body(oa_ref, ob_ref, oc_ref, ga_ref, gb_ref, gc_ref, x_ref, wa_ref, wb_ref, wc_ref, wo_ref, o_ref):
    m = ga_ref[...] * _dot(oa_ref[...].astype(BF16), wa_ref[...])
    m = m + gb_ref[...] * _dot(ob_ref[...].astype(BF16), wb_ref[...])
    m = m + gc_ref[...] * _dot(oc_ref[...].astype(BF16), wc_ref[...])
    o_ref[...] = x_ref[...] + _dot(m.astype(BF16), wo_ref[...])


def _merge(oa, ob, oc, p, x, wa, wb, wc, wo, *, tm):
    n = x.shape[0]
    row = lambda i: (i, 0)
    g0 = P_GATES // 2
    return pl.pallas_call(
        _merge_body,
        grid=(n // tm,),
        in_specs=[
            pl.BlockSpec((tm, GROUP), row),
            pl.BlockSpec((tm, GROUP), row),
            pl.BlockSpec((tm, GROUP), row),
            pl.BlockSpec((tm, D_MODEL), lambda i: (i, g0)),
            pl.BlockSpec((tm, D_MODEL), lambda i: (i, g0 + 1)),
            pl.BlockSpec((tm, D_MODEL), lambda i: (i, g0 + 2)),
            pl.BlockSpec((tm, D_MODEL), row),
            _const_spec((GROUP, D_MODEL)),
            _const_spec((GROUP, D_MODEL)),
            _const_spec((GROUP, D_MODEL)),
            _const_spec((D_MODEL, D_MODEL)),
        ],
        out_specs=pl.BlockSpec((tm, D_MODEL), row),
        out_shape=jax.ShapeDtypeStruct((n, D_MODEL), F32),
        compiler_params=_params("parallel"),
        name="merge",
    )(oa, ob, oc, p, p, p, x, wa, wb, wc, wo)


def _moba_body(q_ref, k_ref, v_ref, o_ref, k16, vt16, kmt, sel_ref, acc_ref, qm_ref, m_ref, l_ref, s_ref, *, nb):
    i = pl.program_id(1)
    blk = MOBA_BLOCK

    @pl.when(i == 0)
    def _():
        lane = lax.broadcasted_iota(jnp.int32, (A_HEADS, GROUP), 1)
        hrow = lax.broadcasted_iota(jnp.int32, (A_HEADS, GROUP), 0)
        headmask = (lax.shift_right_logical(lane, 6) == hrow).astype(F32)

        def init(jj, c):
            r = pl.multiple_of(jj * blk, blk)
            kb = k_ref[pl.ds(r, blk), :]
            k16[jj] = kb.astype(BF16)
            km = jnp.sum(kb, axis=0, keepdims=True) * (1.0 / blk)
            kmt[pl.ds(pl.multiple_of(jj * A_HEADS, A_HEADS), A_HEADS), :] = km * headmask
            vt16[jj] = v_ref[pl.ds(r, blk), :].T.astype(BF16)
            return c

        lax.fori_loop(0, nb, init, 0)

    q = q_ref[...]
    kh, kl = _split_bf16(kmt[...])
    qh, ql = _split_bf16(q)
    gt = _dot_nt(kh, qh) + _dot_nt(kh, ql) + _dot_nt(kl, qh)
    neg_inf = jnp.float32(-jnp.inf)
    g = [jnp.where(j < i, gt[A_HEADS * j:A_HEADS * (j + 1), :], neg_inf) for j in range(nb)]
    for j in range(nb):
        rank = jnp.zeros((A_HEADS, blk), F32)
        for jp in range(nb):
            if jp == j:
                continue
            beats = (g[jp] >= g[j]) if jp < j else (g[jp] > g[j])
            rank = rank + beats.astype(F32)
        sel = (rank < MOBA_TOPK) & (jnp.abs(g[j]) < jnp.inf)
        sel_ref[j] = sel.astype(F32)

    qs = (q * (A_HEAD_DIM ** -0.5)).astype(BF16)
    lane128 = lax.broadcasted_iota(jnp.int32, (blk, 128), 1)
    for p2 in range(A_HEADS // 2):
        qp = qs[:, 128 * p2:128 * p2 + 128]
        zero = jnp.zeros_like(qp)
        qm_ref[p2, 0:blk, :] = jnp.where(lane128 < A_HEAD_DIM, qp, zero)
        qm_ref[p2, blk:2 * blk, :] = jnp.where(lane128 < A_HEAD_DIM, zero, qp)

    masked = jnp.float32(-1e30)
    m_ref[...] = jnp.full(m_ref.shape, masked, F32)
    l_ref[...] = jnp.zeros_like(l_ref)
    acc_ref[...] = jnp.zeros_like(acc_ref)

    def block_step(j, keep_fn):
        for p2 in range(A_HEADS // 2):
            s_ref[p2] = _dot_nt(k16[j, :, 128 * p2:128 * p2 + 128], qm_ref[p2])
        for h in range(A_HEADS):
            rows = slice(A_HEAD_DIM * h, A_HEAD_DIM * (h + 1))
            st = jnp.where(keep_fn(h), s_ref[h // 2, :, blk * (h % 2):blk * (h % 2 + 1)], masked)
            m_old = m_ref[h:h + 1, :]
            m_new = jnp.maximum(m_old, jnp.max(st, axis=0, keepdims=True))
            alpha = jnp.exp(m_old - m_new)
            pt = jnp.exp(st - m_new)
            l_ref[h:h + 1, :] = alpha * l_ref[h:h + 1, :] + jnp.sum(pt, axis=0, keepdims=True)
            m_ref[h:h + 1, :] = m_new
            acc_ref[rows, :] = alpha * acc_ref[rows, :] + _dot(vt16[j, rows, :], pt.astype(BF16))

    kidx = lax.broadcasted_iota(jnp.int32, (blk, blk), 0)
    qidx = lax.broadcasted_iota(jnp.int32, (blk, blk), 1)
    causal = kidx <= qidx
    block_step(i, lambda h: causal)

    def body(j, c):
        block_step(j, lambda h: sel_ref[j, h:h + 1, :] > 0.0)
        return c

    lax.fori_loop(0, i, body, 0)

    for h in range(A_HEADS):
        rows = slice(A_HEAD_DIM * h, A_HEAD_DIM * (h + 1))
        acc_ref[rows, :] = acc_ref[rows, :] / l_ref[h:h + 1, :]
    o_ref[...] = acc_ref[...].T.astype(BF16)


def _moba_prompt(p, *, batch, seq):
    nb = seq // MOBA_BLOCK
    n = batch * seq
    return pl.pallas_call(
        functools.partial(_moba_body, nb=nb),
        grid=(batch, nb),
        in_specs=[
            pl.BlockSpec((MOBA_BLOCK, GROUP), lambda b, i: (b * nb + i, P_Q)),
            pl.BlockSpec((seq, GROUP), lambda b, i: (b, P_K), pipeline_mode=pl.Buffered(1)),
            pl.BlockSpec((seq, GROUP), lambda b, i: (b, P_V), pipeline_mode=pl.Buffered(1)),
        ],
        out_specs=pl.BlockSpec((MOBA_BLOCK, GROUP), lambda b, i: (b * nb + i, 0)),
        out_shape=jax.ShapeDtypeStruct((n, GROUP), BF16),
        scratch_shapes=[
            pltpu.VMEM((nb, MOBA_BLOCK, GROUP), BF16),
            pltpu.VMEM((nb, GROUP, MOBA_BLOCK), BF16),
            pltpu.VMEM((nb * A_HEADS, GROUP), F32),
            pltpu.VMEM((nb, A_HEADS, MOBA_BLOCK), F32),
            pltpu.VMEM((GROUP, MOBA_BLOCK), F32),
            pltpu.VMEM((A_HEADS // 2, 2 * MOBA_BLOCK, 128), BF16),
            pltpu.VMEM((A_HEADS, MOBA_BLOCK), F32),
            pltpu.VMEM((A_HEADS, MOBA_BLOCK), F32),
            pltpu.VMEM((A_HEADS // 2, MOBA_BLOCK, 2 * MOBA_BLOCK), F32),
        ],
        compiler_params=_params("parallel", "arbitrary"),
        name="moba_prompt",
    )(p, p, p)


def _hgrn_tables():
    c = HG_CHUNK
    t = np.arange(c)[:, None]
    u = np.arange(c)[None, :]
    mats = [(u <= t)]
    masks = [np.eye(c, dtype=bool)]
    m = c // 2
    while m >= 1:
        b0 = (t // (2 * m)) * (2 * m)
        right = t >= b0 + m
        if m < HG_SUBLANE_LEVEL:
            mats.append(np.where(right, (u >= b0 + m) & (u <= t), (u > t) & (u <= b0 + m - 1)))
        s = u
        masks.append(((s // (2 * m)) == (t // (2 * m))) & right & (s < b0 + m))
        m //= 2
    return (np.concatenate(mats, 0).astype(np.float32), np.stack(masks).astype(np.float32))


def _hgrn_body(q_ref, lf_ref, kk_ref, v_ref, sg_ref, gain_ref, mall_ref, masks_ref, s0_ref,
               o_ref, so_ref, st_ref, ob_ref, qd_ref, kd_ref, dl_ref, vt_ref, ex_ref, a_ref, *, n_chunks, nt):
    t = pl.program_id(1)
    c = HG_CHUNK

    @pl.when(t == 0)
    def _():
        st_ref[...] = s0_ref[0]

    row = lax.broadcasted_iota(jnp.int32, (c, 1), 0)

    n_lv = masks_ref.shape[0] - 1

    def decays(ci):
        rows = slice(ci * c, (ci + 1) * c)
        hi, lo = _split_bf16(lf_ref[rows, :])
        mall = mall_ref[...]
        e_mm = _dot(mall, hi) + _dot(mall, lo)
        b = e_mm[0:c]
        b_last = b[c - 1:c, :]
        lv = 0
        m = c // 2
        while m >= HG_SUBLANE_LEVEL:
            ref = jnp.concatenate(
                [jnp.broadcast_to(b[s + m - 1:s + m, :], (2 * m, GROUP)) for s in range(0, c, 2 * m)], axis=0)
            d = b - ref
            ex_ref[ci % 2, lv] = jnp.exp(jnp.where((row & (2 * m - 1)) >= m, d, -d))
            lv += 1
            m //= 2
        for i in range(1, mall.shape[0] // c):
            ex_ref[ci % 2, lv] = jnp.exp(e_mm[i * c:(i + 1) * c])
            lv += 1
        qd_ref[ci] = (q_ref[rows, :] * jnp.exp(b)).astype(BF16)
        kd_ref[ci] = (kk_ref[rows, :] * jnp.exp(b_last - b)).astype(BF16)
        dl_ref[ci] = jnp.exp(b_last)

    def pair_weights(ci):
        rows = slice(ci * c, (ci + 1) * c)
        for h in range(HG_HEADS):
            sl = slice(HG_D * h, HG_D * (h + 1))
            qh, kh = q_ref[rows, sl], kk_ref[rows, sl]
            a = masks_ref[0] * _dot_nt(qh.astype(BF16), kh.astype(BF16))
            for lv in range(n_lv):
                e = ex_ref[ci % 2, lv, :, sl]
                a = a + masks_ref[lv + 1] * _dot_nt((qh * e).astype(BF16), (kh * e).astype(BF16))
            a_ref[ci % 2, h] = a.astype(BF16)

    def intra_out(ci):
        rows = slice(ci * c, (ci + 1) * c)
        for h in range(HG_HEADS):
            sl = slice(HG_D * h, HG_D * (h + 1))
            vh = v_ref[rows, sl]
            ob_ref[rows, sl] = _dot(a_ref[ci % 2, h], vh.astype(BF16))
            vt_ref[ci, h] = vh.T.astype(BF16)

    def carry(ci):
        rows = slice(ci * c, (ci + 1) * c)
        for h in range(HG_HEADS):
            sl = slice(HG_D * h, HG_D * (h + 1))
            s_t = st_ref[h]
            ob_ref[rows, sl] += _dot_nt(qd_ref[ci, :, sl], s_t.astype(BF16))
            st_ref[h] = dl_ref[ci, :, sl] * s_t + _dot(vt_ref[ci, h], kd_ref[ci, :, sl])

    decays(0)
    pair_weights(0)
    for ci in range(n_chunks):
        if ci + 1 < n_chunks:
            decays(ci + 1)
        intra_out(ci)
        carry(ci)
        if ci + 1 < n_chunks:
            pair_weights(ci + 1)

    for h in range(HG_HEADS):
        sl = slice(HG_D * h, HG_D * (h + 1))
        x = ob_ref[:, sl]
        ms = jnp.mean(x * x, axis=-1, keepdims=True)
        o_ref[:, sl] = (x * lax.rsqrt(ms + RMS_EPS) * gain_ref[...] * sg_ref[:, sl]).astype(BF16)

    @pl.when(t == nt - 1)
    def _():
        for h in range(HG_HEADS):
            so_ref[0, h] = st_ref[h].T


def _hgrn_prompt(p, gain, s0t, *, batch, seq, tc):
    nt = seq // tc
    n = batch * seq
    mall, masks = _hgrn_tables()
    mall = jnp.asarray(mall, BF16)
    masks = jnp.asarray(masks, F32)
    col = lambda g: pl.BlockSpec((tc, GROUP), lambda b, t: (b * nt + t, g))
    st_spec = pl.BlockSpec((1, HG_HEADS, HG_D, HG_D), lambda b, t: (b, 0, 0, 0))
    return pl.pallas_call(
        functools.partial(_hgrn_body, n_chunks=tc // HG_CHUNK, nt=nt),
        grid=(batch, nt),
        in_specs=[col(P_HQ), col(P_LOGF), col(P_KK), col(P_HV), col(P_HG),
                  _const_spec((1, HG_D)), _const_spec(mall.shape), _const_spec(masks.shape), st_spec],
        out_specs=[pl.BlockSpec((tc, GROUP), lambda b, t: (b * nt + t, 0)), st_spec],
        out_shape=[jax.ShapeDtypeStruct((n, GROUP), BF16),
                   jax.ShapeDtypeStruct((batch, HG_HEADS, HG_D, HG_D), F32)],
        scratch_shapes=[pltpu.VMEM((HG_HEADS, HG_D, HG_D), F32), pltpu.VMEM((tc, GROUP), F32),
                        pltpu.VMEM((tc // HG_CHUNK, HG_CHUNK, GROUP), BF16),
                        pltpu.VMEM((tc // HG_CHUNK, HG_CHUNK, GROUP), BF16),
                        pltpu.VMEM((tc // HG_CHUNK, 1, GROUP), F32),
                        pltpu.VMEM((tc // HG_CHUNK, HG_HEADS, HG_D, HG_CHUNK), BF16),
                        pltpu.VMEM((2, masks.shape[0] - 1, HG_CHUNK, GROUP), F32),
                        pltpu.VMEM((2, HG_HEADS, HG_CHUNK, HG_CHUNK), BF16)],
        compiler_params=_params("parallel", "arbitrary"),
        name="hgrn_prompt",
    )(p, p, p, p, p, gain, mall, masks, s0t)


def _ret_tables():
    c = RET_CHUNK
    f32 = np.float32
    lg = np.log1p(-np.exp2(-5.0 - np.arange(RET_HEADS, dtype=f32))).astype(f32)
    t = np.arange(c, dtype=f32)
    rel = t[:, None] - t[None, :]
    intra = np.where(rel[None] >= 0, np.exp(np.maximum(rel, 0.0)[None] * lg[:, None, None]), 0.0).astype(f32)
    lane_lg = np.repeat(lg, RET_D).reshape(RET_HEADS // 2, 1, 128)
    qdec = np.exp((t + 1.0)[None, :, None] * lane_lg).astype(f32)
    kdec = np.exp((c - 1.0 - t)[None, :, None] * lane_lg).astype(f32)
    cdec = np.exp(f32(c) * lane_lg).astype(f32)
    return tuple(jnp.asarray(a) for a in (intra, qdec, kdec, cdec))


def _ret_body(q_ref, k_ref, v_ref, sg_ref, intra_ref, qdec_ref, kdec_ref, cdec_ref, o_ref, so_ref, s_ref, *, nt):
    t = pl.program_id(1)

    @pl.when(t == 0)
    def _():
        s_ref[...] = jnp.zeros_like(s_ref)

    c = RET_CHUNK
    lane = lax.broadcasted_iota(jnp.int32, (c, 128), 1)
    low = lane < RET_D
    r128 = lax.broadcasted_iota(jnp.int32, (128, 128), 0)
    c128 = lax.broadcasted_iota(jnp.int32, (128, 128), 1)
    same_head = (r128 < RET_D) == (c128 < RET_D)
    for p2 in range(RET_HEADS // 2):
        cols = slice(128 * p2, 128 * p2 + 128)
        qp, kp, vp = q_ref[:, cols], k_ref[:, cols], v_ref[:, cols]
        kb, vb = kp.astype(BF16), vp.astype(BF16)
        s_bd = s_ref[p2]
        o = _dot((qp * qdec_ref[p2]).astype(BF16), s_bd.astype(BF16))
        for hh in range(2):
            in_head = low if hh == 0 else ~low
            qm = jnp.where(in_head, qp, 0.0).astype(BF16)
            a = _dot_nt(qm, kb) * intra_ref[2 * p2 + hh]
            o = o + jnp.where(in_head, _dot(a.astype(BF16), vb), 0.0)
        upd = _dot((kp * kdec_ref[p2]).T.astype(BF16), vb)
        s_ref[p2] = cdec_ref[p2] * s_bd + jnp.where(same_head, upd, 0.0)
        sq = o * o
        s_lo = jnp.sum(jnp.where(low, sq, 0.0), axis=-1, keepdims=True)
        s_hi = jnp.sum(jnp.where(low, 0.0, sq), axis=-1, keepdims=True)
        ms = jnp.where(low, s_lo, s_hi) * (1.0 / RET_D)
        o_ref[:, cols] = (o * lax.rsqrt(ms + RMS_EPS) * sg_ref[:, cols]).astype(BF16)

    @pl.when(t == nt - 1)
    def _():
        so_ref[0] = s_ref[...]


def _ret_prompt(p, *, batch, seq):
    c = RET_CHUNK
    nt = seq // c
    n = batch * seq
    intra, qdec, kdec, cdec = _ret_tables()
    col = lambda g: pl.BlockSpec((c, GROUP), lambda b, t: (b * nt + t, g))
    npair = RET_HEADS // 2
    return pl.pallas_call(
        functools.partial(_ret_body, nt=nt),
        grid=(batch, nt),
        in_specs=[col(P_RQ), col(P_RK), col(P_RV), col(P_RG),
                  _const_spec(intra.shape), _const_spec(qdec.shape), _const_spec(kdec.shape),
                  _const_spec(cdec.shape)],
        out_specs=[pl.BlockSpec((c, GROUP), lambda b, t: (b * nt + t, 0)),
                   pl.BlockSpec((1, npair, 128, 128), lambda b, t: (b, 0, 0, 0))],
        out_shape=[jax.ShapeDtypeStruct((n, GROUP), BF16),
                   jax.ShapeDtypeStruct((batch, npair, 128, 128), F32)],
        scratch_shapes=[pltpu.VMEM((npair, 128, 128), F32)],
        compiler_params=_params("parallel", "arbitrary"),
        name="ret_prompt",
    )(p, p, p, p, intra, qdec, kdec, cdec)


def _unpair_ret_state(s_bd):
    b = s_bd.shape[0]
    lo = s_bd[:, :, :RET_D, :RET_D]
    hi = s_bd[:, :, RET_D:, RET_D:]
    return jnp.stack([lo, hi], axis=2).reshape(b, RET_HEADS, RET_D, RET_D)


SAMPLE_PAGES_PER_STEP = 32


def _pages_feature_major(cache):
    d, n_pool = cache.shape[:2]
    return jnp.transpose(cache, (0, 1, 3, 4, 2)).reshape(d, n_pool, GROUP, PAGE_SIZE)


def _page_specs(layer):
    def spec(r):
        return pl.BlockSpec((None, None, GROUP, PAGE_SIZE),
                            lambda b, g, pt: (layer, pt[b, g * SAMPLE_PAGES_PER_STEP + r], 0, 0))
    return [spec(r) for r in range(SAMPLE_PAGES_PER_STEP)]


def _head_rows(row):
    lane = lax.broadcasted_iota(jnp.int32, (A_HEADS, GROUP), 1)
    hrow = lax.broadcasted_iota(jnp.int32, (A_HEADS, GROUP), 0)
    return jnp.where(lax.shift_right_logical(lane, 6) == hrow, jnp.broadcast_to(row, (A_HEADS, GROUP)), 0.0)


def _moba_scores_body(pt_ref, q_ref, kn_ref, *rest, n_steps, n_blocks):
    pages = rest[:SAMPLE_PAGES_PER_STEP]
    prob_ref, pself_ref, s_ref = rest[SAMPLE_PAGES_PER_STEP:]
    g = pl.program_id(1)
    qm = _head_rows(q_ref[...])
    q_hi, q_lo = _split_bf16(qm)
    q_both = jnp.concatenate([q_hi.astype(F32), q_lo.astype(F32)], axis=0).astype(BF16)
    per_blk = MOBA_BLOCK // PAGE_SIZE
    scale = A_HEAD_DIM ** -0.5
    for r in range(SAMPLE_PAGES_PER_STEP):
        k_hi, k_lo = _split_bf16(pages[r][...])
        s2 = _dot(q_both, k_hi)
        s_ref[g * SAMPLE_PAGES_PER_STEP + r] = s2[0:A_HEADS] + s2[A_HEADS:] + _dot(q_hi, k_lo)

    @pl.when(g == n_steps - 1)
    def _():
        jidx = lax.broadcasted_iota(jnp.int32, (A_HEADS, n_blocks), 1)
        gt = jnp.zeros((A_HEADS, n_blocks), F32)
        for j in range(n_blocks):
            tot = s_ref[per_blk * j]
            for r in range(1, per_blk):
                tot = tot + s_ref[per_blk * j + r]
            gj = jnp.sum(tot, axis=-1, keepdims=True) * (1.0 / MOBA_BLOCK)
            gt = jnp.where(jidx == j, gj, gt)
        rank = jnp.zeros((A_HEADS, n_blocks), F32)
        for jp in range(n_blocks):
            cj = gt[:, jp:jp + 1]
            beats = (cj > gt) | ((cj == gt) & (jidx > jp))
            rank = rank + beats.astype(F32)
        sel = ((rank < MOBA_TOPK) & (jnp.abs(gt) < jnp.inf)).astype(F32)
        s_self = jnp.sum(qm * scale * kn_ref[...], axis=-1, keepdims=True)
        masked = jnp.float32(-1e30)
        n_pages = n_blocks * per_blk
        m = s_self
        for pg in range(n_pages):
            keep = sel[:, pg // per_blk:pg // per_blk + 1] > 0.0
            m = jnp.maximum(m, jnp.max(jnp.where(keep, s_ref[pg] * scale, masked), axis=-1, keepdims=True))
        e_self = jnp.exp(s_self - m)
        l = e_self
        for pg in range(n_pages):
            keep = sel[:, pg // per_blk:pg // per_blk + 1] > 0.0
            e = jnp.exp(jnp.where(keep, s_ref[pg] * scale, masked) - m)
            prob_ref[0, pg] = e
            l = l + jnp.sum(e, axis=-1, keepdims=True)
        inv = 1.0 / l
        for pg in range(n_pages):
            prob_ref[0, pg] = prob_ref[0, pg] * inv
        pself_ref[0] = jnp.broadcast_to(e_self * inv, (A_HEADS, 128))


def _moba_sample_scores(page_table, p3, cache_k, *, layer):
    db, n_pages = page_table.shape
    n_steps = n_pages // SAMPLE_PAGES_PER_STEP
    n_blocks = n_pages * PAGE_SIZE // MOBA_BLOCK
    row = lambda g_: pl.BlockSpec((None, 1, GROUP), lambda b, g, pt: (b, 0, g_))
    grid_spec = pltpu.PrefetchScalarGridSpec(
        num_scalar_prefetch=1,
        grid=(db, n_steps),
        in_specs=[row(P_Q), row(P_K)] + _page_specs(layer),
        out_specs=[pl.BlockSpec((1, n_pages, A_HEADS, PAGE_SIZE), lambda b, g, pt: (b, 0, 0, 0)),
                   pl.BlockSpec((1, A_HEADS, 128), lambda b, g, pt: (b, 0, 0))],
        scratch_shapes=[pltpu.VMEM((n_pages, A_HEADS, PAGE_SIZE), F32)],
    )
    return pl.pallas_call(
        functools.partial(_moba_scores_body, n_steps=n_steps, n_blocks=n_blocks),
        grid_spec=grid_spec,
        out_shape=[jax.ShapeDtypeStruct((db, n_pages, A_HEADS, PAGE_SIZE), F32),
                   jax.ShapeDtypeStruct((db, A_HEADS, 128), F32)],
        compiler_params=_params("parallel", "arbitrary"),
        name="moba_sample_scores",
    )(page_table, p3, p3, *([cache_k] * SAMPLE_PAGES_PER_STEP))


def _moba_out_body(pt_ref, prob_ref, pself_ref, vn_ref, *rest, n_steps):
    pages = rest[:SAMPLE_PAGES_PER_STEP]
    o_ref, acc_ref = rest[SAMPLE_PAGES_PER_STEP:]
    g = pl.program_id(1)

    @pl.when(g == 0)
    def _():
        acc_ref[...] = jnp.zeros_like(acc_ref)

    acc = acc_ref[...]
    for r in range(SAMPLE_PAGES_PER_STEP):
        pr = prob_ref[0, g * SAMPLE_PAGES_PER_STEP + r]
        acc = acc + _dot_nt(pr.astype(BF16), pages[r][...].astype(BF16))
    acc_ref[...] = acc

    @pl.when(g == n_steps - 1)
    def _():
        full = acc_ref[...] + pself_ref[0][:, 0:1] * vn_ref[...]
        lane = lax.broadcasted_iota(jnp.int32, (A_HEADS, GROUP), 1)
        hrow = lax.broadcasted_iota(jnp.int32, (A_HEADS, GROUP), 0)
        diag = jnp.where(lax.shift_right_logical(lane, 6) == hrow, full, 0.0)
        o_ref[...] = jnp.sum(diag, axis=0, keepdims=True)


def _moba_sample_out(page_table, probs, p_self, p3, cache_v, *, layer):
    db, n_pages = page_table.shape
    n_steps = n_pages // SAMPLE_PAGES_PER_STEP
    grid_spec = pltpu.PrefetchScalarGridSpec(
        num_scalar_prefetch=1,
        grid=(db, n_steps),
        in_specs=[pl.BlockSpec((1, n_pages, A_HEADS, PAGE_SIZE), lambda b, g, pt: (b, 0, 0, 0)),
                  pl.BlockSpec((1, A_HEADS, 128), lambda b, g, pt: (b, 0, 0)),
                  pl.BlockSpec((None, 1, GROUP), lambda b, g, pt: (b, 0, P_V))] + _page_specs(layer),
        out_specs=pl.BlockSpec((None, 1, GROUP), lambda b, g, pt: (b, 0, 0)),
        scratch_shapes=[pltpu.VMEM((A_HEADS, GROUP), F32)],
    )
    return pl.pallas_call(
        functools.partial(_moba_out_body, n_steps=n_steps),
        grid_spec=grid_spec,
        out_shape=jax.ShapeDtypeStruct((db, 1, GROUP), F32),
        compiler_params=_params("parallel", "arbitrary"),
        name="moba_sample_out",
    )(page_table, probs, p_self, p3, *([cache_v] * SAMPLE_PAGES_PER_STEP)).reshape(db, GROUP)


def _col_bcast(row):
    return jnp.broadcast_to(row, (128, 128)).T


def _hgrn_step_body(q_ref, lf_ref, kk_ref, v_ref, sg_ref, gain_ref, s_ref, o_ref, so_ref):
    row = slice(None)
    for h in range(HG_HEADS):
        sl = slice(HG_D * h, HG_D * (h + 1))
        f_col = _col_bcast(jnp.exp(lf_ref[row, sl]))
        k_col = _col_bcast(kk_ref[row, sl])
        q_col = _col_bcast(q_ref[row, sl])
        s_new = f_col * s_ref[0, h] + k_col * v_ref[row, sl]
        so_ref[0, h] = s_new
        o = jnp.sum(q_col * s_new, axis=0, keepdims=True)
        ms = jnp.mean(o * o, axis=-1, keepdims=True)
        o_ref[row, sl] = o * lax.rsqrt(ms + RMS_EPS) * gain_ref[...] * sg_ref[row, sl]


def _hgrn_step(p3, gain, state, *, layer):
    db = p3.shape[0]
    col = lambda g_: pl.BlockSpec((None, 1, GROUP), lambda b: (b, 0, g_))
    st_shape = (1, HG_HEADS, HG_D, HG_D)
    o, s = pl.pallas_call(
        _hgrn_step_body,
        grid=(db,),
        in_specs=[col(P_HQ), col(P_LOGF), col(P_KK), col(P_HV), col(P_HG), _const_spec((1, HG_D)),
                  pl.BlockSpec((None,) + st_shape, lambda b: (layer, b, 0, 0, 0))],
        out_specs=[pl.BlockSpec((None, 1, GROUP), lambda b: (b, 0, 0)),
                   pl.BlockSpec(st_shape, lambda b: (b, 0, 0, 0))],
        out_shape=[jax.ShapeDtypeStruct((db, 1, GROUP), F32),
                   jax.ShapeDtypeStruct((db, HG_HEADS, HG_D, HG_D), F32)],
        compiler_params=_params("parallel"),
        name="hgrn_step",
    )(p3, p3, p3, p3, p3, gain, state)
    return o.reshape(db, GROUP), s


def _ret_step_body(q_ref, k_ref, v_ref, sg_ref, s_ref, o_ref, so_ref):
    row = slice(None)
    gammas = np.exp(np.log1p(-np.exp2(-5.0 - np.arange(RET_HEADS, dtype=np.float32))).astype(np.float32))
    outs = []
    for p2 in range(RET_HEADS // 2):
        cols = slice(128 * p2, 128 * p2 + 128)
        k_col = _col_bcast(k_ref[row, cols])
        q_col = _col_bcast(q_ref[row, cols])
        vrow = v_ref[row, cols]
        for hh in range(2):
            h = 2 * p2 + hh
            rs = slice(RET_D * hh, RET_D * (hh + 1))
            s_new = float(gammas[h]) * s_ref[0, h] + k_col[rs, 0:RET_D] * vrow[:, rs]
            so_ref[0, h] = s_new
            o = jnp.sum(q_col[rs, 0:RET_D] * s_new, axis=0, keepdims=True)
            ms = jnp.mean(o * o, axis=-1, keepdims=True)
            outs.append(o * lax.rsqrt(ms + RMS_EPS))
    o_ref[row, :] = jnp.concatenate(outs, axis=1) * sg_ref[row, :]


def _ret_step(p3, state, *, layer):
    db = p3.shape[0]
    col = lambda g_: pl.BlockSpec((None, 1, GROUP), lambda b: (b, 0, g_))
    st_shape = (1, RET_HEADS, RET_D, RET_D)
    o, s = pl.pallas_call(
        _ret_step_body,
        grid=(db,),
        in_specs=[col(P_RQ), col(P_RK), col(P_RV), col(P_RG),
                  pl.BlockSpec((None,) + st_shape, lambda b: (layer, b, 0, 0, 0))],
        out_specs=[pl.BlockSpec((None, 1, GROUP), lambda b: (b, 0, 0)),
                   pl.BlockSpec(st_shape, lambda b: (b, 0, 0, 0))],
        out_shape=[jax.ShapeDtypeStruct((db, 1, GROUP), F32),
                   jax.ShapeDtypeStruct((db, RET_HEADS, RET_D, RET_D), F32)],
        compiler_params=_params("parallel"),
        name="ret_step",
    )(p3, p3, p3, p3, state)
    return o.reshape(db, GROUP), s


def _rotary_tables(pos):
    f32 = np.float32
    theta = (f32(1.0) / (f32(RET_ROPE_BASE) ** np.linspace(0.0, 1.0, RET_D // 2, dtype=f32))).astype(f32)
    ang = (pos.astype(f32)[:, None] * np.repeat(theta, 2)[None, :]).astype(f32)
    ang = np.tile(ang, (1, 128 // RET_D))
    cos, sin = np.cos(ang).astype(f32), np.sin(ang).astype(f32)
    even = (np.arange(128) % 2 == 0)[None, :]
    return tuple(jnp.asarray(a) for a in (cos, np.where(even, -sin, f32(0)), np.where(even, f32(0), sin)))


def _head_block_diag():
    r = np.arange(GROUP)
    return jnp.asarray((r[:, None] // A_HEAD_DIM) == (r[None, :] // A_HEAD_DIM), BF16)


def _layer_weights(l, lb_all, ffn1_norm, ffn1_w_in, ffn1_w_out, mix_norm, w_in, moba_q_gain, moba_k_gain,
                   hgrn_o_gain, w_branch_a, w_branch_b, w_branch_c, w_out, ffn2_norm, ffn2_w_in, ffn2_w_out):
    row = lambda a: a.reshape(1, -1).astype(F32)
    return dict(
        f1g=row(ffn1_norm[l]), f1a=ffn1_w_in[l].astype(BF16), f1b=ffn1_w_out[l].astype(BF16),
        mg=row(mix_norm[l]), w_in=w_in[l].astype(BF16),
        qg=row(jnp.tile(moba_q_gain[l], A_HEADS)), kg=row(jnp.tile(moba_k_gain[l], A_HEADS)),
        lb=row(lb_all[l]), hg=row(hgrn_o_gain[l]),
        wa=w_branch_a[l].astype(BF16), wb=w_branch_b[l].astype(BF16), wc=w_branch_c[l].astype(BF16),
        wo=w_out[l].astype(BF16),
        f2g=row(ffn2_norm[l]), f2a=ffn2_w_in[l].astype(BF16), f2b=ffn2_w_out[l].astype(BF16),
    )


def _prompt_layer(x, w, tabs, bd, *, batch, seq):
    x = _ffn(x, w["f1g"], w["f1a"], w["f1b"], tm=512)
    p, k_t, v_t = _proj(x, w["mg"], w["w_in"], w["qg"], w["kg"], w["lb"], *tabs, bd, tm=256, kv_t_batch=batch)
    oa = _moba_prompt(p, batch=batch, seq=seq)
    s0t = jnp.zeros((batch, HG_HEADS, HG_D, HG_D), F32)
    ob, s_hg = _hgrn_prompt(p, w["hg"], s0t, batch=batch, seq=seq, tc=256)
    oc, s_ret = _ret_prompt(p, batch=batch, seq=seq)
    x = _merge(oa, ob, oc, p, x, w["wa"], w["wb"], w["wc"], w["wo"], tm=512)
    x = _ffn(x, w["f2g"], w["f2a"], w["f2b"], tm=512)
    return x, k_t, v_t, s_hg, _unpair_ret_state(s_ret)


def _token_major(kv_t):
    d, b, _, t = kv_t.shape
    return jnp.transpose(kv_t.reshape(d, b, A_HEADS, A_HEAD_DIM, t), (0, 1, 4, 2, 3))


def _sample_layer(x, w, tabs, bd, l, page_table, cache_k, cache_v, state_hgrn, state_ret):
    db = x.shape[0]
    x = _ffn(x, w["f1g"], w["f1a"], w["f1b"], tm=db)
    (p,) = _proj(x, w["mg"], w["w_in"], w["qg"], w["kg"], w["lb"], *tabs, bd, tm=db)
    p3 = p.reshape(db, 1, P_GROUPS * GROUP)
    probs, p_self = _moba_sample_scores(page_table, p3, cache_k, layer=l)
    oa = _moba_sample_out(page_table, probs, p_self, p3, cache_v, layer=l)
    ob, s_hg = _hgrn_step(p3, w["hg"], state_hgrn, layer=l)
    oc, s_ret = _ret_step(p3, state_ret, layer=l)
    x = _merge(oa, ob, oc, p, x, w["wa"], w["wb"], w["wc"], w["wo"], tm=db)
    x = _ffn(x, w["f2g"], w["f2a"], w["f2b"], tm=db)
    k_new = p[:, P_K * GROUP:(P_K + 1) * GROUP].reshape(db, 1, A_HEADS, A_HEAD_DIM)
    v_new = p[:, P_V * GROUP:(P_V + 1) * GROUP].reshape(db, 1, A_HEADS, A_HEAD_DIM)
    return x, k_new, v_new, s_hg, s_ret


def kernel(x_prompt, x_sample, cache_k, cache_v, state_hgrn, state_ret, page_table, ffn1_norm, ffn1_w_in, ffn1_w_out, mix_norm, w_in, moba_q_gain, moba_k_gain, hgrn_lb, hgrn_o_gain, w_branch_a, w_branch_b, w_branch_c, w_out, ffn2_norm, ffn2_w_in, ffn2_w_out):
    lb_cum = jnp.cumsum(jax.nn.softmax(hgrn_lb.astype(F32), axis=0), axis=0)
    lb_all = lb_cum - lb_cum[:1]
    bp, seq, _ = x_prompt.shape
    db, dec_seq, _ = x_sample.shape
    assert dec_seq == 1
    n_pages = page_table.shape[1]
    past_len = n_pages * PAGE_SIZE
    assert past_len % MOBA_BLOCK == 0 and n_pages % SAMPLE_PAGES_PER_STEP == 0
    tabs_p = _rotary_tables(np.arange(seq))
    tabs_s = _rotary_tables(np.full((db,), past_len))
    bd = _head_block_diag()
    ck = _pages_feature_major(cache_k)
    cv = _pages_feature_major(cache_v)
    xp = x_prompt.reshape(bp * seq, D_MODEL)
    xs = x_sample.reshape(db, D_MODEL)
    cols = [[] for _ in range(8)]
    for l in range(DEPTH):
        w = _layer_weights(l, lb_all, ffn1_norm, ffn1_w_in, ffn1_w_out, mix_norm, w_in, moba_q_gain, moba_k_gain, hgrn_o_gain, w_branch_a, w_branch_b, w_branch_c, w_out, ffn2_norm, ffn2_w_in, ffn2_w_out)
        xp, kp, vp, hp, rp = _prompt_layer(xp, w, tabs_p, bd, batch=bp, seq=seq)
        xs, ks, vs, hs, rs = _sample_layer(xs, w, tabs_s, bd, l, page_table, ck, cv, state_hgrn, state_ret)
        for c, a in zip(cols, (kp, vp, ks, vs, hp, hs, rp, rs)):
            c.append(a)
    outs = [jnp.stack(c) for c in cols]
    outs[0], outs[1] = _token_major(outs[0]), _token_major(outs[1])
    return (xp.reshape(bp, seq, D_MODEL), xs.reshape(db, 1, D_MODEL)) + tuple(outs)
```

```python
import functools
import math

import numpy as np
import jax
import jax.numpy as jnp
from jax import lax
from jax.experimental import pallas as pl
from jax.experimental.pallas import tpu as pltpu

F32 = jnp.float32
BF16 = jnp.bfloat16

D_MODEL = 1024
D_FF = 2816
DEPTH = 4
RMS_EPS = 1e-6
A_HEADS = 8
A_HEAD_DIM = 64
MOBA_BLOCK = 256
MOBA_TOPK = 3
MOBA_ONES_ROWS = 16
HG_HEADS = 4
HG_D = 128
RET_HEADS = 8
RET_D = 64
RET_ROPE_BASE = 10000.0
PAGE_SIZE = 128
GROUP = 512

P_Q, P_K, P_V, P_HQ, P_LOGF, P_KK, P_HV, P_HG, P_RQ, P_RK, P_RV, P_RG, P_GATES = range(13)
P_GROUPS = 18

HG_CHUNK = 64
HG_SUBLANE_LEVEL = 8
RET_CHUNK = 256

VMEM_LIMIT = 56 * 1024 * 1024


def _dot(a, b):
    return jnp.dot(a, b, preferred_element_type=F32)


def _dot_nt(a, b):
    return lax.dot_general(a, b, (((1,), (1,)), ((), ())), preferred_element_type=F32)


def _split_bf16(a):
    hi = a.astype(BF16)
    lo = (a - hi.astype(F32)).astype(BF16)
    return hi, lo


def _params(*sem):
    return pltpu.CompilerParams(dimension_semantics=sem, vmem_limit_bytes=VMEM_LIMIT)


def _const_spec(shape):
    nd = len(shape)
    return pl.BlockSpec(shape, lambda *_: (0,) * nd)


def _ffn_body(x_ref, g_ref, w1_ref, w2_ref, o_ref, acc_ref, *, ff_chunk):
    x = x_ref[...]
    ms = jnp.mean(x * x, axis=-1, keepdims=True)
    h = (x * lax.rsqrt(ms + RMS_EPS) * g_ref[...]).astype(BF16)
    for c in range(D_FF // ff_chunk):
        lo = c * ff_chunk
        gate = _dot(h, w1_ref[:, lo:lo + ff_chunk])
        up = _dot(h, w1_ref[:, D_FF + lo:D_FF + lo + ff_chunk])
        act = (gate * jax.nn.sigmoid(gate) * up).astype(BF16)
        part = _dot(act, w2_ref[lo:lo + ff_chunk, :])
        if c == 0:
            acc_ref[...] = part
        else:
            acc_ref[...] += part
    o_ref[...] = x + 0.5 * acc_ref[...]


def _ffn(x, gain, w1, w2, *, tm):
    n = x.shape[0]
    return pl.pallas_call(
        functools.partial(_ffn_body, ff_chunk=256),
        grid=(n // tm,),
        in_specs=[
            pl.BlockSpec((tm, D_MODEL), lambda i: (i, 0)),
            _const_spec((1, D_MODEL)),
            pl.BlockSpec((D_MODEL, 2 * D_FF), lambda i: (0, 0), pipeline_mode=pl.Buffered(1)),
            pl.BlockSpec((D_FF, D_MODEL), lambda i: (0, 0), pipeline_mode=pl.Buffered(1)),
        ],
        out_specs=pl.BlockSpec((tm, D_MODEL), lambda i: (i, 0)),
        out_shape=jax.ShapeDtypeStruct((n, D_MODEL), F32),
        scratch_shapes=[pltpu.VMEM((tm, D_MODEL), F32)],
        compiler_params=_params("parallel"),
        name="ffn",
    )(x, gain, w1, w2)


def _proj_body(x_ref, g_ref, w_ref, qg_ref, kg_ref, lb_ref, cos_ref, sa_ref, sb_ref, bd_ref, o_ref, *kv_t_refs):
    x = x_ref[...]
    ms = jnp.mean(x * x, axis=-1, keepdims=True)
    h = (x * lax.rsqrt(ms + RMS_EPS) * g_ref[...]).astype(BF16)

    def mm(wg):
        return _dot(h, w_ref[:, wg * GROUP:(wg + 1) * GROUP])

    def put(g, val):
        o_ref[:, g * GROUP:(g + 1) * GROUP] = val

    def headnorm(a, gain_row):
        ss = _dot((a * a).astype(BF16), bd_ref[...])
        return a * lax.rsqrt(ss * (1.0 / A_HEAD_DIM) + RMS_EPS) * gain_row

    def rotary(g, a, scale):
        cos, sa, sb = cos_ref[...], sa_ref[...], sb_ref[...]
        for s in range(GROUP // 128):
            xs = a[:, 128 * s:128 * s + 128]
            nxt = pltpu.roll(xs, 127, 1)
            prv = pltpu.roll(xs, 1, 1)
            r = xs * cos + nxt * sa + prv * sb
            o_ref[:, g * GROUP + 128 * s:g * GROUP + 128 * s + 128] = r * scale if scale != 1.0 else r

    put(P_Q, headnorm(mm(0), qg_ref[...]))
    k_n = headnorm(mm(1), kg_ref[...])
    v = mm(2)
    put(P_K, k_n)
    put(P_V, v)
    if kv_t_refs:
        kv_t_refs[0][...] = k_n.T
        kv_t_refs[1][...] = v.T
    put(P_HQ, mm(3))

    fr = mm(4)
    lb = lb_ref[...]
    log_sig = jnp.minimum(fr, 0.0) - jnp.log1p(jnp.exp(-jnp.abs(fr)))
    a = jnp.log(lb)
    c = jnp.log1p(-lb) + log_sig
    put(P_LOGF, jnp.maximum(a, c) + jnp.log1p(jnp.exp(-jnp.abs(a - c))))
    put(P_KK, (1.0 - lb) * jax.nn.sigmoid(-fr))

    put(P_HV, mm(5))
    hg = mm(6)
    put(P_HG, hg * jax.nn.sigmoid(hg))
    rotary(P_RQ, mm(7), 1.0)
    rotary(P_RK, mm(8), RET_D ** -0.5)
    put(P_RV, mm(9))
    rg = mm(10)
    put(P_RG, rg * jax.nn.sigmoid(rg))
    for t in range(P_GROUPS - P_GATES):
        put(P_GATES + t, jax.nn.sigmoid(mm(11 + t)))


def _proj(x, gain, w, qg, kg, lb, cos, sa, sb, bd, *, tm, kv_t_batch=0):
    n = x.shape[0]
    n_tab = cos.shape[0] // tm
    tab = pl.BlockSpec((tm, 128), lambda i: (i % n_tab, 0))
    out_specs = [pl.BlockSpec((tm, P_GROUPS * GROUP), lambda i: (i, 0))]
    out_shape = [jax.ShapeDtypeStruct((n, P_GROUPS * GROUP), F32)]
    if kv_t_batch:
        seq = n // kv_t_batch
        nt = seq // tm
        for _ in range(2):
            out_specs.append(pl.BlockSpec((None, GROUP, tm), lambda i: (i // nt, 0, i % nt)))
            out_shape.append(jax.ShapeDtypeStruct((kv_t_batch, GROUP, seq), F32))
    return pl.pallas_call(
        _proj_body,
        grid=(n // tm,),
        in_specs=[
            pl.BlockSpec((tm, D_MODEL), lambda i: (i, 0)),
            _const_spec((1, D_MODEL)),
            pl.BlockSpec(w.shape, lambda i: (0, 0), pipeline_mode=pl.Buffered(1)),
            _const_spec((1, GROUP)),
            _const_spec((1, GROUP)),
            _const_spec((1, GROUP)),
            tab, tab, tab,
            _const_spec((GROUP, GROUP)),
        ],
        out_specs=out_specs,
        out_shape=out_shape,
        compiler_params=_params("parallel"),
        name="proj",
    )(x, gain, w, qg, kg, lb, cos, sa, sb, bd)


def _merge_body(oa_ref, ob_ref, oc_ref, ga_ref, gb_ref, gc_ref, x_ref, wa_ref, wb_ref, wc_ref, wo_ref, o_ref):
    m = ga_ref[...] * _dot(oa_ref[...].astype(BF16), wa_ref[...])
    m = m + gb_ref[...] * _dot(ob_ref[...].astype(BF16), wb_ref[...])
    m = m + gc_ref[...] * _dot(oc_ref[...].astype(BF16), wc_ref[...])
    o_ref[...] = x_ref[...] + _dot(m.astype(BF16), wo_ref[...])


def _merge(oa, ob, oc, p, x, wa, wb, wc, wo, *, tm):
    n = x.shape[0]
    row = lambda i: (i, 0)
    g0 = P_GATES // 2
    return pl.pallas_call(
        _merge_body,
        grid=(n // tm,),
        in_specs=[
            pl.BlockSpec((tm, GROUP), row),
            pl.BlockSpec((tm, GROUP), row),
            pl.BlockSpec((tm, GROUP), row),
            pl.BlockSpec((tm, D_MODEL), lambda i: (i, g0)),
            pl.BlockSpec((tm, D_MODEL), lambda i: (i, g0 + 1)),
            pl.BlockSpec((tm, D_MODEL), lambda i: (i, g0 + 2)),
            pl.BlockSpec((tm, D_MODEL), row),
            _const_spec((GROUP, D_MODEL)),
            _const_spec((GROUP, D_MODEL)),
            _const_spec((GROUP, D_MODEL)),
            _const_spec((D_MODEL, D_MODEL)),
        ],
        out_specs=pl.BlockSpec((tm, D_MODEL), row),
        out_shape=jax.ShapeDtypeStruct((n, D_MODEL), F32),
        compiler_params=_params("parallel"),
        name="merge",
    )(oa, ob, oc, p, p, p, x, wa, wb, wc, wo)


def _moba_body(q_ref, k_ref, v_ref, o_ref, k16, vt16, kmt, bias_ref, acc_ref, qmt_ref, m_ref, s_ref, *, nb):
    i = pl.program_id(1)
    blk = MOBA_BLOCK
    dh = A_HEAD_DIM
    masked = jnp.float32(-1e30)

    @pl.when(i == 0)
    def _():
        lane = lax.broadcasted_iota(jnp.int32, (A_HEADS, GROUP), 1)
        hrow = lax.broadcasted_iota(jnp.int32, (A_HEADS, GROUP), 0)
        headmask = (lax.shift_right_logical(lane, 6) == hrow).astype(F32)
        lane128 = lax.broadcasted_iota(jnp.int32, (blk, 128), 1)
        ones = jnp.ones((MOBA_ONES_ROWS, blk), BF16)

        def init(jj, c):
            r = pl.multiple_of(jj * blk, blk)
            kb = k_ref[pl.ds(r, blk), :]
            km = jnp.sum(kb, axis=0, keepdims=True) * (1.0 / blk)
            kmt[pl.ds(pl.multiple_of(jj * A_HEADS, A_HEADS), A_HEADS), :] = km * headmask
            indicator = jnp.where(lane128 == dh + jj, 1.0, 0.0)
            vt = v_ref[pl.ds(r, blk), :].T
            for h in range(A_HEADS):
                slab = kb[:, 128 * (h // 2):128 * (h // 2) + 128]
                if h % 2:
                    slab = pltpu.roll(slab, dh, 1)
                k16[jj, h] = jnp.where(lane128 < dh, slab, indicator).astype(BF16)
                vt16[jj, h, 0:dh, :] = vt[dh * h:dh * (h + 1), :].astype(BF16)
                vt16[jj, h, dh:dh + MOBA_ONES_ROWS, :] = ones
            return c

        lax.fori_loop(0, nb, init, 0)

    q = q_ref[...]
    kh, kl = _split_bf16(kmt[...])
    qh, ql = _split_bf16(q)
    gt = _dot_nt(kh, qh) + _dot_nt(kh, ql) + _dot_nt(kl, qh)
    neg_inf = jnp.float32(-jnp.inf)
    g = [jnp.where(j < i, gt[A_HEADS * j:A_HEADS * (j + 1), :], neg_inf) for j in range(nb)]
    for j in range(nb):
        rank = jnp.zeros((A_HEADS, blk), F32)
        for jp in range(nb):
            if jp == j:
                continue
            beats = (g[jp] >= g[j]) if jp < j else (g[jp] > g[j])
            rank = rank + beats.astype(F32)
        sel = (rank < MOBA_TOPK) & (jnp.abs(g[j]) < jnp.inf)
        bias = jnp.where(sel | (j == i), 0.0, masked)
        for h in range(A_HEADS):
            bias_ref[h, j:j + 1, :] = bias[h:h + 1, :]

    qt = (q * (dh ** -0.5 * math.log2(math.e))).T
    pad = jnp.zeros((128 - dh - nb, blk), F32)
    for h in range(A_HEADS):
        qmt_ref[h] = jnp.concatenate([qt[dh * h:dh * (h + 1), :], bias_ref[h], pad], axis=0).astype(BF16)

    m_ref[...] = jnp.full(m_ref.shape, masked, F32)
    acc_ref[...] = jnp.zeros_like(acc_ref)

    def block_steps(js, causal=None):
        for t, j in enumerate(js):
            for h in range(A_HEADS):
                s_ref[t, h] = _dot(k16[j, h], qmt_ref[h])
        for t, j in enumerate(js):
            for h in range(A_HEADS):
                st = s_ref[t, h]
                if causal is not None:
                    st = jnp.where(causal, st, masked)
                m_old = m_ref[h:h + 1, :]
                m_new = jnp.maximum(m_old, jnp.max(st, axis=0, keepdims=True))
                alpha = jnp.exp2(m_old - m_new)
                pt = jnp.exp2(st - m_new).astype(BF16)
                m_ref[h:h + 1, :] = m_new
                acc_ref[h] = alpha * acc_ref[h] + _dot(vt16[j, h], pt)

    kidx = lax.broadcasted_iota(jnp.int32, (blk, blk), 0)
    qidx = lax.broadcasted_iota(jnp.int32, (blk, blk), 1)
    block_steps([i], kidx <= qidx)

    @pl.when(i % 2 == 1)
    def _():
        block_steps([0])

    def body(t, c):
        j = i % 2 + 2 * t
        block_steps([j, j + 1])
        return c

    lax.fori_loop(0, i // 2, body, 0)

    per = blk // dh
    for h in range(A_HEADS):
        s_ref[0, h // per, dh * (h % per):dh * (h % per + 1), :] = acc_ref[h, 0:dh, :] / acc_ref[h, dh:dh + 1, :]
    o_ref[...] = jnp.concatenate([s_ref[0, t] for t in range(A_HEADS // per)], axis=0).T.astype(BF16)


def _moba_prompt(p, *, batch, seq):
    nb = seq // MOBA_BLOCK
    n = batch * seq
    return pl.pallas_call(
        functools.partial(_moba_body, nb=nb),
        grid=(batch, nb),
        in_specs=[
            pl.BlockSpec((MOBA_BLOCK, GROUP), lambda b, i: (b * nb + i, P_Q)),
            pl.BlockSpec((seq, GROUP), lambda b, i: (b, P_K), pipeline_mode=pl.Buffered(1)),
            pl.BlockSpec((seq, GROUP), lambda b, i: (b, P_V), pipeline_mode=pl.Buffered(1)),
        ],
        out_specs=pl.BlockSpec((MOBA_BLOCK, GROUP), lambda b, i: (b * nb + i, 0)),
        out_shape=jax.ShapeDtypeStruct((n, GROUP), BF16),
        scratch_shapes=[
            pltpu.VMEM((nb, A_HEADS, MOBA_BLOCK, 128), BF16),
            pltpu.VMEM((nb, A_HEADS, A_HEAD_DIM + MOBA_ONES_ROWS, MOBA_BLOCK), BF16),
            pltpu.VMEM((nb * A_HEADS, GROUP), F32),
            pltpu.VMEM((A_HEADS, nb, MOBA_BLOCK), F32),
            pltpu.VMEM((A_HEADS, A_HEAD_DIM + MOBA_ONES_ROWS, MOBA_BLOCK), F32),
            pltpu.VMEM((A_HEADS, 128, MOBA_BLOCK), BF16),
            pltpu.VMEM((A_HEADS, MOBA_BLOCK), F32),
            pltpu.VMEM((2, A_HEADS, MOBA_BLOCK, MOBA_BLOCK), F32),
        ],
        compiler_params=_params("parallel", "arbitrary"),
        name="moba_prompt",
    )(p, p, p)


def _hgrn_tables():
    c = HG_CHUNK
    t = np.arange(c)[:, None]
    u = np.arange(c)[None, :]
    mats = [(u <= t)]
    masks = [np.eye(c, dtype=bool)]
    m = c // 2
    while m >= 1:
        b0 = (t // (2 * m)) * (2 * m)
        right = t >= b0 + m
        if m < HG_SUBLANE_LEVEL:
            mats.append(np.where(right, (u >= b0 + m) & (u <= t), (u > t) & (u <= b0 + m - 1)))
        s = u
        masks.append(((s // (2 * m)) == (t // (2 * m))) & right & (s < b0 + m))
        m //= 2
    return (np.concatenate(mats, 0).astype(np.float32), np.stack(masks).astype(np.float32))


def _hgrn_body(q_ref, lf_ref, kk_ref, v_ref, sg_ref, gain_ref, mall_ref, masks_ref, s0_ref,
               o_ref, so_ref, st_ref, ob_ref, qd_ref, kd_ref, dl_ref, vt_ref, ex_ref, a_ref, *, n_chunks, nt):
    t = pl.program_id(1)
    c = HG_CHUNK

    @pl.when(t == 0)
    def _():
        st_ref[...] = s0_ref[0]

    row = lax.broadcasted_iota(jnp.int32, (c, 1), 0)

    n_lv = masks_ref.shape[0] - 1

    def decays(ci):
        rows = slice(ci * c, (ci + 1) * c)
        hi, lo = _split_bf16(lf_ref[rows, :])
        mall = mall_ref[...]
        e_mm = _dot(mall, hi) + _dot(mall, lo)
        b = e_mm[0:c]
        b_last = b[c - 1:c, :]
        lv = 0
        m = c // 2
        while m >= HG_SUBLANE_LEVEL:
            ref = jnp.concatenate(
                [jnp.broadcast_to(b[s + m - 1:s + m, :], (2 * m, GROUP)) for s in range(0, c, 2 * m)], axis=0)
            d = b - ref
            ex_ref[ci % 2, lv] = jnp.exp(jnp.where((row & (2 * m - 1)) >= m, d, -d))
            lv += 1
            m //= 2
        for i in range(1, mall.shape[0] // c):
            ex_ref[ci % 2, lv] = jnp.exp(e_mm[i * c:(i + 1) * c])
            lv += 1
        qd_ref[ci] = (q_ref[rows, :] * jnp.exp(b)).astype(BF16)
        kd_ref[ci] = (kk_ref[rows, :] * jnp.exp(b_last - b)).astype(BF16)
        dl_ref[ci] = jnp.exp(b_last)

    def pair_weights(ci):
        rows = slice(ci * c, (ci + 1) * c)
        for h in range(HG_HEADS):
            sl = slice(HG_D * h, HG_D * (h + 1))
            qh, kh = q_ref[rows, sl], kk_ref[rows, sl]
            a = masks_ref[0] * _dot_nt(qh.astype(BF16), kh.astype(BF16))
            for lv in range(n_lv):
                e = ex_ref[ci % 2, lv, :, sl]
                a = a + masks_ref[lv + 1] * _dot_nt((qh * e).astype(BF16), (kh * e).astype(BF16))
            a_ref[ci % 2, h] = a.astype(BF16)

    def intra_out(ci):
        rows = slice(ci * c, (ci + 1) * c)
        for h in range(HG_HEADS):
            sl = slice(HG_D * h, HG_D * (h + 1))
            vh = v_ref[rows, sl]
            ob_ref[rows, sl] = _dot(a_ref[ci % 2, h], vh.astype(BF16))
            vt_ref[ci, h] = vh.T.astype(BF16)

    def carry(ci):
        rows = slice(ci * c, (ci + 1) * c)
        for h in range(HG_HEADS):
            sl = slice(HG_D * h, HG_D * (h + 1))
            s_t = st_ref[h]
            ob_ref[rows, sl] += _dot_nt(qd_ref[ci, :, sl], s_t.astype(BF16))
            st_ref[h] = dl_ref[ci, :, sl] * s_t + _dot(vt_ref[ci, h], kd_ref[ci, :, sl])

    decays(0)
    pair_weights(0)
    for ci in range(n_chunks):
        if ci + 1 < n_chunks:
            decays(ci + 1)
        intra_out(ci)
        carry(ci)
        if ci + 1 < n_chunks:
            pair_weights(ci + 1)

    for h in range(HG_HEADS):
        sl = slice(HG_D * h, HG_D * (h + 1))
        x = ob_ref[:, sl]
        ms = jnp.mean(x * x, axis=-1, keepdims=True)
        o_ref[:, sl] = (x * lax.rsqrt(ms + RMS_EPS) * gain_ref[...] * sg_ref[:, sl]).astype(BF16)

    @pl.when(t == nt - 1)
    def _():
        for h in range(HG_HEADS):
            so_ref[0, h] = st_ref[h].T


def _hgrn_prompt(p, gain, s0t, *, batch, seq, tc):
    nt = seq // tc
    n = batch * seq
    mall, masks = _hgrn_tables()
    mall = jnp.asarray(mall, BF16)
    masks = jnp.asarray(masks, F32)
    col = lambda g: pl.BlockSpec((tc, GROUP), lambda b, t: (b * nt + t, g))
    st_spec = pl.BlockSpec((1, HG_HEADS, HG_D, HG_D), lambda b, t: (b, 0, 0, 0))
    return pl.pallas_call(
        functools.partial(_hgrn_body, n_chunks=tc // HG_CHUNK, nt=nt),
        grid=(batch, nt),
        in_specs=[col(P_HQ), col(P_LOGF), col(P_KK), col(P_HV), col(P_HG),
                  _const_spec((1, HG_D)), _const_spec(mall.shape), _const_spec(masks.shape), st_spec],
        out_specs=[pl.BlockSpec((tc, GROUP), lambda b, t: (b * nt + t, 0)), st_spec],
        out_shape=[jax.ShapeDtypeStruct((n, GROUP), BF16),
                   jax.ShapeDtypeStruct((batch, HG_HEADS, HG_D, HG_D), F32)],
        scratch_shapes=[pltpu.VMEM((HG_HEADS, HG_D, HG_D), F32), pltpu.VMEM((tc, GROUP), F32),
                        pltpu.VMEM((tc // HG_CHUNK, HG_CHUNK, GROUP), BF16),
                        pltpu.VMEM((tc // HG_CHUNK, HG_CHUNK, GROUP), BF16),
                        pltpu.VMEM((tc // HG_CHUNK, 1, GROUP), F32),
                        pltpu.VMEM((tc // HG_CHUNK, HG_HEADS, HG_D, HG_CHUNK), BF16),
                        pltpu.VMEM((2, masks.shape[0] - 1, HG_CHUNK, GROUP), F32),
                        pltpu.VMEM((2, HG_HEADS, HG_CHUNK, HG_CHUNK), BF16)],
        compiler_params=_params("parallel", "arbitrary"),
        name="hgrn_prompt",
    )(p, p, p, p, p, gain, mall, masks, s0t)


def _ret_tables():
    c = RET_CHUNK
    f32 = np.float32
    lg = np.log1p(-np.exp2(-5.0 - np.arange(RET_HEADS, dtype=f32))).astype(f32)
    t = np.arange(c, dtype=f32)
    rel = t[:, None] - t[None, :]
    intra = np.where(rel[None] >= 0, np.exp(np.maximum(rel, 0.0)[None] * lg[:, None, None]), 0.0).astype(f32)
    lane_lg = np.repeat(lg, RET_D).reshape(RET_HEADS // 2, 1, 128)
    qdec = np.exp((t + 1.0)[None, :, None] * lane_lg).astype(f32)
    kdec = np.exp((c - 1.0 - t)[None, :, None] * lane_lg).astype(f32)
    cdec = np.exp(f32(c) * lane_lg).astype(f32)
    return tuple(jnp.asarray(a) for a in (intra, qdec, kdec, cdec))


def _ret_body(q_ref, k_ref, v_ref, sg_ref, intra_ref, qdec_ref, kdec_ref, cdec_ref, o_ref, so_ref, s_ref, *, nt):
    t = pl.program_id(1)

    @pl.when(t == 0)
    def _():
        s_ref[...] = jnp.zeros_like(s_ref)

    c = RET_CHUNK
    lane = lax.broadcasted_iota(jnp.int32, (c, 128), 1)
    low = lane < RET_D
    r128 = lax.broadcasted_iota(jnp.int32, (128, 128), 0)
    c128 = lax.broadcasted_iota(jnp.int32, (128, 128), 1)
    same_head = (r128 < RET_D) == (c128 < RET_D)
    for p2 in range(RET_HEADS // 2):
        cols = slice(128 * p2, 128 * p2 + 128)
        qp, kp, vp = q_ref[:, cols], k_ref[:, cols], v_ref[:, cols]
        kb, vb = kp.astype(BF16), vp.astype(BF16)
        s_bd = s_ref[p2]
        o = _dot((qp * qdec_ref[p2]).astype(BF16), s_bd.astype(BF16))
        for hh in range(2):
            in_head = low if hh == 0 else ~low
            qm = jnp.where(in_head, qp, 0.0).astype(BF16)
            a = _dot_nt(qm, kb) * intra_ref[2 * p2 + hh]
            o = o + jnp.where(in_head, _dot(a.astype(BF16), vb), 0.0)
        upd = _dot((kp * kdec_ref[p2]).T.astype(BF16), vb)
        s_ref[p2] = cdec_ref[p2] * s_bd + jnp.where(same_head, upd, 0.0)
        sq = o * o
        s_lo = jnp.sum(jnp.where(low, sq, 0.0), axis=-1, keepdims=True)
        s_hi = jnp.sum(jnp.where(low, 0.0, sq), axis=-1, keepdims=True)
        ms = jnp.where(low, s_lo, s_hi) * (1.0 / RET_D)
        o_ref[:, cols] = (o * lax.rsqrt(ms + RMS_EPS) * sg_ref[:, cols]).astype(BF16)

    @pl.when(t == nt - 1)
    def _():
        so_ref[0] = s_ref[...]


def _ret_prompt(p, *, batch, seq):
    c = RET_CHUNK
    nt = seq // c
    n = batch * seq
    intra, qdec, kdec, cdec = _ret_tables()
    col = lambda g: pl.BlockSpec((c, GROUP), lambda b, t: (b * nt + t, g))
    npair = RET_HEADS // 2
    return pl.pallas_call(
        functools.partial(_ret_body, nt=nt),
        grid=(batch, nt),
        in_specs=[col(P_RQ), col(P_RK), col(P_RV), col(P_RG),
                  _const_spec(intra.shape), _const_spec(qdec.shape), _const_spec(kdec.shape),
                  _const_spec(cdec.shape)],
        out_specs=[pl.BlockSpec((c, GROUP), lambda b, t: (b * nt + t, 0)),
                   pl.BlockSpec((1, npair, 128, 128), lambda b, t: (b, 0, 0, 0))],
        out_shape=[jax.ShapeDtypeStruct((n, GROUP), BF16),
                   jax.ShapeDtypeStruct((batch, npair, 128, 128), F32)],
        scratch_shapes=[pltpu.VMEM((npair, 128, 128), F32)],
        compiler_params=_params("parallel", "arbitrary"),
        name="ret_prompt",
    )(p, p, p, p, intra, qdec, kdec, cdec)


def _unpair_ret_state(s_bd):
    b = s_bd.shape[0]
    lo = s_bd[:, :, :RET_D, :RET_D]
    hi = s_bd[:, :, RET_D:, RET_D:]
    return jnp.stack([lo, hi], axis=2).reshape(b, RET_HEADS, RET_D, RET_D)


SAMPLE_PAGES_PER_STEP = 32


def _pages_feature_major(cache):
    d, n_pool = cache.shape[:2]
    return jnp.transpose(cache, (0, 1, 3, 4, 2)).reshape(d, n_pool, GROUP, PAGE_SIZE)


def _page_specs(layer):
    def spec(r):
        return pl.BlockSpec((None, None, GROUP, PAGE_SIZE),
                            lambda b, g, pt: (layer, pt[b, g * SAMPLE_PAGES_PER_STEP + r], 0, 0))
    return [spec(r) for r in range(SAMPLE_PAGES_PER_STEP)]


def _head_rows(row):
    lane = lax.broadcasted_iota(jnp.int32, (A_HEADS, GROUP), 1)
    hrow = lax.broadcasted_iota(jnp.int32, (A_HEADS, GROUP), 0)
    return jnp.where(lax.shift_right_logical(lane, 6) == hrow, jnp.broadcast_to(row, (A_HEADS, GROUP)), 0.0)


def _moba_scores_body(pt_ref, q_ref, kn_ref, *rest, n_steps, n_blocks):
    pages = rest[:SAMPLE_PAGES_PER_STEP]
    prob_ref, pself_ref, sel_ref, s_ref = rest[SAMPLE_PAGES_PER_STEP:]
    g = pl.program_id(1)
    qm = _head_rows(q_ref[...])
    q_hi, q_lo = _split_bf16(qm)
    q_both = jnp.concatenate([q_hi.astype(F32), q_lo.astype(F32)], axis=0).astype(BF16)
    per_blk = MOBA_BLOCK // PAGE_SIZE
    scale = A_HEAD_DIM ** -0.5
    for r in range(SAMPLE_PAGES_PER_STEP):
        k_hi, k_lo = _split_bf16(pages[r][...])
        s2 = _dot(q_both, k_hi)
        s_ref[g * SAMPLE_PAGES_PER_STEP + r] = s2[0:A_HEADS] + s2[A_HEADS:] + _dot(q_hi, k_lo)

    @pl.when(g == n_steps - 1)
    def _():
        jidx = lax.broadcasted_iota(jnp.int32, (A_HEADS, 128), 1)
        gt = jnp.full((A_HEADS, 128), -jnp.inf, F32)
        for j in range(n_blocks):
            tot = s_ref[per_blk * j]
            for r in range(1, per_blk):
                tot = tot + s_ref[per_blk * j + r]
            gj = jnp.sum(tot, axis=-1, keepdims=True) * (1.0 / MOBA_BLOCK)
            gt = jnp.where(jidx == j, gj, gt)
        rank = jnp.zeros((A_HEADS, 128), F32)
        for jp in range(n_blocks):
            cj = gt[:, jp:jp + 1]
            beats = (cj > gt) | ((cj == gt) & (jidx > jp))
            rank = rank + beats.astype(F32)
        sel = ((rank < MOBA_TOPK) & (jnp.abs(gt) < jnp.inf)).astype(F32)
        sel_ref[0] = sel
        s_self = jnp.sum(qm * scale * kn_ref[...], axis=-1, keepdims=True)
        masked = jnp.float32(-1e30)
        n_pages = n_blocks * per_blk
        m = s_self
        for pg in range(n_pages):
            keep = sel[:, pg // per_blk:pg // per_blk + 1] > 0.0
            m = jnp.maximum(m, jnp.max(jnp.where(keep, s_ref[pg] * scale, masked), axis=-1, keepdims=True))
        e_self = jnp.exp(s_self - m)
        l = e_self
        for pg in range(n_pages):
            keep = sel[:, pg // per_blk:pg // per_blk + 1] > 0.0
            e = jnp.exp(jnp.where(keep, s_ref[pg] * scale, masked) - m)
            prob_ref[0, pg] = e
            l = l + jnp.sum(e, axis=-1, keepdims=True)
        inv = 1.0 / l
        for pg in range(n_pages):
            prob_ref[0, pg] = prob_ref[0, pg] * inv
        pself_ref[0] = jnp.broadcast_to(e_self * inv, (A_HEADS, 128))


def _moba_sample_scores(page_table, p3, cache_k, *, layer):
    db, n_pages = page_table.shape
    n_steps = n_pages // SAMPLE_PAGES_PER_STEP
    n_blocks = n_pages * PAGE_SIZE // MOBA_BLOCK
    row = lambda g_: pl.BlockSpec((None, 1, GROUP), lambda b, g, pt: (b, 0, g_))
    grid_spec = pltpu.PrefetchScalarGridSpec(
        num_scalar_prefetch=1,
        grid=(db, n_steps),
        in_specs=[row(P_Q), row(P_K)] + _page_specs(layer),
        out_specs=[pl.BlockSpec((1, n_pages, A_HEADS, PAGE_SIZE), lambda b, g, pt: (b, 0, 0, 0)),
                   pl.BlockSpec((1, A_HEADS, 128), lambda b, g, pt: (b, 0, 0)),
                   pl.BlockSpec((1, A_HEADS, 128), lambda b, g, pt: (b, 0, 0))],
        scratch_shapes=[pltpu.VMEM((n_pages, A_HEADS, PAGE_SIZE), F32)],
    )
    return pl.pallas_call(
        functools.partial(_moba_scores_body, n_steps=n_steps, n_blocks=n_blocks),
        grid_spec=grid_spec,
        out_shape=[jax.ShapeDtypeStruct((db, n_pages, A_HEADS, PAGE_SIZE), F32),
                   jax.ShapeDtypeStruct((db, A_HEADS, 128), F32),
                   jax.ShapeDtypeStruct((db, A_HEADS, 128), F32)],
        compiler_params=_params("parallel", "arbitrary"),
        name="moba_sample_scores",
    )(page_table, p3, p3, *([cache_k] * SAMPLE_PAGES_PER_STEP))


def _selected_pages(page_table, sel):
    db, n_pages = page_table.shape
    per_blk = MOBA_BLOCK // PAGE_SIZE
    n_steps = n_pages // SAMPLE_PAGES_PER_STEP
    used = jnp.repeat(jnp.max(sel, axis=1)[:, :n_pages // per_blk] > 0.0, per_blk, axis=1)
    pages = page_table.reshape(db * n_steps, SAMPLE_PAGES_PER_STEP)
    used = used.reshape(db * n_steps, SAMPLE_PAGES_PER_STEP)
    order = jnp.arange(db * n_steps, dtype=jnp.int32)[:, None]
    last_used = lax.cummax(jnp.where(used, order, 0), axis=0)
    return jnp.take_along_axis(pages, last_used, axis=0).reshape(db, n_pages)


def _moba_out_body(pt_ref, prob_ref, pself_ref, vn_ref, *rest, n_steps):
    pages = rest[:SAMPLE_PAGES_PER_STEP]
    o_ref, acc_ref = rest[SAMPLE_PAGES_PER_STEP:]
    g = pl.program_id(1)

    @pl.when(g == 0)
    def _():
        acc_ref[...] = jnp.zeros_like(acc_ref)

    acc = acc_ref[...]
    for r in range(SAMPLE_PAGES_PER_STEP):
        pr = prob_ref[0, g * SAMPLE_PAGES_PER_STEP + r]
        acc = acc + _dot_nt(pr.astype(BF16), pages[r][...].astype(BF16))
    acc_ref[...] = acc

    @pl.when(g == n_steps - 1)
    def _():
        full = acc_ref[...] + pself_ref[0][:, 0:1] * vn_ref[...]
        lane = lax.broadcasted_iota(jnp.int32, (A_HEADS, GROUP), 1)
        hrow = lax.broadcasted_iota(jnp.int32, (A_HEADS, GROUP), 0)
        diag = jnp.where(lax.shift_right_logical(lane, 6) == hrow, full, 0.0)
        o_ref[...] = jnp.sum(diag, axis=0, keepdims=True)


def _moba_sample_out(page_table, probs, p_self, p3, cache_v, *, layer):
    db, n_pages = page_table.shape
    n_steps = n_pages // SAMPLE_PAGES_PER_STEP
    grid_spec = pltpu.PrefetchScalarGridSpec(
        num_scalar_prefetch=1,
        grid=(db, n_steps),
        in_specs=[pl.BlockSpec((1, n_pages, A_HEADS, PAGE_SIZE), lambda b, g, pt: (b, 0, 0, 0)),
                  pl.BlockSpec((1, A_HEADS, 128), lambda b, g, pt: (b, 0, 0)),
                  pl.BlockSpec((None, 1, GROUP), lambda b, g, pt: (b, 0, P_V))] + _page_specs(layer),
        out_specs=pl.BlockSpec((None, 1, GROUP), lambda b, g, pt: (b, 0, 0)),
        scratch_shapes=[pltpu.VMEM((A_HEADS, GROUP), F32)],
    )
    return pl.pallas_call(
        functools.partial(_moba_out_body, n_steps=n_steps),
        grid_spec=grid_spec,
        out_shape=jax.ShapeDtypeStruct((db, 1, GROUP), F32),
        compiler_params=_params("parallel", "arbitrary"),
        name="moba_sample_out",
    )(page_table, probs, p_self, p3, *([cache_v] * SAMPLE_PAGES_PER_STEP)).reshape(db, GROUP)


def _col_bcast(row):
    return jnp.broadcast_to(row, (128, 128)).T


def _hgrn_step_body(q_ref, lf_ref, kk_ref, v_ref, sg_ref, gain_ref, s_ref, o_ref, so_ref):
    row = slice(None)
    for h in range(HG_HEADS):
        sl = slice(HG_D * h, HG_D * (h + 1))
        f_col = _col_bcast(jnp.exp(lf_ref[row, sl]))
        k_col = _col_bcast(kk_ref[row, sl])
        q_col = _col_bcast(q_ref[row, sl])
        s_new = f_col * s_ref[0, h] + k_col * v_ref[row, sl]
        so_ref[0, h] = s_new
        o = jnp.sum(q_col * s_new, axis=0, keepdims=True)
        ms = jnp.mean(o * o, axis=-1, keepdims=True)
        o_ref[row, sl] = o * lax.rsqrt(ms + RMS_EPS) * gain_ref[...] * sg_ref[row, sl]


def _hgrn_step(p3, gain, state, *, layer):
    db = p3.shape[0]
    col = lambda g_: pl.BlockSpec((None, 1, GROUP), lambda b: (b, 0, g_))
    st_shape = (1, HG_HEADS, HG_D, HG_D)
    o, s = pl.pallas_call(
        _hgrn_step_body,
        grid=(db,),
        in_specs=[col(P_HQ), col(P_LOGF), col(P_KK), col(P_HV), col(P_HG), _const_spec((1, HG_D)),
                  pl.BlockSpec((None,) + st_shape, lambda b: (layer, b, 0, 0, 0))],
        out_specs=[pl.BlockSpec((None, 1, GROUP), lambda b: (b, 0, 0)),
                   pl.BlockSpec(st_shape, lambda b: (b, 0, 0, 0))],
        out_shape=[jax.ShapeDtypeStruct((db, 1, GROUP), F32),
                   jax.ShapeDtypeStruct((db, HG_HEADS, HG_D, HG_D), F32)],
        compiler_params=_params("parallel"),
        name="hgrn_step",
    )(p3, p3, p3, p3, p3, gain, state)
    return o.reshape(db, GROUP), s


def _ret_step_body(q_ref, k_ref, v_ref, sg_ref, s_ref, o_ref, so_ref):
    row = slice(None)
    gammas = np.exp(np.log1p(-np.exp2(-5.0 - np.arange(RET_HEADS, dtype=np.float32))).astype(np.float32))
    outs = []
    for p2 in range(RET_HEADS // 2):
        cols = slice(128 * p2, 128 * p2 + 128)
        k_col = _col_bcast(k_ref[row, cols])
        q_col = _col_bcast(q_ref[row, cols])
        vrow = v_ref[row, cols]
        for hh in range(2):
            h = 2 * p2 + hh
            rs = slice(RET_D * hh, RET_D * (hh + 1))
            s_new = float(gammas[h]) * s_ref[0, h] + k_col[rs, 0:RET_D] * vrow[:, rs]
            so_ref[0, h] = s_new
            o = jnp.sum(q_col[rs, 0:RET_D] * s_new, axis=0, keepdims=True)
            ms = jnp.mean(o * o, axis=-1, keepdims=True)
            outs.append(o * lax.rsqrt(ms + RMS_EPS))
    o_ref[row, :] = jnp.concatenate(outs, axis=1) * sg_ref[row, :]


def _ret_step(p3, state, *, layer):
    db = p3.shape[0]
    col = lambda g_: pl.BlockSpec((None, 1, GROUP), lambda b: (b, 0, g_))
    st_shape = (1, RET_HEADS, RET_D, RET_D)
    o, s = pl.pallas_call(
        _ret_step_body,
        grid=(db,),
        in_specs=[col(P_RQ), col(P_RK), col(P_RV), col(P_RG),
                  pl.BlockSpec((None,) + st_shape, lambda b: (layer, b, 0, 0, 0))],
        out_specs=[pl.BlockSpec((None, 1, GROUP), lambda b: (b, 0, 0)),
                   pl.BlockSpec(st_shape, lambda b: (b, 0, 0, 0))],
        out_shape=[jax.ShapeDtypeStruct((db, 1, GROUP), F32),
                   jax.ShapeDtypeStruct((db, RET_HEADS, RET_D, RET_D), F32)],
        compiler_params=_params("parallel"),
        name="ret_step",
    )(p3, p3, p3, p3, state)
    return o.reshape(db, GROUP), s


def _rotary_tables(pos):
    f32 = np.float32
    theta = (f32(1.0) / (f32(RET_ROPE_BASE) ** np.linspace(0.0, 1.0, RET_D // 2, dtype=f32))).astype(f32)
    ang = (pos.astype(f32)[:, None] * np.repeat(theta, 2)[None, :]).astype(f32)
    ang = np.tile(ang, (1, 128 // RET_D))
    cos, sin = np.cos(ang).astype(f32), np.sin(ang).astype(f32)
    even = (np.arange(128) % 2 == 0)[None, :]
    return tuple(jnp.asarray(a) for a in (cos, np.where(even, -sin, f32(0)), np.where(even, f32(0), sin)))


def _head_block_diag():
    r = np.arange(GROUP)
    return jnp.asarray((r[:, None] // A_HEAD_DIM) == (r[None, :] // A_HEAD_DIM), BF16)


def _layer_weights(l, lb_all, ffn1_norm, ffn1_w_in, ffn1_w_out, mix_norm, w_in, moba_q_gain, moba_k_gain,
                   hgrn_o_gain, w_branch_a, w_branch_b, w_branch_c, w_out, ffn2_norm, ffn2_w_in, ffn2_w_out):
    row = lambda a: a.reshape(1, -1).astype(F32)
    return dict(
        f1g=row(ffn1_norm[l]), f1a=ffn1_w_in[l].astype(BF16), f1b=ffn1_w_out[l].astype(BF16),
        mg=row(mix_norm[l]), w_in=w_in[l].astype(BF16),
        qg=row(jnp.tile(moba_q_gain[l], A_HEADS)), kg=row(jnp.tile(moba_k_gain[l], A_HEADS)),
        lb=row(lb_all[l]), hg=row(hgrn_o_gain[l]),
        wa=w_branch_a[l].astype(BF16), wb=w_branch_b[l].astype(BF16), wc=w_branch_c[l].astype(BF16),
        wo=w_out[l].astype(BF16),
        f2g=row(ffn2_norm[l]), f2a=ffn2_w_in[l].astype(BF16), f2b=ffn2_w_out[l].astype(BF16),
    )


def _prompt_layer(x, w, tabs, bd, *, batch, seq):
    x = _ffn(x, w["f1g"], w["f1a"], w["f1b"], tm=512)
    p, k_t, v_t = _proj(x, w["mg"], w["w_in"], w["qg"], w["kg"], w["lb"], *tabs, bd, tm=256, kv_t_batch=batch)
    oa = _moba_prompt(p, batch=batch, seq=seq)
    s0t = jnp.zeros((batch, HG_HEADS, HG_D, HG_D), F32)
    ob, s_hg = _hgrn_prompt(p, w["hg"], s0t, batch=batch, seq=seq, tc=256)
    oc, s_ret = _ret_prompt(p, batch=batch, seq=seq)
    x = _merge(oa, ob, oc, p, x, w["wa"], w["wb"], w["wc"], w["wo"], tm=512)
    x = _ffn(x, w["f2g"], w["f2a"], w["f2b"], tm=512)
    return x, k_t, v_t, s_hg, _unpair_ret_state(s_ret)


def _token_major(kv_t):
    d, b, _, t = kv_t.shape
    return jnp.transpose(kv_t.reshape(d, b, A_HEADS, A_HEAD_DIM, t), (0, 1, 4, 2, 3))


def _sample_layer(x, w, tabs, bd, l, page_table, cache_k, cache_v, state_hgrn, state_ret):
    db = x.shape[0]
    x = _ffn(x, w["f1g"], w["f1a"], w["f1b"], tm=db)
    (p,) = _proj(x, w["mg"], w["w_in"], w["qg"], w["kg"], w["lb"], *tabs, bd, tm=db)
    p3 = p.reshape(db, 1, P_GROUPS * GROUP)
    probs, p_self, sel = _moba_sample_scores(page_table, p3, cache_k, layer=l)
    oa = _moba_sample_out(_selected_pages(page_table, sel), probs, p_self, p3, cache_v, layer=l)
    ob, s_hg = _hgrn_step(p3, w["hg"], state_hgrn, layer=l)
    oc, s_ret = _ret_step(p3, state_ret, layer=l)
    x = _merge(oa, ob, oc, p, x, w["wa"], w["wb"], w["wc"], w["wo"], tm=db)
    x = _ffn(x, w["f2g"], w["f2a"], w["f2b"], tm=db)
    k_new = p[:, P_K * GROUP:(P_K + 1) * GROUP].reshape(db, 1, A_HEADS, A_HEAD_DIM)
    v_new = p[:, P_V * GROUP:(P_V + 1) * GROUP].reshape(db, 1, A_HEADS, A_HEAD_DIM)
    return x, k_new, v_new, s_hg, s_ret


def kernel(x_prompt, x_sample, cache_k, cache_v, state_hgrn, state_ret, page_table, ffn1_norm, ffn1_w_in, ffn1_w_out, mix_norm, w_in, moba_q_gain, moba_k_gain, hgrn_lb, hgrn_o_gain, w_branch_a, w_branch_b, w_branch_c, w_out, ffn2_norm, ffn2_w_in, ffn2_w_out):
    lb_cum = jnp.cumsum(jax.nn.softmax(hgrn_lb.astype(F32), axis=0), axis=0)
    lb_all = lb_cum - lb_cum[:1]
    bp, seq, _ = x_prompt.shape
    db, dec_seq, _ = x_sample.shape
    assert dec_seq == 1
    n_pages = page_table.shape[1]
    past_len = n_pages * PAGE_SIZE
    assert past_len % MOBA_BLOCK == 0 and n_pages % SAMPLE_PAGES_PER_STEP == 0
    tabs_p = _rotary_tables(np.arange(seq))
    tabs_s = _rotary_tables(np.full((db,), past_len))
    bd = _head_block_diag()
    ck = _pages_feature_major(cache_k)
    cv = _pages_feature_major(cache_v)
    xp = x_prompt.reshape(bp * seq, D_MODEL)
    xs = x_sample.reshape(db, D_MODEL)
    cols = [[] for _ in range(8)]
    for l in range(DEPTH):
        w = _layer_weights(l, lb_all, ffn1_norm, ffn1_w_in, ffn1_w_out, mix_norm, w_in, moba_q_gain, moba_k_gain, hgrn_o_gain, w_branch_a, w_branch_b, w_branch_c, w_out, ffn2_norm, ffn2_w_in, ffn2_w_out)
        xp, kp, vp, hp, rp = _prompt_layer(xp, w, tabs_p, bd, batch=bp, seq=seq)
        xs, ks, vs, hs, rs = _sample_layer(xs, w, tabs_s, bd, l, page_table, ck, cv, state_hgrn, state_ret)
        for c, a in zip(cols, (kp, vp, ks, vs, hp, hs, rp, rs)):
            c.append(a)
    outs = [jnp.stack(c) for c in cols]
    outs[0], outs[1] = _token_major(outs[0]), _token_major(outs[1])
    return (xp.reshape(bp, seq, D_MODEL), xs.reshape(db, 1, D_MODEL)) + tuple(outs)
```

```python
import functools
import math

import numpy as np
import jax
import jax.numpy as jnp
from jax import lax
from jax.experimental import pallas as pl
from jax.experimental.pallas import tpu as pltpu

F32 = jnp.float32
BF16 = jnp.bfloat16

D_MODEL = 1024
D_FF = 2816
DEPTH = 4
RMS_EPS = 1e-6
A_HEADS = 8
A_HEAD_DIM = 64
MOBA_BLOCK = 256
MOBA_TOPK = 3
MOBA_ONES_ROWS = 16
HG_HEADS = 4
HG_D = 128
RET_HEADS = 8
RET_D = 64
RET_ROPE_BASE = 10000.0
PAGE_SIZE = 128
GROUP = 512

P_Q, P_K, P_V, P_HQ, P_LOGF, P_KK, P_HV, P_HG, P_RQ, P_RK, P_RV, P_RG, P_GATES = range(13)
P_GROUPS = 18

HG_CHUNK = 64
HG_SUBLANE_LEVEL = 8
RET_CHUNK = 256

VMEM_LIMIT = 56 * 1024 * 1024


def _dot(a, b):
    return jnp.dot(a, b, preferred_element_type=F32)


def _dot_nt(a, b):
    return lax.dot_general(a, b, (((1,), (1,)), ((), ())), preferred_element_type=F32)


def _split_bf16(a):
    hi = a.astype(BF16)
    lo = (a - hi.astype(F32)).astype(BF16)
    return hi, lo


def _params(*sem):
    return pltpu.CompilerParams(dimension_semantics=sem, vmem_limit_bytes=VMEM_LIMIT)


def _const_spec(shape):
    nd = len(shape)
    return pl.BlockSpec(shape, lambda *_: (0,) * nd)


def _ffn_body(x_ref, g_ref, w1_ref, w2_ref, o_ref, acc_ref, *, ff_chunk):
    x = x_ref[...]
    ms = jnp.mean(x * x, axis=-1, keepdims=True)
    h = (x * lax.rsqrt(ms + RMS_EPS) * g_ref[...]).astype(BF16)
    for c in range(D_FF // ff_chunk):
        lo = c * ff_chunk
        gate = _dot(h, w1_ref[:, lo:lo + ff_chunk])
        up = _dot(h, w1_ref[:, D_FF + lo:D_FF + lo + ff_chunk])
        act = (gate * jax.nn.sigmoid(gate) * up).astype(BF16)
        part = _dot(act, w2_ref[lo:lo + ff_chunk, :])
        if c == 0:
            acc_ref[...] = part
        else:
            acc_ref[...] += part
    o_ref[...] = x + 0.5 * acc_ref[...]


def _ffn(x, gain, w1, w2, *, tm):
    n = x.shape[0]
    return pl.pallas_call(
        functools.partial(_ffn_body, ff_chunk=256),
        grid=(n // tm,),
        in_specs=[
            pl.BlockSpec((tm, D_MODEL), lambda i: (i, 0)),
            _const_spec((1, D_MODEL)),
            pl.BlockSpec((D_MODEL, 2 * D_FF), lambda i: (0, 0), pipeline_mode=pl.Buffered(1)),
            pl.BlockSpec((D_FF, D_MODEL), lambda i: (0, 0), pipeline_mode=pl.Buffered(1)),
        ],
        out_specs=pl.BlockSpec((tm, D_MODEL), lambda i: (i, 0)),
        out_shape=jax.ShapeDtypeStruct((n, D_MODEL), F32),
        scratch_shapes=[pltpu.VMEM((tm, D_MODEL), F32)],
        compiler_params=_params("parallel"),
        name="ffn",
    )(x, gain, w1, w2)


def _proj_body(x_ref, g_ref, w_ref, qg_ref, kg_ref, lb_ref, cos_ref, sa_ref, sb_ref, bd_ref, o_ref, *kv_t_refs):
    x = x_ref[...]
    ms = jnp.mean(x * x, axis=-1, keepdims=True)
    h = (x * lax.rsqrt(ms + RMS_EPS) * g_ref[...]).astype(BF16)

    def mm(wg):
        return _dot(h, w_ref[:, wg * GROUP:(wg + 1) * GROUP])

    def put(g, val):
        o_ref[:, g * GROUP:(g + 1) * GROUP] = val

    def headnorm(a, gain_row):
        ss = _dot((a * a).astype(BF16), bd_ref[...])
        return a * lax.rsqrt(ss * (1.0 / A_HEAD_DIM) + RMS_EPS) * gain_row

    def rotary(g, a, scale):
        cos, sa, sb = cos_ref[...], sa_ref[...], sb_ref[...]
        for s in range(GROUP // 128):
            xs = a[:, 128 * s:128 * s + 128]
            nxt = pltpu.roll(xs, 127, 1)
            prv = pltpu.roll(xs, 1, 1)
            r = xs * cos + nxt * sa + prv * sb
            o_ref[:, g * GROUP + 128 * s:g * GROUP + 128 * s + 128] = r * scale if scale != 1.0 else r

    put(P_Q, headnorm(mm(0), qg_ref[...]))
    k_n = headnorm(mm(1), kg_ref[...])
    v = mm(2)
    put(P_K, k_n)
    put(P_V, v)
    if kv_t_refs:
        kv_t_refs[0][...] = k_n.T
        kv_t_refs[1][...] = v.T
    put(P_HQ, mm(3))

    fr = mm(4)
    lb = lb_ref[...]
    log_sig = jnp.minimum(fr, 0.0) - jnp.log1p(jnp.exp(-jnp.abs(fr)))
    a = jnp.log(lb)
    c = jnp.log1p(-lb) + log_sig
    put(P_LOGF, jnp.maximum(a, c) + jnp.log1p(jnp.exp(-jnp.abs(a - c))))
    put(P_KK, (1.0 - lb) * jax.nn.sigmoid(-fr))

    put(P_HV, mm(5))
    hg = mm(6)
    put(P_HG, hg * jax.nn.sigmoid(hg))
    rotary(P_RQ, mm(7), 1.0)
    rotary(P_RK, mm(8), RET_D ** -0.5)
    put(P_RV, mm(9))
    rg = mm(10)
    put(P_RG, rg * jax.nn.sigmoid(rg))
    for t in range(P_GROUPS - P_GATES):
        put(P_GATES + t, jax.nn.sigmoid(mm(11 + t)))


def _proj(x, gain, w, qg, kg, lb, cos, sa, sb, bd, *, tm, kv_t_batch=0):
    n = x.shape[0]
    n_tab = cos.shape[0] // tm
    tab = pl.BlockSpec((tm, 128), lambda i: (i % n_tab, 0))
    out_specs = [pl.BlockSpec((tm, P_GROUPS * GROUP), lambda i: (i, 0))]
    out_shape = [jax.ShapeDtypeStruct((n, P_GROUPS * GROUP), F32)]
    if kv_t_batch:
        seq = n // kv_t_batch
        nt = seq // tm
        for _ in range(2):
            out_specs.append(pl.BlockSpec((None, GROUP, tm), lambda i: (i // nt, 0, i % nt)))
            out_shape.append(jax.ShapeDtypeStruct((kv_t_batch, GROUP, seq), F32))
    return pl.pallas_call(
        _proj_body,
        grid=(n // tm,),
        in_specs=[
            pl.BlockSpec((tm, D_MODEL), lambda i: (i, 0)),
            _const_spec((1, D_MODEL)),
            pl.BlockSpec(w.shape, lambda i: (0, 0), pipeline_mode=pl.Buffered(1)),
            _const_spec((1, GROUP)),
            _const_spec((1, GROUP)),
            _const_spec((1, GROUP)),
            tab, tab, tab,
            _const_spec((GROUP, GROUP)),
        ],
        out_specs=out_specs,
        out_shape=out_shape,
        compiler_params=_params("parallel"),
        name="proj",
    )(x, gain, w, qg, kg, lb, cos, sa, sb, bd)


def _merge_body(oa_ref, ob_ref, oc_ref, ga_ref, gb_ref, gc_ref, x_ref, wa_ref, wb_ref, wc_ref, wo_ref, o_ref):
    m = ga_ref[...] * _dot(oa_ref[...].astype(BF16), wa_ref[...])
    m = m + gb_ref[...] * _dot(ob_ref[...].astype(BF16), wb_ref[...])
    m = m + gc_ref[...] * _dot(oc_ref[...].astype(BF16), wc_ref[...])
    o_ref[...] = x_ref[...] + _dot(m.astype(BF16), wo_ref[...])


def _merge(oa, ob, oc, p, x, wa, wb, wc, wo, *, tm):
    n = x.shape[0]
    row = lambda i: (i, 0)
    g0 = P_GATES // 2
    return pl.pallas_call(
        _merge_body,
        grid=(n // tm,),
        in_specs=[
            pl.BlockSpec((tm, GROUP), row),
            pl.BlockSpec((tm, GROUP), row),
            pl.BlockSpec((tm, GROUP), row),
            pl.BlockSpec((tm, D_MODEL), lambda i: (i, g0)),
            pl.BlockSpec((tm, D_MODEL), lambda i: (i, g0 + 1)),
            pl.BlockSpec((tm, D_MODEL), lambda i: (i, g0 + 2)),
            pl.BlockSpec((tm, D_MODEL), row),
            _const_spec((GROUP, D_MODEL)),
            _const_spec((GROUP, D_MODEL)),
            _const_spec((GROUP, D_MODEL)),
            _const_spec((D_MODEL, D_MODEL)),
        ],
        out_specs=pl.BlockSpec((tm, D_MODEL), row),
        out_shape=jax.ShapeDtypeStruct((n, D_MODEL), F32),
        compiler_params=_params("parallel"),
        name="merge",
    )(oa, ob, oc, p, p, p, x, wa, wb, wc, wo)


def _moba_body(q_ref, k_ref, v_ref, o_ref, k16, vt16, kmt, bias_ref, acc_ref, qmt_ref, m_ref, s_ref, *, nb):
    i = pl.program_id(1)
    blk = MOBA_BLOCK
    dh = A_HEAD_DIM
    masked = jnp.float32(-1e30)

    @pl.when(i == 0)
    def _():
        lane = lax.broadcasted_iota(jnp.int32, (A_HEADS, GROUP), 1)
        hrow = lax.broadcasted_iota(jnp.int32, (A_HEADS, GROUP), 0)
        headmask = (lax.shift_right_logical(lane, 6) == hrow).astype(F32)
        lane128 = lax.broadcasted_iota(jnp.int32, (blk, 128), 1)
        ones = jnp.ones((MOBA_ONES_ROWS, blk), BF16)

        def init(jj, c):
            r = pl.multiple_of(jj * blk, blk)
            kb = k_ref[pl.ds(r, blk), :]
            km = jnp.sum(kb, axis=0, keepdims=True) * (1.0 / blk)
            kmt[pl.ds(pl.multiple_of(jj * A_HEADS, A_HEADS), A_HEADS), :] = km * headmask
            indicator = jnp.where(lane128 == dh + jj, 1.0, 0.0)
            vt = v_ref[pl.ds(r, blk), :].T
            for h in range(A_HEADS):
                slab = kb[:, 128 * (h // 2):128 * (h // 2) + 128]
                if h % 2:
                    slab = pltpu.roll(slab, dh, 1)
                k16[jj, h] = jnp.where(lane128 < dh, slab, indicator).astype(BF16)
                vt16[jj, h, 0:dh, :] = vt[dh * h:dh * (h + 1), :].astype(BF16)
                vt16[jj, h, dh:dh + MOBA_ONES_ROWS, :] = ones
            return c

        lax.fori_loop(0, nb, init, 0)

    q = q_ref[...]
    kh, kl = _split_bf16(kmt[...])
    qh, ql = _split_bf16(q)
    gt = _dot_nt(kh, qh) + _dot_nt(kh, ql) + _dot_nt(kl, qh)
    neg_inf = jnp.float32(-jnp.inf)
    g = [jnp.where(j < i, gt[A_HEADS * j:A_HEADS * (j + 1), :], neg_inf) for j in range(nb)]
    for j in range(nb):
        rank = jnp.zeros((A_HEADS, blk), F32)
        for jp in range(nb):
            if jp == j:
                continue
            beats = (g[jp] >= g[j]) if jp < j else (g[jp] > g[j])
            rank = rank + beats.astype(F32)
        sel = (rank < MOBA_TOPK) & (jnp.abs(g[j]) < jnp.inf)
        bias = jnp.where(sel | (j == i), 0.0, masked)
        for h in range(A_HEADS):
            bias_ref[h, j:j + 1, :] = bias[h:h + 1, :]

    qt = (q * (dh ** -0.5 * math.log2(math.e))).T
    pad = jnp.zeros((128 - dh - nb, blk), F32)
    for h in range(A_HEADS):
        qmt_ref[h] = jnp.concatenate([qt[dh * h:dh * (h + 1), :], bias_ref[h], pad], axis=0).astype(BF16)

    m_ref[...] = jnp.full(m_ref.shape, masked, F32)
    acc_ref[...] = jnp.zeros_like(acc_ref)

    def block_steps(js, causal=None):
        for t, j in enumerate(js):
            for h in range(A_HEADS):
                s_ref[t, h] = _dot(k16[j, h], qmt_ref[h])
        for t, j in enumerate(js):
            for h in range(A_HEADS):
                st = s_ref[t, h]
                if causal is not None:
                    st = jnp.where(causal, st, masked)
                m_old = m_ref[h:h + 1, :]
                m_new = jnp.maximum(m_old, jnp.max(st, axis=0, keepdims=True))
                alpha = jnp.exp2(m_old - m_new)
                pt = jnp.exp2(st - m_new).astype(BF16)
                m_ref[h:h + 1, :] = m_new
                acc_ref[h] = alpha * acc_ref[h] + _dot(vt16[j, h], pt)

    kidx = lax.broadcasted_iota(jnp.int32, (blk, blk), 0)
    qidx = lax.broadcasted_iota(jnp.int32, (blk, blk), 1)
    block_steps([i], kidx <= qidx)

    @pl.when(i % 2 == 1)
    def _():
        block_steps([0])

    def body(t, c):
        j = i % 2 + 2 * t
        block_steps([j, j + 1])
        return c

    lax.fori_loop(0, i // 2, body, 0)

    per = blk // dh
    for h in range(A_HEADS):
        s_ref[0, h // per, dh * (h % per):dh * (h % per + 1), :] = acc_ref[h, 0:dh, :] / acc_ref[h, dh:dh + 1, :]
    o_ref[...] = jnp.concatenate([s_ref[0, t] for t in range(A_HEADS // per)], axis=0).T.astype(BF16)


def _moba_prompt(p, *, batch, seq):
    nb = seq // MOBA_BLOCK
    n = batch * seq
    return pl.pallas_call(
        functools.partial(_moba_body, nb=nb),
        grid=(batch, nb),
        in_specs=[
            pl.BlockSpec((MOBA_BLOCK, GROUP), lambda b, i: (b * nb + i, P_Q)),
            pl.BlockSpec((seq, GROUP), lambda b, i: (b, P_K), pipeline_mode=pl.Buffered(1)),
            pl.BlockSpec((seq, GROUP), lambda b, i: (b, P_V), pipeline_mode=pl.Buffered(1)),
        ],
        out_specs=pl.BlockSpec((MOBA_BLOCK, GROUP), lambda b, i: (b * nb + i, 0)),
        out_shape=jax.ShapeDtypeStruct((n, GROUP), BF16),
        scratch_shapes=[
            pltpu.VMEM((nb, A_HEADS, MOBA_BLOCK, 128), BF16),
            pltpu.VMEM((nb, A_HEADS, A_HEAD_DIM + MOBA_ONES_ROWS, MOBA_BLOCK), BF16),
            pltpu.VMEM((nb * A_HEADS, GROUP), F32),
            pltpu.VMEM((A_HEADS, nb, MOBA_BLOCK), F32),
            pltpu.VMEM((A_HEADS, A_HEAD_DIM + MOBA_ONES_ROWS, MOBA_BLOCK), F32),
            pltpu.VMEM((A_HEADS, 128, MOBA_BLOCK), BF16),
            pltpu.VMEM((A_HEADS, MOBA_BLOCK), F32),
            pltpu.VMEM((2, A_HEADS, MOBA_BLOCK, MOBA_BLOCK), F32),
        ],
        compiler_params=_params("parallel", "arbitrary"),
        name="moba_prompt",
    )(p, p, p)


def _hgrn_tables():
    c = HG_CHUNK
    t = np.arange(c)[:, None]
    u = np.arange(c)[None, :]
    mats = [(u <= t)]
    masks = [np.eye(c, dtype=bool)]
    m = c // 2
    while m >= 1:
        b0 = (t // (2 * m)) * (2 * m)
        right = t >= b0 + m
        if m < HG_SUBLANE_LEVEL:
            mats.append(np.where(right, (u >= b0 + m) & (u <= t), (u > t) & (u <= b0 + m - 1)))
        s = u
        masks.append(((s // (2 * m)) == (t // (2 * m))) & right & (s < b0 + m))
        m //= 2
    return (np.concatenate(mats, 0).astype(np.float32), np.stack(masks).astype(np.float32))


def _hgrn_body(q_ref, lf_ref, kk_ref, v_ref, sg_ref, gain_ref, mall_ref, masks_ref, s0_ref,
               o_ref, so_ref, st_ref, ob_ref, qd_ref, kd_ref, dl_ref, vt_ref, ex_ref, a_ref, *, n_chunks, nt):
    t = pl.program_id(1)
    c = HG_CHUNK

    @pl.when(t == 0)
    def _():
        st_ref[...] = s0_ref[0]

    row = lax.broadcasted_iota(jnp.int32, (c, 1), 0)

    n_lv = masks_ref.shape[0] - 1

    def decays(ci):
        rows = slice(ci * c, (ci + 1) * c)
        hi, lo = _split_bf16(lf_ref[rows, :])
        mall = mall_ref[...]
        e_mm = _dot(mall, hi) + _dot(mall, lo)
        b = e_mm[0:c]
        b_last = b[c - 1:c, :]
        lv = 0
        m = c // 2
        while m >= HG_SUBLANE_LEVEL:
            ref = jnp.concatenate(
                [jnp.broadcast_to(b[s + m - 1:s + m, :], (2 * m, GROUP)) for s in range(0, c, 2 * m)], axis=0)
            d = b - ref
            ex_ref[ci % 2, lv] = jnp.exp(jnp.where((row & (2 * m - 1)) >= m, d, -d))
            lv += 1
            m //= 2
        for i in range(1, mall.shape[0] // c):
            ex_ref[ci % 2, lv] = jnp.exp(e_mm[i * c:(i + 1) * c])
            lv += 1
        qd_ref[ci] = (q_ref[rows, :] * jnp.exp(b)).astype(BF16)
        kd_ref[ci] = (kk_ref[rows, :] * jnp.exp(b_last - b)).astype(BF16)
        dl_ref[ci] = jnp.exp(b_last)

    def pair_weights(ci):
        rows = slice(ci * c, (ci + 1) * c)
        for h in range(HG_HEADS):
            sl = slice(HG_D * h, HG_D * (h + 1))
            qh, kh = q_ref[rows, sl], kk_ref[rows, sl]
            a = masks_ref[0] * _dot_nt(qh.astype(BF16), kh.astype(BF16))
            for lv in range(n_lv):
                e = ex_ref[ci % 2, lv, :, sl]
                a = a + masks_ref[lv + 1] * _dot_nt((qh * e).astype(BF16), (kh * e).astype(BF16))
            a_ref[ci % 2, h] = a.astype(BF16)

    def intra_out(ci):
        rows = slice(ci * c, (ci + 1) * c)
        for h in range(HG_HEADS):
            sl = slice(HG_D * h, HG_D * (h + 1))
            vh = v_ref[rows, sl]
            ob_ref[rows, sl] = _dot(a_ref[ci % 2, h], vh.astype(BF16))
            vt_ref[ci, h] = vh.T.astype(BF16)

    def carry(ci):
        rows = slice(ci * c, (ci + 1) * c)
        for h in range(HG_HEADS):
            sl = slice(HG_D * h, HG_D * (h + 1))
            s_t = st_ref[h]
            ob_ref[rows, sl] += _dot_nt(qd_ref[ci, :, sl], s_t.astype(BF16))
            st_ref[h] = dl_ref[ci, :, sl] * s_t + _dot(vt_ref[ci, h], kd_ref[ci, :, sl])

    decays(0)
    pair_weights(0)
    for ci in range(n_chunks):
        if ci + 1 < n_chunks:
            decays(ci + 1)
        intra_out(ci)
        carry(ci)
        if ci + 1 < n_chunks:
            pair_weights(ci + 1)

    for h in range(HG_HEADS):
        sl = slice(HG_D * h, HG_D * (h + 1))
        x = ob_ref[:, sl]
        ms = jnp.mean(x * x, axis=-1, keepdims=True)
        o_ref[:, sl] = (x * lax.rsqrt(ms + RMS_EPS) * gain_ref[...] * sg_ref[:, sl]).astype(BF16)

    @pl.when(t == nt - 1)
    def _():
        for h in range(HG_HEADS):
            so_ref[0, h] = st_ref[h].T


def _hgrn_prompt(p, gain, s0t, *, batch, seq, tc):
    nt = seq // tc
    n = batch * seq
    mall, masks = _hgrn_tables()
    mall = jnp.asarray(mall, BF16)
    masks = jnp.asarray(masks, F32)
    col = lambda g: pl.BlockSpec((tc, GROUP), lambda b, t: (b * nt + t, g))
    st_spec = pl.BlockSpec((1, HG_HEADS, HG_D, HG_D), lambda b, t: (b, 0, 0, 0))
    return pl.pallas_call(
        functools.partial(_hgrn_body, n_chunks=tc // HG_CHUNK, nt=nt),
        grid=(batch, nt),
        in_specs=[col(P_HQ), col(P_LOGF), col(P_KK), col(P_HV), col(P_HG),
                  _const_spec((1, HG_D)), _const_spec(mall.shape), _const_spec(masks.shape), st_spec],
        out_specs=[pl.BlockSpec((tc, GROUP), lambda b, t: (b * nt + t, 0)), st_spec],
        out_shape=[jax.ShapeDtypeStruct((n, GROUP), BF16),
                   jax.ShapeDtypeStruct((batch, HG_HEADS, HG_D, HG_D), F32)],
        scratch_shapes=[pltpu.VMEM((HG_HEADS, HG_D, HG_D), F32), pltpu.VMEM((tc, GROUP), F32),
                        pltpu.VMEM((tc // HG_CHUNK, HG_CHUNK, GROUP), BF16),
                        pltpu.VMEM((tc // HG_CHUNK, HG_CHUNK, GROUP), BF16),
                        pltpu.VMEM((tc // HG_CHUNK, 1, GROUP), F32),
                        pltpu.VMEM((tc // HG_CHUNK, HG_HEADS, HG_D, HG_CHUNK), BF16),
                        pltpu.VMEM((2, masks.shape[0] - 1, HG_CHUNK, GROUP), F32),
                        pltpu.VMEM((2, HG_HEADS, HG_CHUNK, HG_CHUNK), BF16)],
        compiler_params=_params("parallel", "arbitrary"),
        name="hgrn_prompt",
    )(p, p, p, p, p, gain, mall, masks, s0t)


def _ret_tables():
    c = RET_CHUNK
    f32 = np.float32
    lg = np.log1p(-np.exp2(-5.0 - np.arange(RET_HEADS, dtype=f32))).astype(f32)
    t = np.arange(c, dtype=f32)
    rel = t[:, None] - t[None, :]
    intra = np.where(rel[None] >= 0, np.exp(np.maximum(rel, 0.0)[None] * lg[:, None, None]), 0.0).astype(f32)
    lane_lg = np.repeat(lg, RET_D).reshape(RET_HEADS // 2, 1, 128)
    qdec = np.exp((t + 1.0)[None, :, None] * lane_lg).astype(f32)
    kdec = np.exp((c - 1.0 - t)[None, :, None] * lane_lg).astype(f32)
    cdec = np.exp(f32(c) * lane_lg).astype(f32)
    return tuple(jnp.asarray(a) for a in (intra, qdec, kdec, cdec))


def _ret_body(q_ref, k_ref, v_ref, sg_ref, intra_ref, qdec_ref, kdec_ref, cdec_ref, o_ref, so_ref, s_ref, a_ref,
              *, nt):
    t = pl.program_id(1)

    @pl.when(t == 0)
    def _():
        s_ref[...] = jnp.zeros_like(s_ref)

    c = RET_CHUNK
    lane = lax.broadcasted_iota(jnp.int32, (c, 128), 1)
    low = lane < RET_D
    r128 = lax.broadcasted_iota(jnp.int32, (128, 128), 0)
    c128 = lax.broadcasted_iota(jnp.int32, (128, 128), 1)
    same_head = (r128 < RET_D) == (c128 < RET_D)
    for p2 in range(RET_HEADS // 2):
        cols = slice(128 * p2, 128 * p2 + 128)
        qp, kb = q_ref[:, cols], k_ref[:, cols].astype(BF16)
        for hh in range(2):
            qm = jnp.where(low if hh == 0 else ~low, qp, 0.0).astype(BF16)
            a_ref[2 * p2 + hh] = (_dot_nt(qm, kb) * intra_ref[2 * p2 + hh]).astype(BF16)
    for p2 in range(RET_HEADS // 2):
        cols = slice(128 * p2, 128 * p2 + 128)
        qp, kp, vp = q_ref[:, cols], k_ref[:, cols], v_ref[:, cols]
        vb = vp.astype(BF16)
        s_bd = s_ref[p2]
        o = _dot((qp * qdec_ref[p2]).astype(BF16), s_bd.astype(BF16))
        for hh in range(2):
            in_head = low if hh == 0 else ~low
            o = o + jnp.where(in_head, _dot(a_ref[2 * p2 + hh], vb), 0.0)
        upd = _dot((kp * kdec_ref[p2]).T.astype(BF16), vb)
        s_ref[p2] = cdec_ref[p2] * s_bd + jnp.where(same_head, upd, 0.0)
        sq = o * o
        s_lo = jnp.sum(jnp.where(low, sq, 0.0), axis=-1, keepdims=True)
        s_hi = jnp.sum(jnp.where(low, 0.0, sq), axis=-1, keepdims=True)
        ms = jnp.where(low, s_lo, s_hi) * (1.0 / RET_D)
        o_ref[:, cols] = (o * lax.rsqrt(ms + RMS_EPS) * sg_ref[:, cols]).astype(BF16)

    @pl.when(t == nt - 1)
    def _():
        so_ref[0] = s_ref[...]


def _ret_prompt(p, *, batch, seq):
    c = RET_CHUNK
    nt = seq // c
    n = batch * seq
    intra, qdec, kdec, cdec = _ret_tables()
    col = lambda g: pl.BlockSpec((c, GROUP), lambda b, t: (b * nt + t, g))
    npair = RET_HEADS // 2
    return pl.pallas_call(
        functools.partial(_ret_body, nt=nt),
        grid=(batch, nt),
        in_specs=[col(P_RQ), col(P_RK), col(P_RV), col(P_RG),
                  _const_spec(intra.shape), _const_spec(qdec.shape), _const_spec(kdec.shape),
                  _const_spec(cdec.shape)],
        out_specs=[pl.BlockSpec((c, GROUP), lambda b, t: (b * nt + t, 0)),
                   pl.BlockSpec((1, npair, 128, 128), lambda b, t: (b, 0, 0, 0))],
        out_shape=[jax.ShapeDtypeStruct((n, GROUP), BF16),
                   jax.ShapeDtypeStruct((batch, npair, 128, 128), F32)],
        scratch_shapes=[pltpu.VMEM((npair, 128, 128), F32), pltpu.VMEM((RET_HEADS, c, c), BF16)],
        compiler_params=_params("parallel", "arbitrary"),
        name="ret_prompt",
    )(p, p, p, p, intra, qdec, kdec, cdec)


def _unpair_ret_state(s_bd):
    b = s_bd.shape[0]
    lo = s_bd[:, :, :RET_D, :RET_D]
    hi = s_bd[:, :, RET_D:, RET_D:]
    return jnp.stack([lo, hi], axis=2).reshape(b, RET_HEADS, RET_D, RET_D)


SAMPLE_PAGES_PER_STEP = 32


def _pages_feature_major(cache):
    d, n_pool = cache.shape[:2]
    return jnp.transpose(cache, (0, 1, 3, 4, 2)).reshape(d, n_pool, GROUP, PAGE_SIZE)


def _page_specs(layer):
    def spec(r):
        return pl.BlockSpec((None, None, GROUP, PAGE_SIZE),
                            lambda b, g, pt: (layer, pt[b, g * SAMPLE_PAGES_PER_STEP + r], 0, 0))
    return [spec(r) for r in range(SAMPLE_PAGES_PER_STEP)]


def _head_rows(row):
    lane = lax.broadcasted_iota(jnp.int32, (A_HEADS, GROUP), 1)
    hrow = lax.broadcasted_iota(jnp.int32, (A_HEADS, GROUP), 0)
    return jnp.where(lax.shift_right_logical(lane, 6) == hrow, jnp.broadcast_to(row, (A_HEADS, GROUP)), 0.0)


def _moba_scores_body(pt_ref, q_ref, kn_ref, *rest, n_steps, n_blocks):
    pages = rest[:SAMPLE_PAGES_PER_STEP]
    prob_ref, pself_ref, sel_ref, s_ref = rest[SAMPLE_PAGES_PER_STEP:]
    g = pl.program_id(1)
    qm = _head_rows(q_ref[...])
    q_hi, q_lo = _split_bf16(qm)
    q_both = jnp.concatenate([q_hi.astype(F32), q_lo.astype(F32)], axis=0).astype(BF16)
    per_blk = MOBA_BLOCK // PAGE_SIZE
    scale = A_HEAD_DIM ** -0.5
    for r in range(SAMPLE_PAGES_PER_STEP):
        k_hi, k_lo = _split_bf16(pages[r][...])
        s2 = _dot(q_both, k_hi)
        s_ref[g * SAMPLE_PAGES_PER_STEP + r] = s2[0:A_HEADS] + s2[A_HEADS:] + _dot(q_hi, k_lo)

    @pl.when(g == n_steps - 1)
    def _():
        jidx = lax.broadcasted_iota(jnp.int32, (A_HEADS, 128), 1)
        gt = jnp.full((A_HEADS, 128), -jnp.inf, F32)
        for j in range(n_blocks):
            tot = s_ref[per_blk * j]
            for r in range(1, per_blk):
                tot = tot + s_ref[per_blk * j + r]
            gj = jnp.sum(tot, axis=-1, keepdims=True) * (1.0 / MOBA_BLOCK)
            gt = jnp.where(jidx == j, gj, gt)
        rank = jnp.zeros((A_HEADS, 128), F32)
        for jp in range(n_blocks):
            cj = gt[:, jp:jp + 1]
            beats = (cj > gt) | ((cj == gt) & (jidx > jp))
            rank = rank + beats.astype(F32)
        sel = ((rank < MOBA_TOPK) & (jnp.abs(gt) < jnp.inf)).astype(F32)
        sel_ref[0] = sel
        s_self = jnp.sum(qm * scale * kn_ref[...], axis=-1, keepdims=True)
        masked = jnp.float32(-1e30)
        n_pages = n_blocks * per_blk
        m = s_self
        for pg in range(n_pages):
            keep = sel[:, pg // per_blk:pg // per_blk + 1] > 0.0
            m = jnp.maximum(m, jnp.max(jnp.where(keep, s_ref[pg] * scale, masked), axis=-1, keepdims=True))
        e_self = jnp.exp(s_self - m)
        l = e_self
        for pg in range(n_pages):
            keep = sel[:, pg // per_blk:pg // per_blk + 1] > 0.0
            e = jnp.exp(jnp.where(keep, s_ref[pg] * scale, masked) - m)
            prob_ref[0, pg] = e
            l = l + jnp.sum(e, axis=-1, keepdims=True)
        inv = 1.0 / l
        for pg in range(n_pages):
            prob_ref[0, pg] = prob_ref[0, pg] * inv
        pself_ref[0] = jnp.broadcast_to(e_self * inv, (A_HEADS, 128))


def _moba_sample_scores(page_table, p3, cache_k, *, layer):
    db, n_pages = page_table.shape
    n_steps = n_pages // SAMPLE_PAGES_PER_STEP
    n_blocks = n_pages * PAGE_SIZE // MOBA_BLOCK
    row = lambda g_: pl.BlockSpec((None, 1, GROUP), lambda b, g, pt: (b, 0, g_))
    grid_spec = pltpu.PrefetchScalarGridSpec(
        num_scalar_prefetch=1,
        grid=(db, n_steps),
        in_specs=[row(P_Q), row(P_K)] + _page_specs(layer),
        out_specs=[pl.BlockSpec((1, n_pages, A_HEADS, PAGE_SIZE), lambda b, g, pt: (b, 0, 0, 0)),
                   pl.BlockSpec((1, A_HEADS, 128), lambda b, g, pt: (b, 0, 0)),
                   pl.BlockSpec((1, A_HEADS, 128), lambda b, g, pt: (b, 0, 0))],
        scratch_shapes=[pltpu.VMEM((n_pages, A_HEADS, PAGE_SIZE), F32)],
    )
    return pl.pallas_call(
        functools.partial(_moba_scores_body, n_steps=n_steps, n_blocks=n_blocks),
        grid_spec=grid_spec,
        out_shape=[jax.ShapeDtypeStruct((db, n_pages, A_HEADS, PAGE_SIZE), F32),
                   jax.ShapeDtypeStruct((db, A_HEADS, 128), F32),
                   jax.ShapeDtypeStruct((db, A_HEADS, 128), F32)],
        compiler_params=_params("parallel", "arbitrary"),
        name="moba_sample_scores",
    )(page_table, p3, p3, *([cache_k] * SAMPLE_PAGES_PER_STEP))


def _selected_pages(page_table, sel):
    db, n_pages = page_table.shape
    per_blk = MOBA_BLOCK // PAGE_SIZE
    n_steps = n_pages // SAMPLE_PAGES_PER_STEP
    used = jnp.repeat(jnp.max(sel, axis=1)[:, :n_pages // per_blk] > 0.0, per_blk, axis=1)
    pages = page_table.reshape(db * n_steps, SAMPLE_PAGES_PER_STEP)
    used = used.reshape(db * n_steps, SAMPLE_PAGES_PER_STEP)
    order = jnp.arange(db * n_steps, dtype=jnp.int32)[:, None]
    last_used = lax.cummax(jnp.where(used, order, 0), axis=0)
    return jnp.take_along_axis(pages, last_used, axis=0).reshape(db, n_pages)


def _moba_out_body(pt_ref, prob_ref, pself_ref, vn_ref, *rest, n_steps):
    pages = rest[:SAMPLE_PAGES_PER_STEP]
    o_ref, acc_ref = rest[SAMPLE_PAGES_PER_STEP:]
    g = pl.program_id(1)

    @pl.when(g == 0)
    def _():
        acc_ref[...] = jnp.zeros_like(acc_ref)

    acc = acc_ref[...]
    for r in range(SAMPLE_PAGES_PER_STEP):
        pr = prob_ref[0, g * SAMPLE_PAGES_PER_STEP + r]
        acc = acc + _dot_nt(pr.astype(BF16), pages[r][...].astype(BF16))
    acc_ref[...] = acc

    @pl.when(g == n_steps - 1)
    def _():
        full = acc_ref[...] + pself_ref[0][:, 0:1] * vn_ref[...]
        lane = lax.broadcasted_iota(jnp.int32, (A_HEADS, GROUP), 1)
        hrow = lax.broadcasted_iota(jnp.int32, (A_HEADS, GROUP), 0)
        diag = jnp.where(lax.shift_right_logical(lane, 6) == hrow, full, 0.0)
        o_ref[...] = jnp.sum(diag, axis=0, keepdims=True)


def _moba_sample_out(page_table, probs, p_self, p3, cache_v, *, layer):
    db, n_pages = page_table.shape
    n_steps = n_pages // SAMPLE_PAGES_PER_STEP
    grid_spec = pltpu.PrefetchScalarGridSpec(
        num_scalar_prefetch=1,
        grid=(db, n_steps),
        in_specs=[pl.BlockSpec((1, n_pages, A_HEADS, PAGE_SIZE), lambda b, g, pt: (b, 0, 0, 0)),
                  pl.BlockSpec((1, A_HEADS, 128), lambda b, g, pt: (b, 0, 0)),
                  pl.BlockSpec((None, 1, GROUP), lambda b, g, pt: (b, 0, P_V))] + _page_specs(layer),
        out_specs=pl.BlockSpec((None, 1, GROUP), lambda b, g, pt: (b, 0, 0)),
        scratch_shapes=[pltpu.VMEM((A_HEADS, GROUP), F32)],
    )
    return pl.pallas_call(
        functools.partial(_moba_out_body, n_steps=n_steps),
        grid_spec=grid_spec,
        out_shape=jax.ShapeDtypeStruct((db, 1, GROUP), F32),
        compiler_params=_params("parallel", "arbitrary"),
        name="moba_sample_out",
    )(page_table, probs, p_self, p3, *([cache_v] * SAMPLE_PAGES_PER_STEP)).reshape(db, GROUP)


def _col_bcast(row):
    return jnp.broadcast_to(row, (128, 128)).T


STEP_SEQS = 8


def _hgrn_step_body(q_ref, lf_ref, kk_ref, v_ref, sg_ref, gain_ref, s_ref, o_ref, so_ref):
    for s in range(q_ref.shape[0]):
        for h in range(HG_HEADS):
            sl = slice(HG_D * h, HG_D * (h + 1))
            f_col = _col_bcast(jnp.exp(lf_ref[s, :, sl]))
            k_col = _col_bcast(kk_ref[s, :, sl])
            q_col = _col_bcast(q_ref[s, :, sl])
            s_new = f_col * s_ref[s, h] + k_col * v_ref[s, :, sl]
            so_ref[s, h] = s_new
            o = jnp.sum(q_col * s_new, axis=0, keepdims=True)
            ms = jnp.mean(o * o, axis=-1, keepdims=True)
            o_ref[s, :, sl] = o * lax.rsqrt(ms + RMS_EPS) * gain_ref[...] * sg_ref[s, :, sl]


def _hgrn_step(p3, gain, state, *, layer):
    db = p3.shape[0]
    ns = math.gcd(db, STEP_SEQS)
    col = lambda g_: pl.BlockSpec((ns, 1, GROUP), lambda b: (b, 0, g_))
    st_shape = (ns, HG_HEADS, HG_D, HG_D)
    o, s = pl.pallas_call(
        _hgrn_step_body,
        grid=(db // ns,),
        in_specs=[col(P_HQ), col(P_LOGF), col(P_KK), col(P_HV), col(P_HG), _const_spec((1, HG_D)),
                  pl.BlockSpec((None,) + st_shape, lambda b: (layer, b, 0, 0, 0))],
        out_specs=[pl.BlockSpec((ns, 1, GROUP), lambda b: (b, 0, 0)),
                   pl.BlockSpec(st_shape, lambda b: (b, 0, 0, 0))],
        out_shape=[jax.ShapeDtypeStruct((db, 1, GROUP), F32),
                   jax.ShapeDtypeStruct((db, HG_HEADS, HG_D, HG_D), F32)],
        compiler_params=_params("parallel"),
        name="hgrn_step",
    )(p3, p3, p3, p3, p3, gain, state)
    return o.reshape(db, GROUP), s


def _ret_step_body(q_ref, k_ref, v_ref, sg_ref, s_ref, o_ref, so_ref):
    gammas = np.exp(np.log1p(-np.exp2(-5.0 - np.arange(RET_HEADS, dtype=np.float32))).astype(np.float32))
    for s in range(q_ref.shape[0]):
        outs = []
        for p2 in range(RET_HEADS // 2):
            cols = slice(128 * p2, 128 * p2 + 128)
            k_col = _col_bcast(k_ref[s, :, cols])
            q_col = _col_bcast(q_ref[s, :, cols])
            vrow = v_ref[s, :, cols]
            for hh in range(2):
                h = 2 * p2 + hh
                rs = slice(RET_D * hh, RET_D * (hh + 1))
                s_new = float(gammas[h]) * s_ref[s, h] + k_col[rs, 0:RET_D] * vrow[:, rs]
                so_ref[s, h] = s_new
                o = jnp.sum(q_col[rs, 0:RET_D] * s_new, axis=0, keepdims=True)
                ms = jnp.mean(o * o, axis=-1, keepdims=True)
                outs.append(o * lax.rsqrt(ms + RMS_EPS))
        o_ref[s] = jnp.concatenate(outs, axis=1) * sg_ref[s]


def _ret_step(p3, state, *, layer):
    db = p3.shape[0]
    ns = math.gcd(db, STEP_SEQS)
    col = lambda g_: pl.BlockSpec((ns, 1, GROUP), lambda b: (b, 0, g_))
    st_shape = (ns, RET_HEADS, RET_D, RET_D)
    o, s = pl.pallas_call(
        _ret_step_body,
        grid=(db // ns,),
        in_specs=[col(P_RQ), col(P_RK), col(P_RV), col(P_RG),
                  pl.BlockSpec((None,) + st_shape, lambda b: (layer, b, 0, 0, 0))],
        out_specs=[pl.BlockSpec((ns, 1, GROUP), lambda b: (b, 0, 0)),
                   pl.BlockSpec(st_shape, lambda b: (b, 0, 0, 0))],
        out_shape=[jax.ShapeDtypeStruct((db, 1, GROUP), F32),
                   jax.ShapeDtypeStruct((db, RET_HEADS, RET_D, RET_D), F32)],
        compiler_params=_params("parallel"),
        name="ret_step",
    )(p3, p3, p3, p3, state)
    return o.reshape(db, GROUP), s


def _rotary_tables(pos):
    f32 = np.float32
    theta = (f32(1.0) / (f32(RET_ROPE_BASE) ** np.linspace(0.0, 1.0, RET_D // 2, dtype=f32))).astype(f32)
    ang = (pos.astype(f32)[:, None] * np.repeat(theta, 2)[None, :]).astype(f32)
    ang = np.tile(ang, (1, 128 // RET_D))
    cos, sin = np.cos(ang).astype(f32), np.sin(ang).astype(f32)
    even = (np.arange(128) % 2 == 0)[None, :]
    return tuple(jnp.asarray(a) for a in (cos, np.where(even, -sin, f32(0)), np.where(even, f32(0), sin)))


def _head_block_diag():
    r = np.arange(GROUP)
    return jnp.asarray((r[:, None] // A_HEAD_DIM) == (r[None, :] // A_HEAD_DIM), BF16)


def _layer_weights(l, lb_all, ffn1_norm, ffn1_w_in, ffn1_w_out, mix_norm, w_in, moba_q_gain, moba_k_gain,
                   hgrn_o_gain, w_branch_a, w_branch_b, w_branch_c, w_out, ffn2_norm, ffn2_w_in, ffn2_w_out):
    row = lambda a: a.reshape(1, -1).astype(F32)
    return dict(
        f1g=row(ffn1_norm[l]), f1a=ffn1_w_in[l].astype(BF16), f1b=ffn1_w_out[l].astype(BF16),
        mg=row(mix_norm[l]), w_in=w_in[l].astype(BF16),
        qg=row(jnp.tile(moba_q_gain[l], A_HEADS)), kg=row(jnp.tile(moba_k_gain[l], A_HEADS)),
        lb=row(lb_all[l]), hg=row(hgrn_o_gain[l]),
        wa=w_branch_a[l].astype(BF16), wb=w_branch_b[l].astype(BF16), wc=w_branch_c[l].astype(BF16),
        wo=w_out[l].astype(BF16),
        f2g=row(ffn2_norm[l]), f2a=ffn2_w_in[l].astype(BF16), f2b=ffn2_w_out[l].astype(BF16),
    )


def _prompt_layer(x, w, tabs, bd, *, batch, seq):
    x = _ffn(x, w["f1g"], w["f1a"], w["f1b"], tm=512)
    p, k_t, v_t = _proj(x, w["mg"], w["w_in"], w["qg"], w["kg"], w["lb"], *tabs, bd, tm=256, kv_t_batch=batch)
    oa = _moba_prompt(p, batch=batch, seq=seq)
    s0t = jnp.zeros((batch, HG_HEADS, HG_D, HG_D), F32)
    ob, s_hg = _hgrn_prompt(p, w["hg"], s0t, batch=batch, seq=seq, tc=512)
    oc, s_ret = _ret_prompt(p, batch=batch, seq=seq)
    x = _merge(oa, ob, oc, p, x, w["wa"], w["wb"], w["wc"], w["wo"], tm=512)
    x = _ffn(x, w["f2g"], w["f2a"], w["f2b"], tm=512)
    return x, k_t, v_t, s_hg, _unpair_ret_state(s_ret)


def _token_major(kv_t):
    d, b, _, t = kv_t.shape
    return jnp.transpose(kv_t.reshape(d, b, A_HEADS, A_HEAD_DIM, t), (0, 1, 4, 2, 3))


def _sample_layer(x, w, tabs, bd, l, page_table, cache_k, cache_v, state_hgrn, state_ret):
    db = x.shape[0]
    x = _ffn(x, w["f1g"], w["f1a"], w["f1b"], tm=db)
    (p,) = _proj(x, w["mg"], w["w_in"], w["qg"], w["kg"], w["lb"], *tabs, bd, tm=db)
    p3 = p.reshape(db, 1, P_GROUPS * GROUP)
    probs, p_self, sel = _moba_sample_scores(page_table, p3, cache_k, layer=l)
    oa = _moba_sample_out(_selected_pages(page_table, sel), probs, p_self, p3, cache_v, layer=l)
    ob, s_hg = _hgrn_step(p3, w["hg"], state_hgrn, layer=l)
    oc, s_ret = _ret_step(p3, state_ret, layer=l)
    x = _merge(oa, ob, oc, p, x, w["wa"], w["wb"], w["wc"], w["wo"], tm=db)
    x = _ffn(x, w["f2g"], w["f2a"], w["f2b"], tm=db)
    k_new = p[:, P_K * GROUP:(P_K + 1) * GROUP].reshape(db, 1, A_HEADS, A_HEAD_DIM)
    v_new = p[:, P_V * GROUP:(P_V + 1) * GROUP].reshape(db, 1, A_HEADS, A_HEAD_DIM)
    return x, k_new, v_new, s_hg, s_ret


def kernel(x_prompt, x_sample, cache_k, cache_v, state_hgrn, state_ret, page_table, ffn1_norm, ffn1_w_in, ffn1_w_out, mix_norm, w_in, moba_q_gain, moba_k_gain, hgrn_lb, hgrn_o_gain, w_branch_a, w_branch_b, w_branch_c, w_out, ffn2_norm, ffn2_w_in, ffn2_w_out):
    lb_cum = jnp.cumsum(jax.nn.softmax(hgrn_lb.astype(F32), axis=0), axis=0)
    lb_all = lb_cum - lb_cum[:1]
    bp, seq, _ = x_prompt.shape
    db, dec_seq, _ = x_sample.shape
    assert dec_seq == 1
    n_pages = page_table.shape[1]
    past_len = n_pages * PAGE_SIZE
    assert past_len % MOBA_BLOCK == 0 and n_pages % SAMPLE_PAGES_PER_STEP == 0
    tabs_p = _rotary_tables(np.arange(seq))
    tabs_s = _rotary_tables(np.full((db,), past_len))
    bd = _head_block_diag()
    ck = _pages_feature_major(cache_k)
    cv = _pages_feature_major(cache_v)
    xp = x_prompt.reshape(bp * seq, D_MODEL)
    xs = x_sample.reshape(db, D_MODEL)
    cols = [[] for _ in range(8)]
    for l in range(DEPTH):
        w = _layer_weights(l, lb_all, ffn1_norm, ffn1_w_in, ffn1_w_out, mix_norm, w_in, moba_q_gain, moba_k_gain, hgrn_o_gain, w_branch_a, w_branch_b, w_branch_c, w_out, ffn2_norm, ffn2_w_in, ffn2_w_out)
        xp, kp, vp, hp, rp = _prompt_layer(xp, w, tabs_p, bd, batch=bp, seq=seq)
        xs, ks, vs, hs, rs = _sample_layer(xs, w, tabs_s, bd, l, page_table, ck, cv, state_hgrn, state_ret)
        for c, a in zip(cols, (kp, vp, ks, vs, hp, hs, rp, rs)):
            c.append(a)
    outs = [jnp.stack(c) for c in cols]
    outs[0], outs[1] = _token_major(outs[0]), _token_major(outs[1])
    return (xp.reshape(bp, seq, D_MODEL), xs.reshape(db, 1, D_MODEL)) + tuple(outs)
```

```python
import functools
import math

import numpy as np
import jax
import jax.numpy as jnp
from jax import lax
from jax.experimental import pallas as pl
from jax.experimental.pallas import tpu as pltpu

F32 = jnp.float32
BF16 = jnp.bfloat16

D_MODEL = 1024
D_FF = 2816
DEPTH = 4
RMS_EPS = 1e-6
A_HEADS = 8
A_HEAD_DIM = 64
MOBA_BLOCK = 256
MOBA_TOPK = 3
MOBA_ONES_ROWS = 16
HG_HEADS = 4
HG_D = 128
RET_HEADS = 8
RET_D = 64
RET_ROPE_BASE = 10000.0
PAGE_SIZE = 128
GROUP = 512

P_Q, P_K, P_V, P_HQ, P_LOGF, P_KK, P_HG, P_RQ, P_RK, P_RG = range(10)
P_GROUPS = 10
H_HV, H_RV, H_GATES = range(3)
H_GROUPS = 8

HG_CHUNK = 64
HG_SUBLANE_LEVEL = 8
RET_CHUNK = 256

VMEM_LIMIT = 56 * 1024 * 1024


def _dot(a, b):
    return jnp.dot(a, b, preferred_element_type=F32)


def _dot_nt(a, b):
    return lax.dot_general(a, b, (((1,), (1,)), ((), ())), preferred_element_type=F32)


def _split_bf16(a):
    hi = a.astype(BF16)
    lo = (a - hi.astype(F32)).astype(BF16)
    return hi, lo


def _params(*sem):
    return pltpu.CompilerParams(dimension_semantics=sem, vmem_limit_bytes=VMEM_LIMIT)


def _const_spec(shape):
    nd = len(shape)
    return pl.BlockSpec(shape, lambda *_: (0,) * nd)


def _ffn_body(x_ref, g_ref, w1_ref, w2_ref, o_ref, acc_ref, *, ff_chunk):
    x = x_ref[...]
    ms = jnp.mean(x * x, axis=-1, keepdims=True)
    h = (x * lax.rsqrt(ms + RMS_EPS) * g_ref[...]).astype(BF16)
    for c in range(D_FF // ff_chunk):
        lo = c * ff_chunk
        gate = _dot(h, w1_ref[:, lo:lo + ff_chunk])
        up = _dot(h, w1_ref[:, D_FF + lo:D_FF + lo + ff_chunk])
        act = (gate * jax.nn.sigmoid(gate) * up).astype(BF16)
        part = _dot(act, w2_ref[lo:lo + ff_chunk, :])
        if c == 0:
            acc_ref[...] = part
        else:
            acc_ref[...] += part
    o_ref[...] = x + 0.5 * acc_ref[...]


def _ffn(x, gain, w1, w2, *, tm):
    n = x.shape[0]
    return pl.pallas_call(
        functools.partial(_ffn_body, ff_chunk=256),
        grid=(n // tm,),
        in_specs=[
            pl.BlockSpec((tm, D_MODEL), lambda i: (i, 0)),
            _const_spec((1, D_MODEL)),
            pl.BlockSpec((D_MODEL, 2 * D_FF), lambda i: (0, 0), pipeline_mode=pl.Buffered(1)),
            pl.BlockSpec((D_FF, D_MODEL), lambda i: (0, 0), pipeline_mode=pl.Buffered(1)),
        ],
        out_specs=pl.BlockSpec((tm, D_MODEL), lambda i: (i, 0)),
        out_shape=jax.ShapeDtypeStruct((n, D_MODEL), F32),
        scratch_shapes=[pltpu.VMEM((tm, D_MODEL), F32)],
        compiler_params=_params("parallel"),
        name="ffn",
    )(x, gain, w1, w2)


def _proj_body(x_ref, g_ref, w_ref, qg_ref, kg_ref, lb_ref, cos_ref, sa_ref, sb_ref, bd_ref, *rest,
               n_passthrough):
    o_ref, o16_ref, *kv_t_refs = rest[n_passthrough:]
    x = x_ref[...]
    ms = jnp.mean(x * x, axis=-1, keepdims=True)
    h = (x * lax.rsqrt(ms + RMS_EPS) * g_ref[...]).astype(BF16)

    def mm(wg):
        return _dot(h, w_ref[:, wg * GROUP:(wg + 1) * GROUP])

    def put(g, val):
        o_ref[:, g * GROUP:(g + 1) * GROUP] = val

    def put16(g, val):
        o16_ref[:, g * GROUP:(g + 1) * GROUP] = val.astype(BF16)

    def headnorm(a, gain_row):
        ss = _dot((a * a).astype(BF16), bd_ref[...])
        return a * lax.rsqrt(ss * (1.0 / A_HEAD_DIM) + RMS_EPS) * gain_row

    def rotary(g, a, scale):
        cos, sa, sb = cos_ref[...], sa_ref[...], sb_ref[...]
        for s in range(GROUP // 128):
            xs = a[:, 128 * s:128 * s + 128]
            nxt = pltpu.roll(xs, 127, 1)
            prv = pltpu.roll(xs, 1, 1)
            r = xs * cos + nxt * sa + prv * sb
            o_ref[:, g * GROUP + 128 * s:g * GROUP + 128 * s + 128] = r * scale if scale != 1.0 else r

    put(P_Q, headnorm(mm(0), qg_ref[...]))
    k_n = headnorm(mm(1), kg_ref[...])
    v = mm(2)
    put(P_K, k_n)
    put(P_V, v)
    if kv_t_refs:
        kv_t_refs[0][...] = k_n.T
        kv_t_refs[1][...] = v.T
    put(P_HQ, mm(3))

    fr = mm(4)
    lb = lb_ref[...]
    log_sig = jnp.minimum(fr, 0.0) - jnp.log1p(jnp.exp(-jnp.abs(fr)))
    a = jnp.log(lb)
    c = jnp.log1p(-lb) + log_sig
    put(P_LOGF, jnp.maximum(a, c) + jnp.log1p(jnp.exp(-jnp.abs(a - c))))
    put(P_KK, (1.0 - lb) * jax.nn.sigmoid(-fr))

    put16(H_HV, mm(5))
    hg = mm(6)
    put(P_HG, hg * jax.nn.sigmoid(hg))
    rotary(P_RQ, mm(7), 1.0)
    rotary(P_RK, mm(8), RET_D ** -0.5)
    put16(H_RV, mm(9))
    rg = mm(10)
    put(P_RG, rg * jax.nn.sigmoid(rg))
    for t in range(H_GROUPS - H_GATES):
        put16(H_GATES + t, jax.nn.sigmoid(mm(11 + t)))


def _proj(x, gain, w, qg, kg, lb, cos, sa, sb, bd, *, tm, kv_t=None):
    n = x.shape[0]
    n_tab = cos.shape[0] // tm
    tab = pl.BlockSpec((tm, 128), lambda i: (i % n_tab, 0))
    in_specs = [
        pl.BlockSpec((tm, D_MODEL), lambda i: (i, 0)),
        _const_spec((1, D_MODEL)),
        pl.BlockSpec(w.shape, lambda i: (0, 0), pipeline_mode=pl.Buffered(1)),
        _const_spec((1, GROUP)),
        _const_spec((1, GROUP)),
        _const_spec((1, GROUP)),
        tab, tab, tab,
        _const_spec((GROUP, GROUP)),
    ]
    args = [x, gain, w, qg, kg, lb, cos, sa, sb, bd]
    out_specs = [pl.BlockSpec((tm, P_GROUPS * GROUP), lambda i: (i, 0)),
                 pl.BlockSpec((tm, H_GROUPS * GROUP), lambda i: (i, 0))]
    out_shape = [jax.ShapeDtypeStruct((n, P_GROUPS * GROUP), F32),
                 jax.ShapeDtypeStruct((n, H_GROUPS * GROUP), BF16)]
    aliases = {}
    if kv_t is not None:
        k_all, v_all, layer = kv_t
        nt = k_all.shape[3] // tm
        for a in (k_all, v_all):
            aliases[len(args)] = len(out_shape)
            in_specs.append(pl.BlockSpec(memory_space=pl.ANY))
            args.append(a)
            out_specs.append(pl.BlockSpec((None, None, GROUP, tm), lambda i: (layer, i // nt, 0, i % nt)))
            out_shape.append(jax.ShapeDtypeStruct(a.shape, a.dtype))
    return pl.pallas_call(
        functools.partial(_proj_body, n_passthrough=len(aliases)),
        grid=(n // tm,),
        in_specs=in_specs,
        out_specs=out_specs,
        out_shape=out_shape,
        input_output_aliases=aliases,
        compiler_params=_params("parallel"),
        name="proj",
    )(*args)


def _merge_body(oa_ref, ob_ref, oc_ref, ga_ref, gb_ref, gc_ref, x_ref, wa_ref, wb_ref, wc_ref, wo_ref, o_ref):
    m = ga_ref[...] * _dot(oa_ref[...].astype(BF16), wa_ref[...])
    m = m + gb_ref[...] * _dot(ob_ref[...].astype(BF16), wb_ref[...])
    m = m + gc_ref[...] * _dot(oc_ref[...].astype(BF16), wc_ref[...])
    o_ref[...] = x_ref[...] + _dot(m.astype(BF16), wo_ref[...])


def _merge(oa, ob, oc, hh, x, wa, wb, wc, wo, *, tm):
    n = x.shape[0]
    row = lambda i: (i, 0)
    g0 = H_GATES // 2
    return pl.pallas_call(
        _merge_body,
        grid=(n // tm,),
        in_specs=[
            pl.BlockSpec((tm, GROUP), row),
            pl.BlockSpec((tm, GROUP), row),
            pl.BlockSpec((tm, GROUP), row),
            pl.BlockSpec((tm, D_MODEL), lambda i: (i, g0)),
            pl.BlockSpec((tm, D_MODEL), lambda i: (i, g0 + 1)),
            pl.BlockSpec((tm, D_MODEL), lambda i: (i, g0 + 2)),
            pl.BlockSpec((tm, D_MODEL), row),
            _const_spec((GROUP, D_MODEL)),
            _const_spec((GROUP, D_MODEL)),
            _const_spec((GROUP, D_MODEL)),
            _const_spec((D_MODEL, D_MODEL)),
        ],
        out_specs=pl.BlockSpec((tm, D_MODEL), row),
        out_shape=jax.ShapeDtypeStruct((n, D_MODEL), F32),
        compiler_params=_params("parallel"),
        name="merge",
    )(oa, ob, oc, hh, hh, hh, x, wa, wb, wc, wo)


def _moba_body(q_ref, k_ref, v_ref, o_ref, k16, vt16, kmt, bias_ref, acc_ref, qmt_ref, m_ref, s_ref, *, nb):
    i = pl.program_id(1)
    blk = MOBA_BLOCK
    dh = A_HEAD_DIM
    masked = jnp.float32(-1e30)

    @pl.when(i == 0)
    def _():
        lane = lax.broadcasted_iota(jnp.int32, (A_HEADS, GROUP), 1)
        hrow = lax.broadcasted_iota(jnp.int32, (A_HEADS, GROUP), 0)
        headmask = (lax.shift_right_logical(lane, 6) == hrow).astype(F32)
        lane128 = lax.broadcasted_iota(jnp.int32, (blk, 128), 1)
        ones = jnp.ones((MOBA_ONES_ROWS, blk), BF16)

        def init(jj, c):
            r = pl.multiple_of(jj * blk, blk)
            kb = k_ref[pl.ds(r, blk), :]
            km = jnp.sum(kb, axis=0, keepdims=True) * (1.0 / blk)
            kmt[pl.ds(pl.multiple_of(jj * A_HEADS, A_HEADS), A_HEADS), :] = km * headmask
            indicator = jnp.where(lane128 == dh + jj, 1.0, 0.0)
            vt = v_ref[pl.ds(r, blk), :].T
            for h in range(A_HEADS):
                slab = kb[:, 128 * (h // 2):128 * (h // 2) + 128]
                if h % 2:
                    slab = pltpu.roll(slab, dh, 1)
                k16[jj, h] = jnp.where(lane128 < dh, slab, indicator).astype(BF16)
                vt16[jj, h, 0:dh, :] = vt[dh * h:dh * (h + 1), :].astype(BF16)
                vt16[jj, h, dh:dh + MOBA_ONES_ROWS, :] = ones
            return c

        lax.fori_loop(0, nb, init, 0)

    q = q_ref[...]
    kh, kl = _split_bf16(kmt[...])
    qh, ql = _split_bf16(q)
    gt = _dot_nt(kh, qh) + _dot_nt(kh, ql) + _dot_nt(kl, qh)
    neg_inf = jnp.float32(-jnp.inf)
    g = [jnp.where(j < i, gt[A_HEADS * j:A_HEADS * (j + 1), :], neg_inf) for j in range(nb)]
    for j in range(nb):
        rank = jnp.zeros((A_HEADS, blk), F32)
        for jp in range(nb):
            if jp == j:
                continue
            beats = (g[jp] >= g[j]) if jp < j else (g[jp] > g[j])
            rank = rank + beats.astype(F32)
        sel = (rank < MOBA_TOPK) & (jnp.abs(g[j]) < jnp.inf)
        bias = jnp.where(sel | (j == i), 0.0, masked)
        for h in range(A_HEADS):
            bias_ref[h, j:j + 1, :] = bias[h:h + 1, :]

    qt = (q * (dh ** -0.5 * math.log2(math.e))).T
    pad = jnp.zeros((128 - dh - nb, blk), F32)
    for h in range(A_HEADS):
        qmt_ref[h] = jnp.concatenate([qt[dh * h:dh * (h + 1), :], bias_ref[h], pad], axis=0).astype(BF16)

    m_ref[...] = jnp.full(m_ref.shape, masked, F32)
    acc_ref[...] = jnp.zeros_like(acc_ref)

    def block_steps(js, causal=None):
        for t, j in enumerate(js):
            for h in range(A_HEADS):
                s_ref[t, h] = _dot(k16[j, h], qmt_ref[h])
        for t, j in enumerate(js):
            for h in range(A_HEADS):
                st = s_ref[t, h]
                if causal is not None:
                    st = jnp.where(causal, st, masked)
                m_old = m_ref[h:h + 1, :]
                m_new = jnp.maximum(m_old, jnp.max(st, axis=0, keepdims=True))
                alpha = jnp.exp2(m_old - m_new)
                pt = jnp.exp2(st - m_new).astype(BF16)
                m_ref[h:h + 1, :] = m_new
                acc_ref[h] = alpha * acc_ref[h] + _dot(vt16[j, h], pt)

    kidx = lax.broadcasted_iota(jnp.int32, (blk, blk), 0)
    qidx = lax.broadcasted_iota(jnp.int32, (blk, blk), 1)
    block_steps([i], kidx <= qidx)

    @pl.when(i % 2 == 1)
    def _():
        block_steps([0])

    def body(t, c):
        j = i % 2 + 2 * t
        block_steps([j, j + 1])
        return c

    lax.fori_loop(0, i // 2, body, 0)

    per = blk // dh
    for h in range(A_HEADS):
        s_ref[0, h // per, dh * (h % per):dh * (h % per + 1), :] = acc_ref[h, 0:dh, :] / acc_ref[h, dh:dh + 1, :]
    o_ref[...] = jnp.concatenate([s_ref[0, t] for t in range(A_HEADS // per)], axis=0).T.astype(BF16)


def _moba_prompt(p, *, batch, seq):
    nb = seq // MOBA_BLOCK
    n = batch * seq
    return pl.pallas_call(
        functools.partial(_moba_body, nb=nb),
        grid=(batch, nb),
        in_specs=[
            pl.BlockSpec((MOBA_BLOCK, GROUP), lambda b, i: (b * nb + i, P_Q)),
            pl.BlockSpec((seq, GROUP), lambda b, i: (b, P_K), pipeline_mode=pl.Buffered(1)),
            pl.BlockSpec((seq, GROUP), lambda b, i: (b, P_V), pipeline_mode=pl.Buffered(1)),
        ],
        out_specs=pl.BlockSpec((MOBA_BLOCK, GROUP), lambda b, i: (b * nb + i, 0)),
        out_shape=jax.ShapeDtypeStruct((n, GROUP), BF16),
        scratch_shapes=[
            pltpu.VMEM((nb, A_HEADS, MOBA_BLOCK, 128), BF16),
            pltpu.VMEM((nb, A_HEADS, A_HEAD_DIM + MOBA_ONES_ROWS, MOBA_BLOCK), BF16),
            pltpu.VMEM((nb * A_HEADS, GROUP), F32),
            pltpu.VMEM((A_HEADS, nb, MOBA_BLOCK), F32),
            pltpu.VMEM((A_HEADS, A_HEAD_DIM + MOBA_ONES_ROWS, MOBA_BLOCK), F32),
            pltpu.VMEM((A_HEADS, 128, MOBA_BLOCK), BF16),
            pltpu.VMEM((A_HEADS, MOBA_BLOCK), F32),
            pltpu.VMEM((2, A_HEADS, MOBA_BLOCK, MOBA_BLOCK), F32),
        ],
        compiler_params=_params("parallel", "arbitrary"),
        name="moba_prompt",
    )(p, p, p)


def _hgrn_tables():
    c = HG_CHUNK
    t = np.arange(c)[:, None]
    u = np.arange(c)[None, :]
    mats = [(u <= t)]
    masks = [np.eye(c, dtype=bool)]
    m = c // 2
    while m >= 1:
        b0 = (t // (2 * m)) * (2 * m)
        right = t >= b0 + m
        if m < HG_SUBLANE_LEVEL:
            mats.append(np.where(right, (u >= b0 + m) & (u <= t), (u > t) & (u <= b0 + m - 1)))
        s = u
        masks.append(((s // (2 * m)) == (t // (2 * m))) & right & (s < b0 + m))
        m //= 2
    return (np.concatenate(mats, 0).astype(np.float32), np.stack(masks).astype(np.float32))


def _hgrn_body(q_ref, lf_ref, kk_ref, v_ref, sg_ref, gain_ref, mall_ref, masks_ref, s0_ref,
               o_ref, so_ref, st_ref, ob_ref, qd_ref, kd_ref, dl_ref, vt_ref, ex_ref, a_ref, *, n_chunks, nt):
    t = pl.program_id(1)
    c = HG_CHUNK

    @pl.when(t == 0)
    def _():
        st_ref[...] = s0_ref[0]

    row = lax.broadcasted_iota(jnp.int32, (c, 1), 0)

    n_lv = masks_ref.shape[0] - 1

    def decays(ci):
        rows = slice(ci * c, (ci + 1) * c)
        hi, lo = _split_bf16(lf_ref[rows, :])
        mall = mall_ref[...]
        e_mm = _dot(mall, hi) + _dot(mall, lo)
        b = e_mm[0:c]
        b_last = b[c - 1:c, :]
        lv = 0
        m = c // 2
        while m >= HG_SUBLANE_LEVEL:
            ref = jnp.concatenate(
                [jnp.broadcast_to(b[s + m - 1:s + m, :], (2 * m, GROUP)) for s in range(0, c, 2 * m)], axis=0)
            d = b - ref
            ex_ref[ci % 2, lv] = jnp.exp(jnp.where((row & (2 * m - 1)) >= m, d, -d))
            lv += 1
            m //= 2
        for i in range(1, mall.shape[0] // c):
            ex_ref[ci % 2, lv] = jnp.exp(e_mm[i * c:(i + 1) * c])
            lv += 1
        qd_ref[ci] = (q_ref[rows, :] * jnp.exp(b)).astype(BF16)
        kd_ref[ci] = (kk_ref[rows, :] * jnp.exp(b_last - b)).astype(BF16)
        dl_ref[ci] = jnp.exp(b_last)

    def pair_weights(ci):
        rows = slice(ci * c, (ci + 1) * c)
        for h in range(HG_HEADS):
            sl = slice(HG_D * h, HG_D * (h + 1))
            qh, kh = q_ref[rows, sl], kk_ref[rows, sl]
            a = masks_ref[0] * _dot_nt(qh.astype(BF16), kh.astype(BF16))
            for lv in range(n_lv):
                e = ex_ref[ci % 2, lv, :, sl]
                a = a + masks_ref[lv + 1] * _dot_nt((qh * e).astype(BF16), (kh * e).astype(BF16))
            a_ref[ci % 2, h] = a.astype(BF16)

    def intra_out(ci):
        rows = slice(ci * c, (ci + 1) * c)
        for h in range(HG_HEADS):
            sl = slice(HG_D * h, HG_D * (h + 1))
            vh = v_ref[rows, sl]
            ob_ref[rows, sl] = _dot(a_ref[ci % 2, h], vh.astype(BF16))
            vt_ref[ci, h] = vh.astype(F32).T.astype(BF16)

    def carry(ci):
        rows = slice(ci * c, (ci + 1) * c)
        for h in range(HG_HEADS):
            sl = slice(HG_D * h, HG_D * (h + 1))
            s_t = st_ref[h]
            ob_ref[rows, sl] += _dot_nt(qd_ref[ci, :, sl], s_t.astype(BF16))
            st_ref[h] = dl_ref[ci, :, sl] * s_t + _dot(vt_ref[ci, h], kd_ref[ci, :, sl])

    decays(0)
    pair_weights(0)
    for ci in range(n_chunks):
        if ci + 1 < n_chunks:
            decays(ci + 1)
        intra_out(ci)
        carry(ci)
        if ci + 1 < n_chunks:
            pair_weights(ci + 1)

    for h in range(HG_HEADS):
        sl = slice(HG_D * h, HG_D * (h + 1))
        x = ob_ref[:, sl]
        ms = jnp.mean(x * x, axis=-1, keepdims=True)
        o_ref[:, sl] = (x * lax.rsqrt(ms + RMS_EPS) * gain_ref[...] * sg_ref[:, sl]).astype(BF16)

    @pl.when(t == nt - 1)
    def _():
        for h in range(HG_HEADS):
            so_ref[0, h] = st_ref[h].T


def _hgrn_prompt(p, hh, gain, s0t, *, batch, seq, tc):
    nt = seq // tc
    n = batch * seq
    mall, masks = _hgrn_tables()
    mall = jnp.asarray(mall, BF16)
    masks = jnp.asarray(masks, F32)
    col = lambda g: pl.BlockSpec((tc, GROUP), lambda b, t: (b * nt + t, g))
    st_spec = pl.BlockSpec((1, HG_HEADS, HG_D, HG_D), lambda b, t: (b, 0, 0, 0))
    return pl.pallas_call(
        functools.partial(_hgrn_body, n_chunks=tc // HG_CHUNK, nt=nt),
        grid=(batch, nt),
        in_specs=[col(P_HQ), col(P_LOGF), col(P_KK), col(H_HV), col(P_HG),
                  _const_spec((1, HG_D)), _const_spec(mall.shape), _const_spec(masks.shape), st_spec],
        out_specs=[pl.BlockSpec((tc, GROUP), lambda b, t: (b * nt + t, 0)), st_spec],
        out_shape=[jax.ShapeDtypeStruct((n, GROUP), BF16),
                   jax.ShapeDtypeStruct((batch, HG_HEADS, HG_D, HG_D), F32)],
        scratch_shapes=[pltpu.VMEM((HG_HEADS, HG_D, HG_D), F32), pltpu.VMEM((tc, GROUP), F32),
                        pltpu.VMEM((tc // HG_CHUNK, HG_CHUNK, GROUP), BF16),
                        pltpu.VMEM((tc // HG_CHUNK, HG_CHUNK, GROUP), BF16),
                        pltpu.VMEM((tc // HG_CHUNK, 1, GROUP), F32),
                        pltpu.VMEM((tc // HG_CHUNK, HG_HEADS, HG_D, HG_CHUNK), BF16),
                        pltpu.VMEM((2, masks.shape[0] - 1, HG_CHUNK, GROUP), F32),
                        pltpu.VMEM((2, HG_HEADS, HG_CHUNK, HG_CHUNK), BF16)],
        compiler_params=_params("parallel", "arbitrary"),
        name="hgrn_prompt",
    )(p, p, p, hh, p, gain, mall, masks, s0t)


def _ret_tables():
    c = RET_CHUNK
    f32 = np.float32
    lg = np.log1p(-np.exp2(-5.0 - np.arange(RET_HEADS, dtype=f32))).astype(f32)
    t = np.arange(c, dtype=f32)
    rel = t[:, None] - t[None, :]
    intra = np.where(rel[None] >= 0, np.exp(np.maximum(rel, 0.0)[None] * lg[:, None, None]), 0.0).astype(f32)
    lane_lg = np.repeat(lg, RET_D).reshape(RET_HEADS // 2, 1, 128)
    qdec = np.exp((t + 1.0)[None, :, None] * lane_lg).astype(f32)
    kdec = np.exp((c - 1.0 - t)[None, :, None] * lane_lg).astype(f32)
    cdec = np.exp(f32(c) * lane_lg).astype(f32)
    return tuple(jnp.asarray(a) for a in (intra, qdec, kdec, cdec))


def _ret_body(q_ref, k_ref, v_ref, sg_ref, intra_ref, qdec_ref, kdec_ref, cdec_ref, o_ref, so_ref, s_ref, a_ref,
              *, nt):
    t = pl.program_id(1)

    @pl.when(t == 0)
    def _():
        s_ref[...] = jnp.zeros_like(s_ref)

    c = RET_CHUNK
    lane = lax.broadcasted_iota(jnp.int32, (c, 128), 1)
    low = lane < RET_D
    r128 = lax.broadcasted_iota(jnp.int32, (128, 128), 0)
    c128 = lax.broadcasted_iota(jnp.int32, (128, 128), 1)
    same_head = (r128 < RET_D) == (c128 < RET_D)
    for p2 in range(RET_HEADS // 2):
        cols = slice(128 * p2, 128 * p2 + 128)
        qp, kb = q_ref[:, cols], k_ref[:, cols].astype(BF16)
        for hh in range(2):
            qm = jnp.where(low if hh == 0 else ~low, qp, 0.0).astype(BF16)
            a_ref[2 * p2 + hh] = (_dot_nt(qm, kb) * intra_ref[2 * p2 + hh]).astype(BF16)
    for p2 in range(RET_HEADS // 2):
        cols = slice(128 * p2, 128 * p2 + 128)
        qp, kp, vp = q_ref[:, cols], k_ref[:, cols], v_ref[:, cols]
        vb = vp.astype(BF16)
        s_bd = s_ref[p2]
        o = _dot((qp * qdec_ref[p2]).astype(BF16), s_bd.astype(BF16))
        for hh in range(2):
            in_head = low if hh == 0 else ~low
            o = o + jnp.where(in_head, _dot(a_ref[2 * p2 + hh], vb), 0.0)
        upd = _dot((kp * kdec_ref[p2]).T.astype(BF16), vb)
        s_ref[p2] = cdec_ref[p2] * s_bd + jnp.where(same_head, upd, 0.0)
        sq = o * o
        s_lo = jnp.sum(jnp.where(low, sq, 0.0), axis=-1, keepdims=True)
        s_hi = jnp.sum(jnp.where(low, 0.0, sq), axis=-1, keepdims=True)
        ms = jnp.where(low, s_lo, s_hi) * (1.0 / RET_D)
        o_ref[:, cols] = (o * lax.rsqrt(ms + RMS_EPS) * sg_ref[:, cols]).astype(BF16)

    @pl.when(t == nt - 1)
    def _():
        so_ref[0] = s_ref[...]


def _ret_prompt(p, hh, *, batch, seq):
    c = RET_CHUNK
    nt = seq // c
    n = batch * seq
    intra, qdec, kdec, cdec = _ret_tables()
    col = lambda g: pl.BlockSpec((c, GROUP), lambda b, t: (b * nt + t, g))
    npair = RET_HEADS // 2
    return pl.pallas_call(
        functools.partial(_ret_body, nt=nt),
        grid=(batch, nt),
        in_specs=[col(P_RQ), col(P_RK), col(H_RV), col(P_RG),
                  _const_spec(intra.shape), _const_spec(qdec.shape), _const_spec(kdec.shape),
                  _const_spec(cdec.shape)],
        out_specs=[pl.BlockSpec((c, GROUP), lambda b, t: (b * nt + t, 0)),
                   pl.BlockSpec((1, npair, 128, 128), lambda b, t: (b, 0, 0, 0))],
        out_shape=[jax.ShapeDtypeStruct((n, GROUP), BF16),
                   jax.ShapeDtypeStruct((batch, npair, 128, 128), F32)],
        scratch_shapes=[pltpu.VMEM((npair, 128, 128), F32), pltpu.VMEM((RET_HEADS, c, c), BF16)],
        compiler_params=_params("parallel", "arbitrary"),
        name="ret_prompt",
    )(p, p, hh, p, intra, qdec, kdec, cdec)


def _unpair_ret_state(s_bd):
    b = s_bd.shape[0]
    lo = s_bd[:, :, :RET_D, :RET_D]
    hi = s_bd[:, :, RET_D:, RET_D:]
    return jnp.stack([lo, hi], axis=2).reshape(b, RET_HEADS, RET_D, RET_D)


SAMPLE_PAGES_PER_STEP = 32


def _pages_feature_major(cache):
    d, n_pool = cache.shape[:2]
    return jnp.transpose(cache, (0, 1, 3, 4, 2)).reshape(d, n_pool, GROUP, PAGE_SIZE)


def _page_specs(layer):
    def spec(r):
        return pl.BlockSpec((None, None, GROUP, PAGE_SIZE),
                            lambda b, g, pt: (layer, pt[b, g * SAMPLE_PAGES_PER_STEP + r], 0, 0))
    return [spec(r) for r in range(SAMPLE_PAGES_PER_STEP)]


def _head_rows(row):
    lane = lax.broadcasted_iota(jnp.int32, (A_HEADS, GROUP), 1)
    hrow = lax.broadcasted_iota(jnp.int32, (A_HEADS, GROUP), 0)
    return jnp.where(lax.shift_right_logical(lane, 6) == hrow, jnp.broadcast_to(row, (A_HEADS, GROUP)), 0.0)


def _moba_scores_body(pt_ref, q_ref, kn_ref, *rest, n_steps, n_blocks):
    pages = rest[:SAMPLE_PAGES_PER_STEP]
    prob_ref, pself_ref, sel_ref, s_ref = rest[SAMPLE_PAGES_PER_STEP:]
    g = pl.program_id(1)
    qm = _head_rows(q_ref[...])
    q_hi, q_lo = _split_bf16(qm)
    q_both = jnp.concatenate([q_hi.astype(F32), q_lo.astype(F32)], axis=0).astype(BF16)
    per_blk = MOBA_BLOCK // PAGE_SIZE
    scale = A_HEAD_DIM ** -0.5
    for r in range(SAMPLE_PAGES_PER_STEP):
        k_hi, k_lo = _split_bf16(pages[r][...])
        s2 = _dot(q_both, k_hi)
        s_ref[g * SAMPLE_PAGES_PER_STEP + r] = s2[0:A_HEADS] + s2[A_HEADS:] + _dot(q_hi, k_lo)

    @pl.when(g == n_steps - 1)
    def _():
        jidx = lax.broadcasted_iota(jnp.int32, (A_HEADS, 128), 1)
        gt = jnp.full((A_HEADS, 128), -jnp.inf, F32)
        for j in range(n_blocks):
            tot = s_ref[per_blk * j]
            for r in range(1, per_blk):
                tot = tot + s_ref[per_blk * j + r]
            gj = jnp.sum(tot, axis=-1, keepdims=True) * (1.0 / MOBA_BLOCK)
            gt = jnp.where(jidx == j, gj, gt)
        rank = jnp.zeros((A_HEADS, 128), F32)
        for jp in range(n_blocks):
            cj = gt[:, jp:jp + 1]
            beats = (cj > gt) | ((cj == gt) & (jidx > jp))
            rank = rank + beats.astype(F32)
        sel = ((rank < MOBA_TOPK) & (jnp.abs(gt) < jnp.inf)).astype(F32)
        sel_ref[0] = sel
        s_self = jnp.sum(qm * scale * kn_ref[...], axis=-1, keepdims=True)
        masked = jnp.float32(-1e30)
        n_pages = n_blocks * per_blk
        m = s_self
        for pg in range(n_pages):
            keep = sel[:, pg // per_blk:pg // per_blk + 1] > 0.0
            m = jnp.maximum(m, jnp.max(jnp.where(keep, s_ref[pg] * scale, masked), axis=-1, keepdims=True))
        e_self = jnp.exp(s_self - m)
        l = e_self
        for pg in range(n_pages):
            keep = sel[:, pg // per_blk:pg // per_blk + 1] > 0.0
            e = jnp.exp(jnp.where(keep, s_ref[pg] * scale, masked) - m)
            prob_ref[0, pg] = e
            l = l + jnp.sum(e, axis=-1, keepdims=True)
        inv = 1.0 / l
        for pg in range(n_pages):
            prob_ref[0, pg] = prob_ref[0, pg] * inv
        pself_ref[0] = jnp.broadcast_to(e_self * inv, (A_HEADS, 128))


def _moba_sample_scores(page_table, p3, cache_k, *, layer):
    db, n_pages = page_table.shape
    n_steps = n_pages // SAMPLE_PAGES_PER_STEP
    n_blocks = n_pages * PAGE_SIZE // MOBA_BLOCK
    row = lambda g_: pl.BlockSpec((None, 1, GROUP), lambda b, g, pt: (b, 0, g_))
    grid_spec = pltpu.PrefetchScalarGridSpec(
        num_scalar_prefetch=1,
        grid=(db, n_steps),
        in_specs=[row(P_Q), row(P_K)] + _page_specs(layer),
        out_specs=[pl.BlockSpec((1, n_pages, A_HEADS, PAGE_SIZE), lambda b, g, pt: (b, 0, 0, 0)),
                   pl.BlockSpec((1, A_HEADS, 128), lambda b, g, pt: (b, 0, 0)),
                   pl.BlockSpec((1, A_HEADS, 128), lambda b, g, pt: (b, 0, 0))],
        scratch_shapes=[pltpu.VMEM((n_pages, A_HEADS, PAGE_SIZE), F32)],
    )
    return pl.pallas_call(
        functools.partial(_moba_scores_body, n_steps=n_steps, n_blocks=n_blocks),
        grid_spec=grid_spec,
        out_shape=[jax.ShapeDtypeStruct((db, n_pages, A_HEADS, PAGE_SIZE), F32),
                   jax.ShapeDtypeStruct((db, A_HEADS, 128), F32),
                   jax.ShapeDtypeStruct((db, A_HEADS, 128), F32)],
        compiler_params=_params("parallel", "arbitrary"),
        name="moba_sample_scores",
    )(page_table, p3, p3, *([cache_k] * SAMPLE_PAGES_PER_STEP))


def _selected_pages(page_table, sel):
    db, n_pages = page_table.shape
    per_blk = MOBA_BLOCK // PAGE_SIZE
    n_steps = n_pages // SAMPLE_PAGES_PER_STEP
    used = jnp.repeat(jnp.max(sel, axis=1)[:, :n_pages // per_blk] > 0.0, per_blk, axis=1)
    pages = page_table.reshape(db * n_steps, SAMPLE_PAGES_PER_STEP)
    used = used.reshape(db * n_steps, SAMPLE_PAGES_PER_STEP)
    order = jnp.arange(db * n_steps, dtype=jnp.int32)[:, None]
    last_used = lax.cummax(jnp.where(used, order, 0), axis=0)
    return jnp.take_along_axis(pages, last_used, axis=0).reshape(db, n_pages)


def _moba_out_body(pt_ref, prob_ref, pself_ref, vn_ref, *rest, n_steps):
    pages = rest[:SAMPLE_PAGES_PER_STEP]
    o_ref, acc_ref = rest[SAMPLE_PAGES_PER_STEP:]
    g = pl.program_id(1)

    @pl.when(g == 0)
    def _():
        acc_ref[...] = jnp.zeros_like(acc_ref)

    acc = acc_ref[...]
    for r in range(SAMPLE_PAGES_PER_STEP):
        pr = prob_ref[0, g * SAMPLE_PAGES_PER_STEP + r]
        acc = acc + _dot_nt(pr.astype(BF16), pages[r][...].astype(BF16))
    acc_ref[...] = acc

    @pl.when(g == n_steps - 1)
    def _():
        full = acc_ref[...] + pself_ref[0][:, 0:1] * vn_ref[...]
        lane = lax.broadcasted_iota(jnp.int32, (A_HEADS, GROUP), 1)
        hrow = lax.broadcasted_iota(jnp.int32, (A_HEADS, GROUP), 0)
        diag = jnp.where(lax.shift_right_logical(lane, 6) == hrow, full, 0.0)
        o_ref[...] = jnp.sum(diag, axis=0, keepdims=True)


def _moba_sample_out(page_table, probs, p_self, p3, cache_v, *, layer):
    db, n_pages = page_table.shape
    n_steps = n_pages // SAMPLE_PAGES_PER_STEP
    grid_spec = pltpu.PrefetchScalarGridSpec(
        num_scalar_prefetch=1,
        grid=(db, n_steps),
        in_specs=[pl.BlockSpec((1, n_pages, A_HEADS, PAGE_SIZE), lambda b, g, pt: (b, 0, 0, 0)),
                  pl.BlockSpec((1, A_HEADS, 128), lambda b, g, pt: (b, 0, 0)),
                  pl.BlockSpec((None, 1, GROUP), lambda b, g, pt: (b, 0, P_V))] + _page_specs(layer),
        out_specs=pl.BlockSpec((None, 1, GROUP), lambda b, g, pt: (b, 0, 0)),
        scratch_shapes=[pltpu.VMEM((A_HEADS, GROUP), F32)],
    )
    return pl.pallas_call(
        functools.partial(_moba_out_body, n_steps=n_steps),
        grid_spec=grid_spec,
        out_shape=jax.ShapeDtypeStruct((db, 1, GROUP), F32),
        compiler_params=_params("parallel", "arbitrary"),
        name="moba_sample_out",
    )(page_table, probs, p_self, p3, *([cache_v] * SAMPLE_PAGES_PER_STEP)).reshape(db, GROUP)


def _col_bcast(row):
    return jnp.broadcast_to(row, (128, 128)).T


STEP_SEQS = 8


def _hgrn_step_body(q_ref, lf_ref, kk_ref, v_ref, sg_ref, gain_ref, s_ref, o_ref, so_ref):
    for s in range(q_ref.shape[0]):
        for h in range(HG_HEADS):
            sl = slice(HG_D * h, HG_D * (h + 1))
            f_col = _col_bcast(jnp.exp(lf_ref[s, :, sl]))
            k_col = _col_bcast(kk_ref[s, :, sl])
            q_col = _col_bcast(q_ref[s, :, sl])
            s_new = f_col * s_ref[s, h] + k_col * v_ref[s, :, sl]
            so_ref[s, h] = s_new
            o = jnp.sum(q_col * s_new, axis=0, keepdims=True)
            ms = jnp.mean(o * o, axis=-1, keepdims=True)
            o_ref[s, :, sl] = o * lax.rsqrt(ms + RMS_EPS) * gain_ref[...] * sg_ref[s, :, sl]


def _hgrn_step(p3, h3, gain, state, *, layer):
    db = p3.shape[0]
    ns = math.gcd(db, STEP_SEQS)
    col = lambda g_: pl.BlockSpec((ns, 1, GROUP), lambda b: (b, 0, g_))
    st_shape = (ns, HG_HEADS, HG_D, HG_D)
    o, s = pl.pallas_call(
        _hgrn_step_body,
        grid=(db // ns,),
        in_specs=[col(P_HQ), col(P_LOGF), col(P_KK), col(H_HV), col(P_HG), _const_spec((1, HG_D)),
                  pl.BlockSpec((None,) + st_shape, lambda b: (layer, b, 0, 0, 0))],
        out_specs=[pl.BlockSpec((ns, 1, GROUP), lambda b: (b, 0, 0)),
                   pl.BlockSpec(st_shape, lambda b: (b, 0, 0, 0))],
        out_shape=[jax.ShapeDtypeStruct((db, 1, GROUP), F32),
                   jax.ShapeDtypeStruct((db, HG_HEADS, HG_D, HG_D), F32)],
        compiler_params=_params("parallel"),
        name="hgrn_step",
    )(p3, p3, p3, h3, p3, gain, state)
    return o.reshape(db, GROUP), s


def _ret_step_body(q_ref, k_ref, v_ref, sg_ref, s_ref, o_ref, so_ref):
    gammas = np.exp(np.log1p(-np.exp2(-5.0 - np.arange(RET_HEADS, dtype=np.float32))).astype(np.float32))
    for s in range(q_ref.shape[0]):
        outs = []
        for p2 in range(RET_HEADS // 2):
            cols = slice(128 * p2, 128 * p2 + 128)
            k_col = _col_bcast(k_ref[s, :, cols])
            q_col = _col_bcast(q_ref[s, :, cols])
            vrow = v_ref[s, :, cols]
            for hh in range(2):
                h = 2 * p2 + hh
                rs = slice(RET_D * hh, RET_D * (hh + 1))
                s_new = float(gammas[h]) * s_ref[s, h] + k_col[rs, 0:RET_D] * vrow[:, rs]
                so_ref[s, h] = s_new
                o = jnp.sum(q_col[rs, 0:RET_D] * s_new, axis=0, keepdims=True)
                ms = jnp.mean(o * o, axis=-1, keepdims=True)
                outs.append(o * lax.rsqrt(ms + RMS_EPS))
        o_ref[s] = jnp.concatenate(outs, axis=1) * sg_ref[s]


def _ret_step(p3, h3, state, *, layer):
    db = p3.shape[0]
    ns = math.gcd(db, STEP_SEQS)
    col = lambda g_: pl.BlockSpec((ns, 1, GROUP), lambda b: (b, 0, g_))
    st_shape = (ns, RET_HEADS, RET_D, RET_D)
    o, s = pl.pallas_call(
        _ret_step_body,
        grid=(db // ns,),
        in_specs=[col(P_RQ), col(P_RK), col(H_RV), col(P_RG),
                  pl.BlockSpec((None,) + st_shape, lambda b: (layer, b, 0, 0, 0))],
        out_specs=[pl.BlockSpec((ns, 1, GROUP), lambda b: (b, 0, 0)),
                   pl.BlockSpec(st_shape, lambda b: (b, 0, 0, 0))],
        out_shape=[jax.ShapeDtypeStruct((db, 1, GROUP), F32),
                   jax.ShapeDtypeStruct((db, RET_HEADS, RET_D, RET_D), F32)],
        compiler_params=_params("parallel"),
        name="ret_step",
    )(p3, p3, h3, p3, state)
    return o.reshape(db, GROUP), s


def _rotary_tables(pos):
    f32 = np.float32
    theta = (f32(1.0) / (f32(RET_ROPE_BASE) ** np.linspace(0.0, 1.0, RET_D // 2, dtype=f32))).astype(f32)
    ang = (pos.astype(f32)[:, None] * np.repeat(theta, 2)[None, :]).astype(f32)
    ang = np.tile(ang, (1, 128 // RET_D))
    cos, sin = np.cos(ang).astype(f32), np.sin(ang).astype(f32)
    even = (np.arange(128) % 2 == 0)[None, :]
    return tuple(jnp.asarray(a) for a in (cos, np.where(even, -sin, f32(0)), np.where(even, f32(0), sin)))


def _head_block_diag():
    r = np.arange(GROUP)
    return jnp.asarray((r[:, None] // A_HEAD_DIM) == (r[None, :] // A_HEAD_DIM), BF16)


def _layer_weights(l, lb_all, ffn1_norm, ffn1_w_in, ffn1_w_out, mix_norm, w_in, moba_q_gain, moba_k_gain,
                   hgrn_o_gain, w_branch_a, w_branch_b, w_branch_c, w_out, ffn2_norm, ffn2_w_in, ffn2_w_out):
    row = lambda a: a.reshape(1, -1).astype(F32)
    return dict(
        f1g=row(ffn1_norm[l]), f1a=ffn1_w_in[l].astype(BF16), f1b=ffn1_w_out[l].astype(BF16),
        mg=row(mix_norm[l]), w_in=w_in[l].astype(BF16),
        qg=row(jnp.tile(moba_q_gain[l], A_HEADS)), kg=row(jnp.tile(moba_k_gain[l], A_HEADS)),
        lb=row(lb_all[l]), hg=row(hgrn_o_gain[l]),
        wa=w_branch_a[l].astype(BF16), wb=w_branch_b[l].astype(BF16), wc=w_branch_c[l].astype(BF16),
        wo=w_out[l].astype(BF16),
        f2g=row(ffn2_norm[l]), f2a=ffn2_w_in[l].astype(BF16), f2b=ffn2_w_out[l].astype(BF16),
    )


def _prompt_layer(x, w, tabs, bd, kv_t, *, batch, seq):
    x = _ffn(x, w["f1g"], w["f1a"], w["f1b"], tm=512)
    p, hh, k_t, v_t = _proj(x, w["mg"], w["w_in"], w["qg"], w["kg"], w["lb"], *tabs, bd, tm=256, kv_t=kv_t)
    oa = _moba_prompt(p, batch=batch, seq=seq)
    s0t = jnp.zeros((batch, HG_HEADS, HG_D, HG_D), F32)
    ob, s_hg = _hgrn_prompt(p, hh, w["hg"], s0t, batch=batch, seq=seq, tc=512)
    oc, s_ret = _ret_prompt(p, hh, batch=batch, seq=seq)
    x = _merge(oa, ob, oc, hh, x, w["wa"], w["wb"], w["wc"], w["wo"], tm=512)
    x = _ffn(x, w["f2g"], w["f2a"], w["f2b"], tm=512)
    return x, k_t, v_t, s_hg, _unpair_ret_state(s_ret)


def _token_major(kv_t):
    d, b, _, t = kv_t.shape
    return jnp.transpose(kv_t.reshape(d, b, A_HEADS, A_HEAD_DIM, t), (0, 1, 4, 2, 3))


def _sample_layer(x, w, tabs, bd, l, page_table, cache_k, cache_v, state_hgrn, state_ret):
    db = x.shape[0]
    x = _ffn(x, w["f1g"], w["f1a"], w["f1b"], tm=db)
    p, hh = _proj(x, w["mg"], w["w_in"], w["qg"], w["kg"], w["lb"], *tabs, bd, tm=db)
    p3 = p.reshape(db, 1, P_GROUPS * GROUP)
    h3 = hh.reshape(db, 1, H_GROUPS * GROUP)
    probs, p_self, sel = _moba_sample_scores(page_table, p3, cache_k, layer=l)
    oa = _moba_sample_out(_selected_pages(page_table, sel), probs, p_self, p3, cache_v, layer=l)
    ob, s_hg = _hgrn_step(p3, h3, w["hg"], state_hgrn, layer=l)
    oc, s_ret = _ret_step(p3, h3, state_ret, layer=l)
    x = _merge(oa, ob, oc, hh, x, w["wa"], w["wb"], w["wc"], w["wo"], tm=db)
    x = _ffn(x, w["f2g"], w["f2a"], w["f2b"], tm=db)
    k_new = p[:, P_K * GROUP:(P_K + 1) * GROUP].reshape(db, 1, A_HEADS, A_HEAD_DIM)
    v_new = p[:, P_V * GROUP:(P_V + 1) * GROUP].reshape(db, 1, A_HEADS, A_HEAD_DIM)
    return x, k_new, v_new, s_hg, s_ret


def kernel(x_prompt, x_sample, cache_k, cache_v, state_hgrn, state_ret, page_table, ffn1_norm, ffn1_w_in, ffn1_w_out, mix_norm, w_in, moba_q_gain, moba_k_gain, hgrn_lb, hgrn_o_gain, w_branch_a, w_branch_b, w_branch_c, w_out, ffn2_norm, ffn2_w_in, ffn2_w_out):
    lb_cum = jnp.cumsum(jax.nn.softmax(hgrn_lb.astype(F32), axis=0), axis=0)
    lb_all = lb_cum - lb_cum[:1]
    bp, seq, _ = x_prompt.shape
    db, dec_seq, _ = x_sample.shape
    assert dec_seq == 1
    n_pages = page_table.shape[1]
    past_len = n_pages * PAGE_SIZE
    assert past_len % MOBA_BLOCK == 0 and n_pages % SAMPLE_PAGES_PER_STEP == 0
    tabs_p = _rotary_tables(np.arange(seq))
    tabs_s = _rotary_tables(np.full((db,), past_len))
    bd = _head_block_diag()
    ck = _pages_feature_major(cache_k)
    cv = _pages_feature_major(cache_v)
    xp = x_prompt.reshape(bp * seq, D_MODEL)
    xs = x_sample.reshape(db, D_MODEL)
    cols = [[] for _ in range(6)]
    kp = jnp.zeros((DEPTH, bp, GROUP, seq), F32)
    vp = jnp.zeros((DEPTH, bp, GROUP, seq), F32)
    for l in range(DEPTH):
        w = _layer_weights(l, lb_all, ffn1_norm, ffn1_w_in, ffn1_w_out, mix_norm, w_in, moba_q_gain, moba_k_gain, hgrn_o_gain, w_branch_a, w_branch_b, w_branch_c, w_out, ffn2_norm, ffn2_w_in, ffn2_w_out)
        xp, kp, vp, hp, rp = _prompt_layer(xp, w, tabs_p, bd, (kp, vp, l), batch=bp, seq=seq)
        xs, ks, vs, hs, rs = _sample_layer(xs, w, tabs_s, bd, l, page_table, ck, cv, state_hgrn, state_ret)
        for c, a in zip(cols, (ks, vs, hp, hs, rp, rs)):
            c.append(a)
    outs = [_token_major(kp), _token_major(vp)] + [jnp.stack(c) for c in cols]
    return (xp.reshape(bp, seq, D_MODEL), xs.reshape(db, 1, D_MODEL)) + tuple(outs)
```

```python
import functools
import math

import numpy as np
import jax
import jax.numpy as jnp
from jax import lax
from jax.experimental import pallas as pl
from jax.experimental.pallas import tpu as pltpu

F32 = jnp.float32
BF16 = jnp.bfloat16

D_MODEL = 1024
D_FF = 2816
DEPTH = 4
RMS_EPS = 1e-6
A_HEADS = 8
A_HEAD_DIM = 64
MOBA_BLOCK = 256
MOBA_TOPK = 3
MOBA_ONES_ROWS = 16
HG_HEADS = 4
HG_D = 128
RET_HEADS = 8
RET_D = 64
RET_ROPE_BASE = 10000.0
PAGE_SIZE = 128
GROUP = 512

P_Q, P_K, P_V, P_HQ, P_LOGF, P_KK, P_HG, P_RQ, P_RK, P_RG = range(10)
P_GROUPS = 10
H_HV, H_RV, H_GATES = range(3)
H_GROUPS = 8

HG_CHUNK = 64
HG_SUBLANE_LEVEL = 8
RET_CHUNK = 256

VMEM_LIMIT = 56 * 1024 * 1024


def _dot(a, b):
    return jnp.dot(a, b, preferred_element_type=F32)


def _dot_nt(a, b):
    return lax.dot_general(a, b, (((1,), (1,)), ((), ())), preferred_element_type=F32)


def _split_bf16(a):
    hi = a.astype(BF16)
    lo = (a - hi.astype(F32)).astype(BF16)
    return hi, lo


def _params(*sem):
    return pltpu.CompilerParams(dimension_semantics=sem, vmem_limit_bytes=VMEM_LIMIT)


def _const_spec(shape):
    nd = len(shape)
    return pl.BlockSpec(shape, lambda *_: (0,) * nd)


def _ffn_body(x_ref, g_ref, w1_ref, w2_ref, o_ref, acc_ref, *, ff_chunk):
    x = x_ref[...]
    ms = jnp.mean(x * x, axis=-1, keepdims=True)
    h = (x * lax.rsqrt(ms + RMS_EPS) * g_ref[...]).astype(BF16)
    for c in range(D_FF // ff_chunk):
        lo = c * ff_chunk
        gate = _dot(h, w1_ref[:, lo:lo + ff_chunk])
        up = _dot(h, w1_ref[:, D_FF + lo:D_FF + lo + ff_chunk])
        act = (gate * jax.nn.sigmoid(gate) * up).astype(BF16)
        part = _dot(act, w2_ref[lo:lo + ff_chunk, :])
        if c == 0:
            acc_ref[...] = part
        else:
            acc_ref[...] += part
    o_ref[...] = x + 0.5 * acc_ref[...]


def _ffn(x, gain, w1, w2, *, tm):
    n = x.shape[0]
    return pl.pallas_call(
        functools.partial(_ffn_body, ff_chunk=256),
        grid=(n // tm,),
        in_specs=[
            pl.BlockSpec((tm, D_MODEL), lambda i: (i, 0)),
            _const_spec((1, D_MODEL)),
            pl.BlockSpec((D_MODEL, 2 * D_FF), lambda i: (0, 0), pipeline_mode=pl.Buffered(1)),
            pl.BlockSpec((D_FF, D_MODEL), lambda i: (0, 0), pipeline_mode=pl.Buffered(1)),
        ],
        out_specs=pl.BlockSpec((tm, D_MODEL), lambda i: (i, 0)),
        out_shape=jax.ShapeDtypeStruct((n, D_MODEL), F32),
        scratch_shapes=[pltpu.VMEM((tm, D_MODEL), F32)],
        compiler_params=_params("parallel"),
        name="ffn",
    )(x, gain, w1, w2)


def _proj_body(x_ref, g_ref, w_ref, qg_ref, kg_ref, lb_ref, cos_ref, sa_ref, sb_ref, bd_ref, *rest,
               n_passthrough):
    o_ref, o16_ref, *kv_t_refs = rest[n_passthrough:]
    x = x_ref[...]
    ms = jnp.mean(x * x, axis=-1, keepdims=True)
    h = (x * lax.rsqrt(ms + RMS_EPS) * g_ref[...]).astype(BF16)

    def mm(wg):
        return _dot(h, w_ref[:, wg * GROUP:(wg + 1) * GROUP])

    def put(g, val):
        o_ref[:, g * GROUP:(g + 1) * GROUP] = val

    def put16(g, val):
        o16_ref[:, g * GROUP:(g + 1) * GROUP] = val.astype(BF16)

    def headnorm(a, gain_row):
        ss = _dot((a * a).astype(BF16), bd_ref[...])
        return a * lax.rsqrt(ss * (1.0 / A_HEAD_DIM) + RMS_EPS) * gain_row

    def rotary(g, a, scale):
        cos, sa, sb = cos_ref[...], sa_ref[...], sb_ref[...]
        for s in range(GROUP // 128):
            xs = a[:, 128 * s:128 * s + 128]
            nxt = pltpu.roll(xs, 127, 1)
            prv = pltpu.roll(xs, 1, 1)
            r = xs * cos + nxt * sa + prv * sb
            o_ref[:, g * GROUP + 128 * s:g * GROUP + 128 * s + 128] = r * scale if scale != 1.0 else r

    put(P_Q, headnorm(mm(0), qg_ref[...]))
    k_n = headnorm(mm(1), kg_ref[...])
    v = mm(2)
    put(P_K, k_n)
    put(P_V, v)
    if kv_t_refs:
        kv_t_refs[0][...] = k_n.T
        kv_t_refs[1][...] = v.T
    put(P_HQ, mm(3))

    fr = mm(4)
    lb = lb_ref[...]
    log_sig = jnp.minimum(fr, 0.0) - jnp.log1p(jnp.exp(-jnp.abs(fr)))
    a = jnp.log(lb)
    c = jnp.log1p(-lb) + log_sig
    put(P_LOGF, jnp.maximum(a, c) + jnp.log1p(jnp.exp(-jnp.abs(a - c))))
    put(P_KK, (1.0 - lb) * jax.nn.sigmoid(-fr))

    put16(H_HV, mm(5))
    hg = mm(6)
    put(P_HG, hg * jax.nn.sigmoid(hg))
    rotary(P_RQ, mm(7), 1.0)
    rotary(P_RK, mm(8), RET_D ** -0.5)
    put16(H_RV, mm(9))
    rg = mm(10)
    put(P_RG, rg * jax.nn.sigmoid(rg))
    for t in range(H_GROUPS - H_GATES):
        put16(H_GATES + t, jax.nn.sigmoid(mm(11 + t)))


def _proj(x, gain, w, qg, kg, lb, cos, sa, sb, bd, *, tm, kv_t=None):
    n = x.shape[0]
    n_tab = cos.shape[0] // tm
    tab = pl.BlockSpec((tm, 128), lambda i: (i % n_tab, 0))
    in_specs = [
        pl.BlockSpec((tm, D_MODEL), lambda i: (i, 0)),
        _const_spec((1, D_MODEL)),
        pl.BlockSpec(w.shape, lambda i: (0, 0), pipeline_mode=pl.Buffered(1)),
        _const_spec((1, GROUP)),
        _const_spec((1, GROUP)),
        _const_spec((1, GROUP)),
        tab, tab, tab,
        _const_spec((GROUP, GROUP)),
    ]
    args = [x, gain, w, qg, kg, lb, cos, sa, sb, bd]
    out_specs = [pl.BlockSpec((tm, P_GROUPS * GROUP), lambda i: (i, 0)),
                 pl.BlockSpec((tm, H_GROUPS * GROUP), lambda i: (i, 0))]
    out_shape = [jax.ShapeDtypeStruct((n, P_GROUPS * GROUP), F32),
                 jax.ShapeDtypeStruct((n, H_GROUPS * GROUP), BF16)]
    aliases = {}
    if kv_t is not None:
        k_all, v_all, layer = kv_t
        nt = k_all.shape[3] // tm
        for a in (k_all, v_all):
            aliases[len(args)] = len(out_shape)
            in_specs.append(pl.BlockSpec(memory_space=pl.ANY))
            args.append(a)
            out_specs.append(pl.BlockSpec((None, None, GROUP, tm), lambda i: (layer, i // nt, 0, i % nt)))
            out_shape.append(jax.ShapeDtypeStruct(a.shape, a.dtype))
    return pl.pallas_call(
        functools.partial(_proj_body, n_passthrough=len(aliases)),
        grid=(n // tm,),
        in_specs=in_specs,
        out_specs=out_specs,
        out_shape=out_shape,
        input_output_aliases=aliases,
        compiler_params=_params("parallel"),
        name="proj",
    )(*args)


def _merge_body(oa_ref, ob_ref, oc_ref, ga_ref, gb_ref, gc_ref, x_ref, wa_ref, wb_ref, wc_ref, wo_ref, o_ref):
    m = ga_ref[...] * _dot(oa_ref[...].astype(BF16), wa_ref[...])
    m = m + gb_ref[...] * _dot(ob_ref[...].astype(BF16), wb_ref[...])
    m = m + gc_ref[...] * _dot(oc_ref[...].astype(BF16), wc_ref[...])
    o_ref[...] = x_ref[...] + _dot(m.astype(BF16), wo_ref[...])


def _merge(oa, ob, oc, hh, x, wa, wb, wc, wo, *, tm):
    n = x.shape[0]
    row = lambda i: (i, 0)
    g0 = H_GATES // 2
    return pl.pallas_call(
        _merge_body,
        grid=(n // tm,),
        in_specs=[
            pl.BlockSpec((tm, GROUP), row),
            pl.BlockSpec((tm, GROUP), row),
            pl.BlockSpec((tm, GROUP), row),
            pl.BlockSpec((tm, D_MODEL), lambda i: (i, g0)),
            pl.BlockSpec((tm, D_MODEL), lambda i: (i, g0 + 1)),
            pl.BlockSpec((tm, D_MODEL), lambda i: (i, g0 + 2)),
            pl.BlockSpec((tm, D_MODEL), row),
            _const_spec((GROUP, D_MODEL)),
            _const_spec((GROUP, D_MODEL)),
            _const_spec((GROUP, D_MODEL)),
            _const_spec((D_MODEL, D_MODEL)),
        ],
        out_specs=pl.BlockSpec((tm, D_MODEL), row),
        out_shape=jax.ShapeDtypeStruct((n, D_MODEL), F32),
        compiler_params=_params("parallel"),
        name="merge",
    )(oa, ob, oc, hh, hh, hh, x, wa, wb, wc, wo)


def _moba_body(q_ref, k_ref, v_ref, o_ref, k16, vt16, kmt, bias_ref, acc_ref, qmt_ref, m_ref, s_ref, *, nb):
    i = pl.program_id(1)
    blk = MOBA_BLOCK
    dh = A_HEAD_DIM
    masked = jnp.float32(-1e30)

    @pl.when(i == 0)
    def _():
        lane = lax.broadcasted_iota(jnp.int32, (A_HEADS, GROUP), 1)
        hrow = lax.broadcasted_iota(jnp.int32, (A_HEADS, GROUP), 0)
        headmask = (lax.shift_right_logical(lane, 6) == hrow).astype(F32)
        lane128 = lax.broadcasted_iota(jnp.int32, (blk, 128), 1)
        ones = jnp.ones((MOBA_ONES_ROWS, blk), BF16)

        def init(jj, c):
            r = pl.multiple_of(jj * blk, blk)
            kb = k_ref[pl.ds(r, blk), :]
            km = jnp.sum(kb, axis=0, keepdims=True) * (1.0 / blk)
            kmt[pl.ds(pl.multiple_of(jj * A_HEADS, A_HEADS), A_HEADS), :] = km * headmask
            indicator = jnp.where(lane128 == dh + jj, 1.0, 0.0)
            vt = v_ref[pl.ds(r, blk), :].T
            for h in range(A_HEADS):
                slab = kb[:, 128 * (h // 2):128 * (h // 2) + 128]
                if h % 2:
                    slab = pltpu.roll(slab, dh, 1)
                k16[jj, h] = jnp.where(lane128 < dh, slab, indicator).astype(BF16)
                vt16[jj, h, 0:dh, :] = vt[dh * h:dh * (h + 1), :].astype(BF16)
                vt16[jj, h, dh:dh + MOBA_ONES_ROWS, :] = ones
            return c

        lax.fori_loop(0, nb, init, 0)

    q = q_ref[...]
    kh, kl = _split_bf16(kmt[...])
    qh, ql = _split_bf16(q)
    gt = _dot_nt(kh, qh) + _dot_nt(kh, ql) + _dot_nt(kl, qh)
    neg_inf = jnp.float32(-jnp.inf)
    g = [jnp.where(j < i, gt[A_HEADS * j:A_HEADS * (j + 1), :], neg_inf) for j in range(nb)]
    for j in range(nb):
        rank = jnp.zeros((A_HEADS, blk), F32)
        for jp in range(nb):
            if jp == j:
                continue
            beats = (g[jp] >= g[j]) if jp < j else (g[jp] > g[j])
            rank = rank + beats.astype(F32)
        sel = (rank < MOBA_TOPK) & (jnp.abs(g[j]) < jnp.inf)
        bias = jnp.where(sel | (j == i), 0.0, masked)
        for h in range(A_HEADS):
            bias_ref[h, j:j + 1, :] = bias[h:h + 1, :]

    qt = (q * (dh ** -0.5 * math.log2(math.e))).T
    pad = jnp.zeros((128 - dh - nb, blk), F32)
    for h in range(A_HEADS):
        qmt_ref[h] = jnp.concatenate([qt[dh * h:dh * (h + 1), :], bias_ref[h], pad], axis=0).astype(BF16)

    m_ref[...] = jnp.full(m_ref.shape, masked, F32)
    acc_ref[...] = jnp.zeros_like(acc_ref)

    def block_steps(js, causal=None):
        for t, j in enumerate(js):
            for h in range(A_HEADS):
                s_ref[t, h] = _dot(k16[j, h], qmt_ref[h])
        for t, j in enumerate(js):
            for h in range(A_HEADS):
                st = s_ref[t, h]
                if causal is not None:
                    st = jnp.where(causal, st, masked)
                m_old = m_ref[h:h + 1, :]
                m_new = jnp.maximum(m_old, jnp.max(st, axis=0, keepdims=True))
                alpha = jnp.exp2(m_old - m_new)
                pt = jnp.exp2(st - m_new).astype(BF16)
                m_ref[h:h + 1, :] = m_new
                acc_ref[h] = alpha * acc_ref[h] + _dot(vt16[j, h], pt)

    kidx = lax.broadcasted_iota(jnp.int32, (blk, blk), 0)
    qidx = lax.broadcasted_iota(jnp.int32, (blk, blk), 1)
    block_steps([i], kidx <= qidx)

    @pl.when(i % 2 == 1)
    def _():
        block_steps([0])

    def body(t, c):
        j = i % 2 + 2 * t
        block_steps([j, j + 1])
        return c

    lax.fori_loop(0, i // 2, body, 0)

    per = blk // dh
    for h in range(A_HEADS):
        s_ref[0, h // per, dh * (h % per):dh * (h % per + 1), :] = acc_ref[h, 0:dh, :] / acc_ref[h, dh:dh + 1, :]
    o_ref[...] = jnp.concatenate([s_ref[0, t] for t in range(A_HEADS // per)], axis=0).T.astype(BF16)


def _moba_prompt(p, *, batch, seq):
    nb = seq // MOBA_BLOCK
    n = batch * seq
    return pl.pallas_call(
        functools.partial(_moba_body, nb=nb),
        grid=(batch, nb),
        in_specs=[
            pl.BlockSpec((MOBA_BLOCK, GROUP), lambda b, i: (b * nb + i, P_Q)),
            pl.BlockSpec((seq, GROUP), lambda b, i: (b, P_K), pipeline_mode=pl.Buffered(1)),
            pl.BlockSpec((seq, GROUP), lambda b, i: (b, P_V), pipeline_mode=pl.Buffered(1)),
        ],
        out_specs=pl.BlockSpec((MOBA_BLOCK, GROUP), lambda b, i: (b * nb + i, 0)),
        out_shape=jax.ShapeDtypeStruct((n, GROUP), BF16),
        scratch_shapes=[
            pltpu.VMEM((nb, A_HEADS, MOBA_BLOCK, 128), BF16),
            pltpu.VMEM((nb, A_HEADS, A_HEAD_DIM + MOBA_ONES_ROWS, MOBA_BLOCK), BF16),
            pltpu.VMEM((nb * A_HEADS, GROUP), F32),
            pltpu.VMEM((A_HEADS, nb, MOBA_BLOCK), F32),
            pltpu.VMEM((A_HEADS, A_HEAD_DIM + MOBA_ONES_ROWS, MOBA_BLOCK), F32),
            pltpu.VMEM((A_HEADS, 128, MOBA_BLOCK), BF16),
            pltpu.VMEM((A_HEADS, MOBA_BLOCK), F32),
            pltpu.VMEM((2, A_HEADS, MOBA_BLOCK, MOBA_BLOCK), F32),
        ],
        compiler_params=_params("parallel", "arbitrary"),
        name="moba_prompt",
    )(p, p, p)


def _hgrn_tables():
    c = HG_CHUNK
    t = np.arange(c)[:, None]
    u = np.arange(c)[None, :]
    mats = [(u <= t)]
    masks = [np.eye(c, dtype=bool)]
    m = c // 2
    while m >= 1:
        b0 = (t // (2 * m)) * (2 * m)
        right = t >= b0 + m
        if m < HG_SUBLANE_LEVEL:
            mats.append(np.where(right, (u >= b0 + m) & (u <= t), (u > t) & (u <= b0 + m - 1)))
        s = u
        masks.append(((s // (2 * m)) == (t // (2 * m))) & right & (s < b0 + m))
        m //= 2
    return (np.concatenate(mats, 0).astype(np.float32), np.stack(masks).astype(np.float32))


def _hgrn_body(q_ref, lf_ref, kk_ref, v_ref, sg_ref, gain_ref, mall_ref, masks_ref, s0_ref,
               o_ref, so_ref, st_ref, ob_ref, qd_ref, kd_ref, dl_ref, vt_ref, ex_ref, a_ref, *, n_chunks, nt):
    t = pl.program_id(1)
    c = HG_CHUNK

    @pl.when(t == 0)
    def _():
        st_ref[...] = s0_ref[0]

    row = lax.broadcasted_iota(jnp.int32, (c, 1), 0)

    n_lv = masks_ref.shape[0] - 1

    def decays(ci):
        rows = slice(ci * c, (ci + 1) * c)
        hi, lo = _split_bf16(lf_ref[rows, :])
        mall = mall_ref[...]
        e_mm = _dot(mall, hi) + _dot(mall, lo)
        b = e_mm[0:c]
        b_last = b[c - 1:c, :]
        lv = 0
        m = c // 2
        while m >= HG_SUBLANE_LEVEL:
            ref = jnp.concatenate(
                [jnp.broadcast_to(b[s + m - 1:s + m, :], (2 * m, GROUP)) for s in range(0, c, 2 * m)], axis=0)
            d = b - ref
            ex_ref[ci % 2, lv] = jnp.exp(jnp.where((row & (2 * m - 1)) >= m, d, -d))
            lv += 1
            m //= 2
        for i in range(1, mall.shape[0] // c):
            ex_ref[ci % 2, lv] = jnp.exp(e_mm[i * c:(i + 1) * c])
            lv += 1
        qd_ref[ci] = (q_ref[rows, :] * jnp.exp(b)).astype(BF16)
        kd_ref[ci] = (kk_ref[rows, :] * jnp.exp(b_last - b)).astype(BF16)
        dl_ref[ci] = jnp.exp(b_last)

    def pair_weights(ci):
        rows = slice(ci * c, (ci + 1) * c)
        for h in range(HG_HEADS):
            sl = slice(HG_D * h, HG_D * (h + 1))
            qh, kh = q_ref[rows, sl], kk_ref[rows, sl]
            a = masks_ref[0] * _dot_nt(qh.astype(BF16), kh.astype(BF16))
            for lv in range(n_lv):
                e = ex_ref[ci % 2, lv, :, sl]
                a = a + masks_ref[lv + 1] * _dot_nt((qh * e).astype(BF16), (kh * e).astype(BF16))
            a_ref[ci % 2, h] = a.astype(BF16)

    def intra_out(ci):
        rows = slice(ci * c, (ci + 1) * c)
        for h in range(HG_HEADS):
            sl = slice(HG_D * h, HG_D * (h + 1))
            vh = v_ref[rows, sl]
            ob_ref[rows, sl] = _dot(a_ref[ci % 2, h], vh.astype(BF16))
            vt_ref[ci, h] = vh.astype(F32).T.astype(BF16)

    def carry(ci):
        rows = slice(ci * c, (ci + 1) * c)
        for h in range(HG_HEADS):
            sl = slice(HG_D * h, HG_D * (h + 1))
            s_t = st_ref[h]
            ob_ref[rows, sl] += _dot_nt(qd_ref[ci, :, sl], s_t.astype(BF16))
            st_ref[h] = dl_ref[ci, :, sl] * s_t + _dot(vt_ref[ci, h], kd_ref[ci, :, sl])

    decays(0)
    pair_weights(0)
    for ci in range(n_chunks):
        if ci + 1 < n_chunks:
            decays(ci + 1)
        intra_out(ci)
        carry(ci)
        if ci + 1 < n_chunks:
            pair_weights(ci + 1)

    for h in range(HG_HEADS):
        sl = slice(HG_D * h, HG_D * (h + 1))
        x = ob_ref[:, sl]
        ms = jnp.mean(x * x, axis=-1, keepdims=True)
        o_ref[:, sl] = (x * lax.rsqrt(ms + RMS_EPS) * gain_ref[...] * sg_ref[:, sl]).astype(BF16)

    @pl.when(t == nt - 1)
    def _():
        for h in range(HG_HEADS):
            so_ref[0, h] = st_ref[h].T


def _hgrn_prompt(p, hh, gain, s0t, *, batch, seq, tc):
    nt = seq // tc
    n = batch * seq
    mall, masks = _hgrn_tables()
    mall = jnp.asarray(mall, BF16)
    masks = jnp.asarray(masks, F32)
    col = lambda g: pl.BlockSpec((tc, GROUP), lambda b, t: (b * nt + t, g))
    st_spec = pl.BlockSpec((1, HG_HEADS, HG_D, HG_D), lambda b, t: (b, 0, 0, 0))
    return pl.pallas_call(
        functools.partial(_hgrn_body, n_chunks=tc // HG_CHUNK, nt=nt),
        grid=(batch, nt),
        in_specs=[col(P_HQ), col(P_LOGF), col(P_KK), col(H_HV), col(P_HG),
                  _const_spec((1, HG_D)), _const_spec(mall.shape), _const_spec(masks.shape), st_spec],
        out_specs=[pl.BlockSpec((tc, GROUP), lambda b, t: (b * nt + t, 0)), st_spec],
        out_shape=[jax.ShapeDtypeStruct((n, GROUP), BF16),
                   jax.ShapeDtypeStruct((batch, HG_HEADS, HG_D, HG_D), F32)],
        scratch_shapes=[pltpu.VMEM((HG_HEADS, HG_D, HG_D), F32), pltpu.VMEM((tc, GROUP), F32),
                        pltpu.VMEM((tc // HG_CHUNK, HG_CHUNK, GROUP), BF16),
                        pltpu.VMEM((tc // HG_CHUNK, HG_CHUNK, GROUP), BF16),
                        pltpu.VMEM((tc // HG_CHUNK, 1, GROUP), F32),
                        pltpu.VMEM((tc // HG_CHUNK, HG_HEADS, HG_D, HG_CHUNK), BF16),
                        pltpu.VMEM((2, masks.shape[0] - 1, HG_CHUNK, GROUP), F32),
                        pltpu.VMEM((2, HG_HEADS, HG_CHUNK, HG_CHUNK), BF16)],
        compiler_params=_params("parallel", "arbitrary"),
        name="hgrn_prompt",
    )(p, p, p, hh, p, gain, mall, masks, s0t)


def _ret_tables():
    c = RET_CHUNK
    f32 = np.float32
    lg = np.log1p(-np.exp2(-5.0 - np.arange(RET_HEADS, dtype=f32))).astype(f32)
    t = np.arange(c, dtype=f32)
    rel = t[:, None] - t[None, :]
    intra = np.where(rel[None] >= 0, np.exp(np.maximum(rel, 0.0)[None] * lg[:, None, None]), 0.0).astype(f32)
    lane_lg = np.repeat(lg, RET_D).reshape(RET_HEADS // 2, 1, 128)
    qdec = np.exp((t + 1.0)[None, :, None] * lane_lg).astype(f32)
    kdec = np.exp((c - 1.0 - t)[None, :, None] * lane_lg).astype(f32)
    cdec = np.exp(f32(c) * lane_lg).astype(f32)
    return tuple(jnp.asarray(a) for a in (intra, qdec, kdec, cdec))


def _ret_body(q_ref, k_ref, v_ref, sg_ref, intra_ref, qdec_ref, kdec_ref, cdec_ref, o_ref, so_ref, s_ref, a_ref,
              *, nt):
    t = pl.program_id(1)

    @pl.when(t == 0)
    def _():
        s_ref[...] = jnp.zeros_like(s_ref)

    c = RET_CHUNK
    lane = lax.broadcasted_iota(jnp.int32, (c, 128), 1)
    low = lane < RET_D
    r128 = lax.broadcasted_iota(jnp.int32, (128, 128), 0)
    c128 = lax.broadcasted_iota(jnp.int32, (128, 128), 1)
    same_head = (r128 < RET_D) == (c128 < RET_D)
    for p2 in range(RET_HEADS // 2):
        cols = slice(128 * p2, 128 * p2 + 128)
        qp, kb = q_ref[:, cols], k_ref[:, cols].astype(BF16)
        for hh in range(2):
            qm = jnp.where(low if hh == 0 else ~low, qp, 0.0).astype(BF16)
            a_ref[2 * p2 + hh] = (_dot_nt(qm, kb) * intra_ref[2 * p2 + hh]).astype(BF16)
    for p2 in range(RET_HEADS // 2):
        cols = slice(128 * p2, 128 * p2 + 128)
        qp, kp, vp = q_ref[:, cols], k_ref[:, cols], v_ref[:, cols]
        vb = vp.astype(BF16)
        s_bd = s_ref[p2]
        o = _dot((qp * qdec_ref[p2]).astype(BF16), s_bd.astype(BF16))
        for hh in range(2):
            in_head = low if hh == 0 else ~low
            o = o + jnp.where(in_head, _dot(a_ref[2 * p2 + hh], vb), 0.0)
        upd = _dot((kp * kdec_ref[p2]).T.astype(BF16), vb)
        s_ref[p2] = cdec_ref[p2] * s_bd + jnp.where(same_head, upd, 0.0)
        sq = o * o
        s_lo = jnp.sum(jnp.where(low, sq, 0.0), axis=-1, keepdims=True)
        s_hi = jnp.sum(jnp.where(low, 0.0, sq), axis=-1, keepdims=True)
        ms = jnp.where(low, s_lo, s_hi) * (1.0 / RET_D)
        o_ref[:, cols] = (o * lax.rsqrt(ms + RMS_EPS) * sg_ref[:, cols]).astype(BF16)

    @pl.when(t == nt - 1)
    def _():
        so_ref[0] = s_ref[...]


def _ret_prompt(p, hh, *, batch, seq):
    c = RET_CHUNK
    nt = seq // c
    n = batch * seq
    intra, qdec, kdec, cdec = _ret_tables()
    col = lambda g: pl.BlockSpec((c, GROUP), lambda b, t: (b * nt + t, g))
    npair = RET_HEADS // 2
    return pl.pallas_call(
        functools.partial(_ret_body, nt=nt),
        grid=(batch, nt),
        in_specs=[col(P_RQ), col(P_RK), col(H_RV), col(P_RG),
                  _const_spec(intra.shape), _const_spec(qdec.shape), _const_spec(kdec.shape),
                  _const_spec(cdec.shape)],
        out_specs=[pl.BlockSpec((c, GROUP), lambda b, t: (b * nt + t, 0)),
                   pl.BlockSpec((1, npair, 128, 128), lambda b, t: (b, 0, 0, 0))],
        out_shape=[jax.ShapeDtypeStruct((n, GROUP), BF16),
                   jax.ShapeDtypeStruct((batch, npair, 128, 128), F32)],
        scratch_shapes=[pltpu.VMEM((npair, 128, 128), F32), pltpu.VMEM((RET_HEADS, c, c), BF16)],
        compiler_params=_params("parallel", "arbitrary"),
        name="ret_prompt",
    )(p, p, hh, p, intra, qdec, kdec, cdec)


def _unpair_ret_state(s_bd):
    b = s_bd.shape[0]
    lo = s_bd[:, :, :RET_D, :RET_D]
    hi = s_bd[:, :, RET_D:, RET_D:]
    return jnp.stack([lo, hi], axis=2).reshape(b, RET_HEADS, RET_D, RET_D)


SAMPLE_PAGES_PER_STEP = 64


def _pages_feature_major(cache):
    d, n_pool = cache.shape[:2]
    return jnp.transpose(cache, (0, 1, 3, 4, 2)).reshape(d, n_pool, GROUP, PAGE_SIZE)


def _page_specs(layer):
    def spec(r):
        return pl.BlockSpec((None, None, GROUP, PAGE_SIZE),
                            lambda b, g, pt, *_: (layer, pt[b, g * SAMPLE_PAGES_PER_STEP + r], 0, 0))
    return [spec(r) for r in range(SAMPLE_PAGES_PER_STEP)]


def _head_rows(row):
    lane = lax.broadcasted_iota(jnp.int32, (A_HEADS, GROUP), 1)
    hrow = lax.broadcasted_iota(jnp.int32, (A_HEADS, GROUP), 0)
    return jnp.where(lax.shift_right_logical(lane, 6) == hrow, jnp.broadcast_to(row, (A_HEADS, GROUP)), 0.0)


def _moba_scores_body(pt_ref, q_ref, kn_ref, *rest, n_steps, n_blocks):
    pages = rest[:SAMPLE_PAGES_PER_STEP]
    prob_ref, pself_ref, sel_ref, s_ref = rest[SAMPLE_PAGES_PER_STEP:]
    g = pl.program_id(1)
    qm = _head_rows(q_ref[...])
    q_hi, q_lo = _split_bf16(qm)
    q_both = jnp.concatenate([q_hi.astype(F32), q_lo.astype(F32)], axis=0).astype(BF16)
    per_blk = MOBA_BLOCK // PAGE_SIZE
    scale = A_HEAD_DIM ** -0.5
    for r in range(SAMPLE_PAGES_PER_STEP):
        k_hi, k_lo = _split_bf16(pages[r][...])
        s2 = _dot(q_both, k_hi)
        s_ref[g * SAMPLE_PAGES_PER_STEP + r] = s2[0:A_HEADS] + s2[A_HEADS:] + _dot(q_hi, k_lo)

    @pl.when(g == n_steps - 1)
    def _():
        jidx = lax.broadcasted_iota(jnp.int32, (A_HEADS, 128), 1)
        gt = jnp.full((A_HEADS, 128), -jnp.inf, F32)
        for j in range(n_blocks):
            tot = s_ref[per_blk * j]
            for r in range(1, per_blk):
                tot = tot + s_ref[per_blk * j + r]
            gj = jnp.sum(tot, axis=-1, keepdims=True) * (1.0 / MOBA_BLOCK)
            gt = jnp.where(jidx == j, gj, gt)
        rank = jnp.zeros((A_HEADS, 128), F32)
        for jp in range(n_blocks):
            cj = gt[:, jp:jp + 1]
            beats = (cj > gt) | ((cj == gt) & (jidx > jp))
            rank = rank + beats.astype(F32)
        sel = ((rank < MOBA_TOPK) & (jnp.abs(gt) < jnp.inf)).astype(F32)
        sel_ref[0] = sel
        s_self = jnp.sum(qm * scale * kn_ref[...], axis=-1, keepdims=True)
        masked = jnp.float32(-1e30)
        n_pages = n_blocks * per_blk
        m = s_self
        for pg in range(n_pages):
            keep = sel[:, pg // per_blk:pg // per_blk + 1] > 0.0
            m = jnp.maximum(m, jnp.max(jnp.where(keep, s_ref[pg] * scale, masked), axis=-1, keepdims=True))
        e_self = jnp.exp(s_self - m)
        l = e_self
        for pg in range(n_pages):
            keep = sel[:, pg // per_blk:pg // per_blk + 1] > 0.0
            e = jnp.exp(jnp.where(keep, s_ref[pg] * scale, masked) - m)
            prob_ref[0, pg] = e
            l = l + jnp.sum(e, axis=-1, keepdims=True)
        inv = 1.0 / l
        for pg in range(n_pages):
            prob_ref[0, pg] = prob_ref[0, pg] * inv
        pself_ref[0] = jnp.broadcast_to(e_self * inv, (A_HEADS, 128))


def _moba_sample_scores(page_table, p3, cache_k, *, layer):
    db, n_pages = page_table.shape
    n_steps = n_pages // SAMPLE_PAGES_PER_STEP
    n_blocks = n_pages * PAGE_SIZE // MOBA_BLOCK
    row = lambda g_: pl.BlockSpec((None, 1, GROUP), lambda b, g, pt: (b, 0, g_))
    grid_spec = pltpu.PrefetchScalarGridSpec(
        num_scalar_prefetch=1,
        grid=(db, n_steps),
        in_specs=[row(P_Q), row(P_K)] + _page_specs(layer),
        out_specs=[pl.BlockSpec((1, n_pages, A_HEADS, PAGE_SIZE), lambda b, g, pt: (b, 0, 0, 0)),
                   pl.BlockSpec((1, A_HEADS, 128), lambda b, g, pt: (b, 0, 0)),
                   pl.BlockSpec((1, A_HEADS, 128), lambda b, g, pt: (b, 0, 0))],
        scratch_shapes=[pltpu.VMEM((n_pages, A_HEADS, PAGE_SIZE), F32)],
    )
    return pl.pallas_call(
        functools.partial(_moba_scores_body, n_steps=n_steps, n_blocks=n_blocks),
        grid_spec=grid_spec,
        out_shape=[jax.ShapeDtypeStruct((db, n_pages, A_HEADS, PAGE_SIZE), F32),
                   jax.ShapeDtypeStruct((db, A_HEADS, 128), F32),
                   jax.ShapeDtypeStruct((db, A_HEADS, 128), F32)],
        compiler_params=_params("parallel", "arbitrary"),
        name="moba_sample_scores",
    )(page_table, p3, p3, *([cache_k] * SAMPLE_PAGES_PER_STEP))


def _selected_pages(page_table, sel):
    db, n_pages = page_table.shape
    per_blk = MOBA_BLOCK // PAGE_SIZE
    n_steps = n_pages // SAMPLE_PAGES_PER_STEP
    used = jnp.repeat(jnp.max(sel, axis=1)[:, :n_pages // per_blk] > 0.0, per_blk, axis=1)
    pages = page_table.reshape(db * n_steps, SAMPLE_PAGES_PER_STEP)
    used = used.reshape(db * n_steps, SAMPLE_PAGES_PER_STEP)
    order = jnp.arange(db * n_steps, dtype=jnp.int32)[:, None]
    last_used = lax.cummax(jnp.where(used, order, 0), axis=0)
    schedule = jnp.take_along_axis(pages, last_used, axis=0).reshape(db, n_pages)
    return schedule, used.reshape(db, n_pages).astype(jnp.int32)


def _moba_out_body(pt_ref, used_ref, prob_ref, pself_ref, vn_ref, *rest, n_steps):
    pages = rest[:SAMPLE_PAGES_PER_STEP]
    o_ref, acc_ref = rest[SAMPLE_PAGES_PER_STEP:]
    b = pl.program_id(0)
    g = pl.program_id(1)

    @pl.when(g == 0)
    def _():
        acc_ref[...] = jnp.zeros_like(acc_ref)

    for r in range(SAMPLE_PAGES_PER_STEP):
        pg = g * SAMPLE_PAGES_PER_STEP + r

        @pl.when(used_ref[b, pg] > 0)
        def _(r=r, pg=pg):
            pr = prob_ref[0, pg]
            acc_ref[...] += _dot_nt(pr.astype(BF16), pages[r][...].astype(BF16))

    @pl.when(g == n_steps - 1)
    def _():
        full = acc_ref[...] + pself_ref[0][:, 0:1] * vn_ref[...]
        lane = lax.broadcasted_iota(jnp.int32, (A_HEADS, GROUP), 1)
        hrow = lax.broadcasted_iota(jnp.int32, (A_HEADS, GROUP), 0)
        diag = jnp.where(lax.shift_right_logical(lane, 6) == hrow, full, 0.0)
        o_ref[...] = jnp.sum(diag, axis=0, keepdims=True)


def _moba_sample_out(schedule, probs, p_self, p3, cache_v, *, layer):
    page_table, used = schedule
    db, n_pages = page_table.shape
    n_steps = n_pages // SAMPLE_PAGES_PER_STEP
    grid_spec = pltpu.PrefetchScalarGridSpec(
        num_scalar_prefetch=2,
        grid=(db, n_steps),
        in_specs=[pl.BlockSpec((1, n_pages, A_HEADS, PAGE_SIZE), lambda b, g, *_: (b, 0, 0, 0)),
                  pl.BlockSpec((1, A_HEADS, 128), lambda b, g, *_: (b, 0, 0)),
                  pl.BlockSpec((None, 1, GROUP), lambda b, g, *_: (b, 0, P_V))] + _page_specs(layer),
        out_specs=pl.BlockSpec((None, 1, GROUP), lambda b, g, *_: (b, 0, 0)),
        scratch_shapes=[pltpu.VMEM((A_HEADS, GROUP), F32)],
    )
    return pl.pallas_call(
        functools.partial(_moba_out_body, n_steps=n_steps),
        grid_spec=grid_spec,
        out_shape=jax.ShapeDtypeStruct((db, 1, GROUP), F32),
        compiler_params=_params("parallel", "arbitrary"),
        name="moba_sample_out",
    )(page_table, used, probs, p_self, p3, *([cache_v] * SAMPLE_PAGES_PER_STEP)).reshape(db, GROUP)


def _col_bcast(row):
    return jnp.broadcast_to(row, (128, 128)).T


STEP_SEQS = 8


def _hgrn_step_body(q_ref, lf_ref, kk_ref, v_ref, sg_ref, gain_ref, s_ref, o_ref, so_ref):
    for s in range(q_ref.shape[0]):
        for h in range(HG_HEADS):
            sl = slice(HG_D * h, HG_D * (h + 1))
            f_col = _col_bcast(jnp.exp(lf_ref[s, :, sl]))
            k_col = _col_bcast(kk_ref[s, :, sl])
            q_col = _col_bcast(q_ref[s, :, sl])
            s_new = f_col * s_ref[s, h] + k_col * v_ref[s, :, sl]
            so_ref[s, h] = s_new
            o = jnp.sum(q_col * s_new, axis=0, keepdims=True)
            ms = jnp.mean(o * o, axis=-1, keepdims=True)
            o_ref[s, :, sl] = o * lax.rsqrt(ms + RMS_EPS) * gain_ref[...] * sg_ref[s, :, sl]


def _hgrn_step(p3, h3, gain, state, *, layer):
    db = p3.shape[0]
    ns = math.gcd(db, STEP_SEQS)
    col = lambda g_: pl.BlockSpec((ns, 1, GROUP), lambda b: (b, 0, g_))
    st_shape = (ns, HG_HEADS, HG_D, HG_D)
    o, s = pl.pallas_call(
        _hgrn_step_body,
        grid=(db // ns,),
        in_specs=[col(P_HQ), col(P_LOGF), col(P_KK), col(H_HV), col(P_HG), _const_spec((1, HG_D)),
                  pl.BlockSpec((None,) + st_shape, lambda b: (layer, b, 0, 0, 0))],
        out_specs=[pl.BlockSpec((ns, 1, GROUP), lambda b: (b, 0, 0)),
                   pl.BlockSpec(st_shape, lambda b: (b, 0, 0, 0))],
        out_shape=[jax.ShapeDtypeStruct((db, 1, GROUP), F32),
                   jax.ShapeDtypeStruct((db, HG_HEADS, HG_D, HG_D), F32)],
        compiler_params=_params("parallel"),
        name="hgrn_step",
    )(p3, p3, p3, h3, p3, gain, state)
    return o.reshape(db, GROUP), s


def _ret_step_body(q_ref, k_ref, v_ref, sg_ref, s_ref, o_ref, so_ref):
    gammas = np.exp(np.log1p(-np.exp2(-5.0 - np.arange(RET_HEADS, dtype=np.float32))).astype(np.float32))
    for s in range(q_ref.shape[0]):
        outs = []
        for p2 in range(RET_HEADS // 2):
            cols = slice(128 * p2, 128 * p2 + 128)
            k_col = _col_bcast(k_ref[s, :, cols])
            q_col = _col_bcast(q_ref[s, :, cols])
            vrow = v_ref[s, :, cols]
            for hh in range(2):
                h = 2 * p2 + hh
                rs = slice(RET_D * hh, RET_D * (hh + 1))
                s_new = float(gammas[h]) * s_ref[s, h] + k_col[rs, 0:RET_D] * vrow[:, rs]
                so_ref[s, h] = s_new
                o = jnp.sum(q_col[rs, 0:RET_D] * s_new, axis=0, keepdims=True)
                ms = jnp.mean(o * o, axis=-1, keepdims=True)
                outs.append(o * lax.rsqrt(ms + RMS_EPS))
        o_ref[s] = jnp.concatenate(outs, axis=1) * sg_ref[s]


def _ret_step(p3, h3, state, *, layer):
    db = p3.shape[0]
    ns = math.gcd(db, STEP_SEQS)
    col = lambda g_: pl.BlockSpec((ns, 1, GROUP), lambda b: (b, 0, g_))
    st_shape = (ns, RET_HEADS, RET_D, RET_D)
    o, s = pl.pallas_call(
        _ret_step_body,
        grid=(db // ns,),
        in_specs=[col(P_RQ), col(P_RK), col(H_RV), col(P_RG),
                  pl.BlockSpec((None,) + st_shape, lambda b: (layer, b, 0, 0, 0))],
        out_specs=[pl.BlockSpec((ns, 1, GROUP), lambda b: (b, 0, 0)),
                   pl.BlockSpec(st_shape, lambda b: (b, 0, 0, 0))],
        out_shape=[jax.ShapeDtypeStruct((db, 1, GROUP), F32),
                   jax.ShapeDtypeStruct((db, RET_HEADS, RET_D, RET_D), F32)],
        compiler_params=_params("parallel"),
        name="ret_step",
    )(p3, p3, h3, p3, state)
    return o.reshape(db, GROUP), s


def _rotary_tables(pos):
    f32 = np.float32
    theta = (f32(1.0) / (f32(RET_ROPE_BASE) ** np.linspace(0.0, 1.0, RET_D // 2, dtype=f32))).astype(f32)
    ang = (pos.astype(f32)[:, None] * np.repeat(theta, 2)[None, :]).astype(f32)
    ang = np.tile(ang, (1, 128 // RET_D))
    cos, sin = np.cos(ang).astype(f32), np.sin(ang).astype(f32)
    even = (np.arange(128) % 2 == 0)[None, :]
    return tuple(jnp.asarray(a) for a in (cos, np.where(even, -sin, f32(0)), np.where(even, f32(0), sin)))


def _head_block_diag():
    r = np.arange(GROUP)
    return jnp.asarray((r[:, None] // A_HEAD_DIM) == (r[None, :] // A_HEAD_DIM), BF16)


def _layer_weights(l, lb_all, ffn1_norm, ffn1_w_in, ffn1_w_out, mix_norm, w_in, moba_q_gain, moba_k_gain,
                   hgrn_o_gain, w_branch_a, w_branch_b, w_branch_c, w_out, ffn2_norm, ffn2_w_in, ffn2_w_out):
    row = lambda a: a.reshape(1, -1).astype(F32)
    return dict(
        f1g=row(ffn1_norm[l]), f1a=ffn1_w_in[l].astype(BF16), f1b=ffn1_w_out[l].astype(BF16),
        mg=row(mix_norm[l]), w_in=w_in[l].astype(BF16),
        qg=row(jnp.tile(moba_q_gain[l], A_HEADS)), kg=row(jnp.tile(moba_k_gain[l], A_HEADS)),
        lb=row(lb_all[l]), hg=row(hgrn_o_gain[l]),
        wa=w_branch_a[l].astype(BF16), wb=w_branch_b[l].astype(BF16), wc=w_branch_c[l].astype(BF16),
        wo=w_out[l].astype(BF16),
        f2g=row(ffn2_norm[l]), f2a=ffn2_w_in[l].astype(BF16), f2b=ffn2_w_out[l].astype(BF16),
    )


def _prompt_layer(x, w, tabs, bd, kv_t, *, batch, seq):
    x = _ffn(x, w["f1g"], w["f1a"], w["f1b"], tm=512)
    p, hh, k_t, v_t = _proj(x, w["mg"], w["w_in"], w["qg"], w["kg"], w["lb"], *tabs, bd, tm=256, kv_t=kv_t)
    oa = _moba_prompt(p, batch=batch, seq=seq)
    s0t = jnp.zeros((batch, HG_HEADS, HG_D, HG_D), F32)
    ob, s_hg = _hgrn_prompt(p, hh, w["hg"], s0t, batch=batch, seq=seq, tc=512)
    oc, s_ret = _ret_prompt(p, hh, batch=batch, seq=seq)
    x = _merge(oa, ob, oc, hh, x, w["wa"], w["wb"], w["wc"], w["wo"], tm=512)
    x = _ffn(x, w["f2g"], w["f2a"], w["f2b"], tm=512)
    return x, k_t, v_t, s_hg, _unpair_ret_state(s_ret)


def _token_major(kv_t):
    d, b, _, t = kv_t.shape
    return jnp.transpose(kv_t.reshape(d, b, A_HEADS, A_HEAD_DIM, t), (0, 1, 4, 2, 3))


def _sample_layer(x, w, tabs, bd, l, page_table, cache_k, cache_v, state_hgrn, state_ret):
    db = x.shape[0]
    x = _ffn(x, w["f1g"], w["f1a"], w["f1b"], tm=db)
    p, hh = _proj(x, w["mg"], w["w_in"], w["qg"], w["kg"], w["lb"], *tabs, bd, tm=db)
    p3 = p.reshape(db, 1, P_GROUPS * GROUP)
    h3 = hh.reshape(db, 1, H_GROUPS * GROUP)
    probs, p_self, sel = _moba_sample_scores(page_table, p3, cache_k, layer=l)
    oa = _moba_sample_out(_selected_pages(page_table, sel), probs, p_self, p3, cache_v, layer=l)
    ob, s_hg = _hgrn_step(p3, h3, w["hg"], state_hgrn, layer=l)
    oc, s_ret = _ret_step(p3, h3, state_ret, layer=l)
    x = _merge(oa, ob, oc, hh, x, w["wa"], w["wb"], w["wc"], w["wo"], tm=db)
    x = _ffn(x, w["f2g"], w["f2a"], w["f2b"], tm=db)
    k_new = p[:, P_K * GROUP:(P_K + 1) * GROUP].reshape(db, 1, A_HEADS, A_HEAD_DIM)
    v_new = p[:, P_V * GROUP:(P_V + 1) * GROUP].reshape(db, 1, A_HEADS, A_HEAD_DIM)
    return x, k_new, v_new, s_hg, s_ret


def kernel(x_prompt, x_sample, cache_k, cache_v, state_hgrn, state_ret, page_table, ffn1_norm, ffn1_w_in, ffn1_w_out, mix_norm, w_in, moba_q_gain, moba_k_gain, hgrn_lb, hgrn_o_gain, w_branch_a, w_branch_b, w_branch_c, w_out, ffn2_norm, ffn2_w_in, ffn2_w_out):
    lb_cum = jnp.cumsum(jax.nn.softmax(hgrn_lb.astype(F32), axis=0), axis=0)
    lb_all = lb_cum - lb_cum[:1]
    bp, seq, _ = x_prompt.shape
    db, dec_seq, _ = x_sample.shape
    assert dec_seq == 1
    n_pages = page_table.shape[1]
    past_len = n_pages * PAGE_SIZE
    assert past_len % MOBA_BLOCK == 0 and n_pages % SAMPLE_PAGES_PER_STEP == 0
    tabs_p = _rotary_tables(np.arange(seq))
    tabs_s = _rotary_tables(np.full((db,), past_len))
    bd = _head_block_diag()
    ck = _pages_feature_major(cache_k)
    cv = _pages_feature_major(cache_v)
    xp = x_prompt.reshape(bp * seq, D_MODEL)
    xs = x_sample.reshape(db, D_MODEL)
    cols = [[] for _ in range(6)]
    kp = jnp.zeros((DEPTH, bp, GROUP, seq), F32)
    vp = jnp.zeros((DEPTH, bp, GROUP, seq), F32)
    for l in range(DEPTH):
        w = _layer_weights(l, lb_all, ffn1_norm, ffn1_w_in, ffn1_w_out, mix_norm, w_in, moba_q_gain, moba_k_gain, hgrn_o_gain, w_branch_a, w_branch_b, w_branch_c, w_out, ffn2_norm, ffn2_w_in, ffn2_w_out)
        xp, kp, vp, hp, rp = _prompt_layer(xp, w, tabs_p, bd, (kp, vp, l), batch=bp, seq=seq)
        xs, ks, vs, hs, rs = _sample_layer(xs, w, tabs_s, bd, l, page_table, ck, cv, state_hgrn, state_ret)
        for c, a in zip(cols, (ks, vs, hp, hs, rp, rs)):
            c.append(a)
    outs = [_token_major(kp), _token_major(vp)] + [jnp.stack(c) for c in cols]
    return (xp.reshape(bp, seq, D_MODEL), xs.reshape(db, 1, D_MODEL)) + tuple(outs)
```

```python
import functools
import math

import numpy as np
import jax
import jax.numpy as jnp
from jax import lax
from jax.experimental import pallas as pl
from jax.experimental.pallas import tpu as pltpu

F32 = jnp.float32
BF16 = jnp.bfloat16

D_MODEL = 1024
D_FF = 2816
DEPTH = 4
RMS_EPS = 1e-6
A_HEADS = 8
A_HEAD_DIM = 64
MOBA_BLOCK = 256
MOBA_TOPK = 3
MOBA_ONES_ROWS = 16
HG_HEADS = 4
HG_D = 128
RET_HEADS = 8
RET_D = 64
RET_ROPE_BASE = 10000.0
PAGE_SIZE = 128
GROUP = 512

P_Q, P_K, P_V, P_HQ, P_LOGF, P_KK, P_HG, P_RQ, P_RK, P_RG = range(10)
P_GROUPS = 10
H_HV, H_RV, H_GATES = range(3)
H_GROUPS = 8

HG_CHUNK = 64
HG_SUBLANE_LEVEL = 8
RET_CHUNK = 256

VMEM_LIMIT = 56 * 1024 * 1024


def _dot(a, b):
    return jnp.dot(a, b, preferred_element_type=F32)


def _dot_nt(a, b):
    return lax.dot_general(a, b, (((1,), (1,)), ((), ())), preferred_element_type=F32)


def _split_bf16(a):
    hi = a.astype(BF16)
    lo = (a - hi.astype(F32)).astype(BF16)
    return hi, lo


def _params(*sem):
    return pltpu.CompilerParams(dimension_semantics=sem, vmem_limit_bytes=VMEM_LIMIT)


def _const_spec(shape):
    nd = len(shape)
    return pl.BlockSpec(shape, lambda *_: (0,) * nd)


def _ffn_body(x_ref, g_ref, w1_ref, w2_ref, o_ref, acc_ref, *, ff_chunk):
    x = x_ref[...]
    ms = jnp.mean(x * x, axis=-1, keepdims=True)
    h = (x * lax.rsqrt(ms + RMS_EPS) * g_ref[...]).astype(BF16)
    for c in range(D_FF // ff_chunk):
        lo = c * ff_chunk
        gate = _dot(h, w1_ref[:, lo:lo + ff_chunk])
        up = _dot(h, w1_ref[:, D_FF + lo:D_FF + lo + ff_chunk])
        act = (gate * jax.nn.sigmoid(gate) * up).astype(BF16)
        part = _dot(act, w2_ref[lo:lo + ff_chunk, :])
        if c == 0:
            acc_ref[...] = part
        else:
            acc_ref[...] += part
    o_ref[...] = x + 0.5 * acc_ref[...]


def _ffn(x, gain, w1, w2, *, tm):
    n = x.shape[0]
    return pl.pallas_call(
        functools.partial(_ffn_body, ff_chunk=256),
        grid=(n // tm,),
        in_specs=[
            pl.BlockSpec((tm, D_MODEL), lambda i: (i, 0)),
            _const_spec((1, D_MODEL)),
            pl.BlockSpec((D_MODEL, 2 * D_FF), lambda i: (0, 0), pipeline_mode=pl.Buffered(1)),
            pl.BlockSpec((D_FF, D_MODEL), lambda i: (0, 0), pipeline_mode=pl.Buffered(1)),
        ],
        out_specs=pl.BlockSpec((tm, D_MODEL), lambda i: (i, 0)),
        out_shape=jax.ShapeDtypeStruct((n, D_MODEL), F32),
        scratch_shapes=[pltpu.VMEM((tm, D_MODEL), F32)],
        compiler_params=_params("parallel"),
        name="ffn",
    )(x, gain, w1, w2)


def _proj_body(x_ref, g_ref, w_ref, qg_ref, kg_ref, lb_ref, cos_ref, sa_ref, sb_ref, bd_ref, *rest,
               n_passthrough):
    o_ref, o16_ref, *kv_t_refs = rest[n_passthrough:]
    x = x_ref[...]
    ms = jnp.mean(x * x, axis=-1, keepdims=True)
    h = (x * lax.rsqrt(ms + RMS_EPS) * g_ref[...]).astype(BF16)

    def mm(wg):
        return _dot(h, w_ref[:, wg * GROUP:(wg + 1) * GROUP])

    def put(g, val):
        o_ref[:, g * GROUP:(g + 1) * GROUP] = val

    def put16(g, val):
        o16_ref[:, g * GROUP:(g + 1) * GROUP] = val.astype(BF16)

    def headnorm(a, gain_row):
        ss = _dot((a * a).astype(BF16), bd_ref[...])
        return a * lax.rsqrt(ss * (1.0 / A_HEAD_DIM) + RMS_EPS) * gain_row

    def rotary(g, a, scale):
        cos, sa, sb = cos_ref[...], sa_ref[...], sb_ref[...]
        for s in range(GROUP // 128):
            xs = a[:, 128 * s:128 * s + 128]
            nxt = pltpu.roll(xs, 127, 1)
            prv = pltpu.roll(xs, 1, 1)
            r = xs * cos + nxt * sa + prv * sb
            o_ref[:, g * GROUP + 128 * s:g * GROUP + 128 * s + 128] = r * scale if scale != 1.0 else r

    put(P_Q, headnorm(mm(0), qg_ref[...]))
    k_n = headnorm(mm(1), kg_ref[...])
    v = mm(2)
    put(P_K, k_n)
    put(P_V, v)
    if kv_t_refs:
        kv_t_refs[0][...] = k_n.T
        kv_t_refs[1][...] = v.T
    put(P_HQ, mm(3))

    fr = mm(4)
    lb = lb_ref[...]
    log_sig = jnp.minimum(fr, 0.0) - jnp.log1p(jnp.exp(-jnp.abs(fr)))
    a = jnp.log(lb)
    c = jnp.log1p(-lb) + log_sig
    put(P_LOGF, jnp.maximum(a, c) + jnp.log1p(jnp.exp(-jnp.abs(a - c))))
    put(P_KK, (1.0 - lb) * jax.nn.sigmoid(-fr))

    put16(H_HV, mm(5))
    hg = mm(6)
    put(P_HG, hg * jax.nn.sigmoid(hg))
    rotary(P_RQ, mm(7), 1.0)
    rotary(P_RK, mm(8), RET_D ** -0.5)
    put16(H_RV, mm(9))
    rg = mm(10)
    put(P_RG, rg * jax.nn.sigmoid(rg))
    for t in range(H_GROUPS - H_GATES):
        put16(H_GATES + t, jax.nn.sigmoid(mm(11 + t)))


def _proj(x, gain, w, qg, kg, lb, cos, sa, sb, bd, *, tm, kv_t=None):
    n = x.shape[0]
    n_tab = cos.shape[0] // tm
    tab = pl.BlockSpec((tm, 128), lambda i: (i % n_tab, 0))
    in_specs = [
        pl.BlockSpec((tm, D_MODEL), lambda i: (i, 0)),
        _const_spec((1, D_MODEL)),
        pl.BlockSpec(w.shape, lambda i: (0, 0), pipeline_mode=pl.Buffered(1)),
        _const_spec((1, GROUP)),
        _const_spec((1, GROUP)),
        _const_spec((1, GROUP)),
        tab, tab, tab,
        _const_spec((GROUP, GROUP)),
    ]
    args = [x, gain, w, qg, kg, lb, cos, sa, sb, bd]
    out_specs = [pl.BlockSpec((tm, P_GROUPS * GROUP), lambda i: (i, 0)),
                 pl.BlockSpec((tm, H_GROUPS * GROUP), lambda i: (i, 0))]
    out_shape = [jax.ShapeDtypeStruct((n, P_GROUPS * GROUP), F32),
                 jax.ShapeDtypeStruct((n, H_GROUPS * GROUP), BF16)]
    aliases = {}
    if kv_t is not None:
        k_all, v_all, layer = kv_t
        nt = k_all.shape[3] // tm
        for a in (k_all, v_all):
            aliases[len(args)] = len(out_shape)
            in_specs.append(pl.BlockSpec(memory_space=pl.ANY))
            args.append(a)
            out_specs.append(pl.BlockSpec((None, None, GROUP, tm), lambda i: (layer, i // nt, 0, i % nt)))
            out_shape.append(jax.ShapeDtypeStruct(a.shape, a.dtype))
    return pl.pallas_call(
        functools.partial(_proj_body, n_passthrough=len(aliases)),
        grid=(n // tm,),
        in_specs=in_specs,
        out_specs=out_specs,
        out_shape=out_shape,
        input_output_aliases=aliases,
        compiler_params=_params("parallel"),
        name="proj",
    )(*args)


def _merge_body(oa_ref, ob_ref, oc_ref, ga_ref, gb_ref, gc_ref, x_ref, wa_ref, wb_ref, wc_ref, wo_ref, o_ref):
    m = ga_ref[...] * _dot(oa_ref[...].astype(BF16), wa_ref[...])
    m = m + gb_ref[...] * _dot(ob_ref[...].astype(BF16), wb_ref[...])
    m = m + gc_ref[...] * _dot(oc_ref[...].astype(BF16), wc_ref[...])
    o_ref[...] = x_ref[...] + _dot(m.astype(BF16), wo_ref[...])


def _merge(oa, ob, oc, hh, x, wa, wb, wc, wo, *, tm):
    n = x.shape[0]
    row = lambda i: (i, 0)
    g0 = H_GATES // 2
    return pl.pallas_call(
        _merge_body,
        grid=(n // tm,),
        in_specs=[
            pl.BlockSpec((tm, GROUP), row),
            pl.BlockSpec((tm, GROUP), row),
            pl.BlockSpec((tm, GROUP), row),
            pl.BlockSpec((tm, D_MODEL), lambda i: (i, g0)),
            pl.BlockSpec((tm, D_MODEL), lambda i: (i, g0 + 1)),
            pl.BlockSpec((tm, D_MODEL), lambda i: (i, g0 + 2)),
            pl.BlockSpec((tm, D_MODEL), row),
            _const_spec((GROUP, D_MODEL)),
            _const_spec((GROUP, D_MODEL)),
            _const_spec((GROUP, D_MODEL)),
            _const_spec((D_MODEL, D_MODEL)),
        ],
        out_specs=pl.BlockSpec((tm, D_MODEL), row),
        out_shape=jax.ShapeDtypeStruct((n, D_MODEL), F32),
        compiler_params=_params("parallel"),
        name="merge",
    )(oa, ob, oc, hh, hh, hh, x, wa, wb, wc, wo)


def _moba_body(q_ref, k_ref, v_ref, o_ref, k16, vt16, kmt, bias_ref, acc_ref, qmt_ref, m_ref, s_ref, *, nb):
    i = pl.program_id(1)
    blk = MOBA_BLOCK
    dh = A_HEAD_DIM
    masked = jnp.float32(-1e30)

    @pl.when(i == 0)
    def _():
        lane = lax.broadcasted_iota(jnp.int32, (A_HEADS, GROUP), 1)
        hrow = lax.broadcasted_iota(jnp.int32, (A_HEADS, GROUP), 0)
        headmask = (lax.shift_right_logical(lane, 6) == hrow).astype(F32)
        lane128 = lax.broadcasted_iota(jnp.int32, (blk, 128), 1)
        ones = jnp.ones((MOBA_ONES_ROWS, blk), BF16)

        def init(jj, c):
            r = pl.multiple_of(jj * blk, blk)
            kb = k_ref[pl.ds(r, blk), :]
            km = jnp.sum(kb, axis=0, keepdims=True) * (1.0 / blk)
            kmt[pl.ds(pl.multiple_of(jj * A_HEADS, A_HEADS), A_HEADS), :] = km * headmask
            indicator = jnp.where(lane128 == dh + jj, 1.0, 0.0)
            vt = v_ref[pl.ds(r, blk), :].T
            for h in range(A_HEADS):
                slab = kb[:, 128 * (h // 2):128 * (h // 2) + 128]
                if h % 2:
                    slab = pltpu.roll(slab, dh, 1)
                k16[jj, h] = jnp.where(lane128 < dh, slab, indicator).astype(BF16)
                vt16[jj, h, 0:dh, :] = vt[dh * h:dh * (h + 1), :].astype(BF16)
                vt16[jj, h, dh:dh + MOBA_ONES_ROWS, :] = ones
            return c

        lax.fori_loop(0, nb, init, 0)

    q = q_ref[...]
    kh, kl = _split_bf16(kmt[...])
    qh, ql = _split_bf16(q)
    gt = _dot_nt(kh, qh) + _dot_nt(kh, ql) + _dot_nt(kl, qh)
    neg_inf = jnp.float32(-jnp.inf)
    g = [jnp.where(j < i, gt[A_HEADS * j:A_HEADS * (j + 1), :], neg_inf) for j in range(nb)]
    for j in range(nb):
        rank = jnp.zeros((A_HEADS, blk), F32)
        for jp in range(nb):
            if jp == j:
                continue
            beats = (g[jp] >= g[j]) if jp < j else (g[jp] > g[j])
            rank = rank + beats.astype(F32)
        sel = (rank < MOBA_TOPK) & (jnp.abs(g[j]) < jnp.inf)
        bias = jnp.where(sel | (j == i), 0.0, masked)
        for h in range(A_HEADS):
            bias_ref[h, j:j + 1, :] = bias[h:h + 1, :]

    qt = (q * (dh ** -0.5 * math.log2(math.e))).T
    pad = jnp.zeros((128 - dh - nb, blk), F32)
    for h in range(A_HEADS):
        qmt_ref[h] = jnp.concatenate([qt[dh * h:dh * (h + 1), :], bias_ref[h], pad], axis=0).astype(BF16)

    m_ref[...] = jnp.full(m_ref.shape, masked, F32)
    acc_ref[...] = jnp.zeros_like(acc_ref)

    def block_steps(js, causal=None):
        for t, j in enumerate(js):
            for h in range(A_HEADS):
                s_ref[t, h] = _dot(k16[j, h], qmt_ref[h])
        for t, j in enumerate(js):
            for h in range(A_HEADS):
                st = s_ref[t, h]
                if causal is not None:
                    st = jnp.where(causal, st, masked)
                m_old = m_ref[h:h + 1, :]
                m_new = jnp.maximum(m_old, jnp.max(st, axis=0, keepdims=True))
                alpha = jnp.exp2(m_old - m_new)
                pt = jnp.exp2(st - m_new).astype(BF16)
                m_ref[h:h + 1, :] = m_new
                acc_ref[h] = alpha * acc_ref[h] + _dot(vt16[j, h], pt)

    kidx = lax.broadcasted_iota(jnp.int32, (blk, blk), 0)
    qidx = lax.broadcasted_iota(jnp.int32, (blk, blk), 1)
    block_steps([i], kidx <= qidx)

    @pl.when(i % 2 == 1)
    def _():
        block_steps([0])

    def body(t, c):
        j = i % 2 + 2 * t
        block_steps([j, j + 1])
        return c

    lax.fori_loop(0, i // 2, body, 0)

    per = blk // dh
    for h in range(A_HEADS):
        s_ref[0, h // per, dh * (h % per):dh * (h % per + 1), :] = acc_ref[h, 0:dh, :] / acc_ref[h, dh:dh + 1, :]
    o_ref[...] = jnp.concatenate([s_ref[0, t] for t in range(A_HEADS // per)], axis=0).T.astype(BF16)


def _moba_prompt(p, *, batch, seq):
    nb = seq // MOBA_BLOCK
    n = batch * seq
    return pl.pallas_call(
        functools.partial(_moba_body, nb=nb),
        grid=(batch, nb),
        in_specs=[
            pl.BlockSpec((MOBA_BLOCK, GROUP), lambda b, i: (b * nb + i, P_Q)),
            pl.BlockSpec((seq, GROUP), lambda b, i: (b, P_K), pipeline_mode=pl.Buffered(1)),
            pl.BlockSpec((seq, GROUP), lambda b, i: (b, P_V), pipeline_mode=pl.Buffered(1)),
        ],
        out_specs=pl.BlockSpec((MOBA_BLOCK, GROUP), lambda b, i: (b * nb + i, 0)),
        out_shape=jax.ShapeDtypeStruct((n, GROUP), BF16),
        scratch_shapes=[
            pltpu.VMEM((nb, A_HEADS, MOBA_BLOCK, 128), BF16),
            pltpu.VMEM((nb, A_HEADS, A_HEAD_DIM + MOBA_ONES_ROWS, MOBA_BLOCK), BF16),
            pltpu.VMEM((nb * A_HEADS, GROUP), F32),
            pltpu.VMEM((A_HEADS, nb, MOBA_BLOCK), F32),
            pltpu.VMEM((A_HEADS, A_HEAD_DIM + MOBA_ONES_ROWS, MOBA_BLOCK), F32),
            pltpu.VMEM((A_HEADS, 128, MOBA_BLOCK), BF16),
            pltpu.VMEM((A_HEADS, MOBA_BLOCK), F32),
            pltpu.VMEM((2, A_HEADS, MOBA_BLOCK, MOBA_BLOCK), F32),
        ],
        compiler_params=_params("parallel", "arbitrary"),
        name="moba_prompt",
    )(p, p, p)


def _hgrn_tables():
    c = HG_CHUNK
    t = np.arange(c)[:, None]
    u = np.arange(c)[None, :]
    mats = [(u <= t)]
    masks = [np.eye(c, dtype=bool)]
    m = c // 2
    while m >= 1:
        b0 = (t // (2 * m)) * (2 * m)
        right = t >= b0 + m
        if m < HG_SUBLANE_LEVEL:
            mats.append(np.where(right, (u >= b0 + m) & (u <= t), (u > t) & (u <= b0 + m - 1)))
        s = u
        masks.append(((s // (2 * m)) == (t // (2 * m))) & right & (s < b0 + m))
        m //= 2
    return (np.concatenate(mats, 0).astype(np.float32), np.stack(masks).astype(np.float32))


def _hgrn_body(q_ref, lf_ref, kk_ref, v_ref, sg_ref, gain_ref, mall_ref, masks_ref, s0_ref,
               o_ref, so_ref, st_ref, ob_ref, qd_ref, kd_ref, dl_ref, vt_ref, ex_ref, a_ref, *, n_chunks, nt):
    t = pl.program_id(1)
    c = HG_CHUNK

    @pl.when(t == 0)
    def _():
        st_ref[...] = s0_ref[0]

    row = lax.broadcasted_iota(jnp.int32, (c, 1), 0)

    n_lv = masks_ref.shape[0] - 1

    def decays(ci):
        rows = slice(ci * c, (ci + 1) * c)
        hi, lo = _split_bf16(lf_ref[rows, :])
        mall = mall_ref[...]
        e_mm = _dot(mall, hi) + _dot(mall, lo)
        b = e_mm[0:c]
        b_last = b[c - 1:c, :]
        lv = 0
        m = c // 2
        while m >= HG_SUBLANE_LEVEL:
            ref = jnp.concatenate(
                [jnp.broadcast_to(b[s + m - 1:s + m, :], (2 * m, GROUP)) for s in range(0, c, 2 * m)], axis=0)
            d = b - ref
            ex_ref[ci % 2, lv] = jnp.exp(jnp.where((row & (2 * m - 1)) >= m, d, -d))
            lv += 1
            m //= 2
        for i in range(1, mall.shape[0] // c):
            ex_ref[ci % 2, lv] = jnp.exp(e_mm[i * c:(i + 1) * c])
            lv += 1
        qd_ref[ci] = (q_ref[rows, :] * jnp.exp(b)).astype(BF16)
        kd_ref[ci] = (kk_ref[rows, :] * jnp.exp(b_last - b)).astype(BF16)
        dl_ref[ci] = jnp.exp(b_last)

    def pair_weights(ci):
        rows = slice(ci * c, (ci + 1) * c)
        for h in range(HG_HEADS):
            sl = slice(HG_D * h, HG_D * (h + 1))
            qh, kh = q_ref[rows, sl], kk_ref[rows, sl]
            a = masks_ref[0] * _dot_nt(qh.astype(BF16), kh.astype(BF16))
            for lv in range(n_lv):
                e = ex_ref[ci % 2, lv, :, sl]
                a = a + masks_ref[lv + 1] * _dot_nt((qh * e).astype(BF16), (kh * e).astype(BF16))
            a_ref[ci % 2, h] = a.astype(BF16)

    def intra_out(ci):
        rows = slice(ci * c, (ci + 1) * c)
        for h in range(HG_HEADS):
            sl = slice(HG_D * h, HG_D * (h + 1))
            vh = v_ref[rows, sl]
            ob_ref[rows, sl] = _dot(a_ref[ci % 2, h], vh.astype(BF16))
            vt_ref[ci, h] = vh.astype(F32).T.astype(BF16)

    def carry(ci):
        rows = slice(ci * c, (ci + 1) * c)
        for h in range(HG_HEADS):
            sl = slice(HG_D * h, HG_D * (h + 1))
            s_t = st_ref[h]
            ob_ref[rows, sl] += _dot_nt(qd_ref[ci, :, sl], s_t.astype(BF16))
            st_ref[h] = dl_ref[ci, :, sl] * s_t + _dot(vt_ref[ci, h], kd_ref[ci, :, sl])

    decays(0)
    pair_weights(0)
    for ci in range(n_chunks):
        if ci + 1 < n_chunks:
            decays(ci + 1)
        intra_out(ci)
        carry(ci)
        if ci + 1 < n_chunks:
            pair_weights(ci + 1)

    for h in range(HG_HEADS):
        sl = slice(HG_D * h, HG_D * (h + 1))
        x = ob_ref[:, sl]
        ms = jnp.mean(x * x, axis=-1, keepdims=True)
        o_ref[:, sl] = (x * lax.rsqrt(ms + RMS_EPS) * gain_ref[...] * sg_ref[:, sl]).astype(BF16)

    @pl.when(t == nt - 1)
    def _():
        for h in range(HG_HEADS):
            so_ref[0, h] = st_ref[h].T


def _hgrn_prompt(p, hh, gain, s0t, *, batch, seq, tc):
    nt = seq // tc
    n = batch * seq
    mall, masks = _hgrn_tables()
    mall = jnp.asarray(mall, BF16)
    masks = jnp.asarray(masks, F32)
    col = lambda g: pl.BlockSpec((tc, GROUP), lambda b, t: (b * nt + t, g))
    st_spec = pl.BlockSpec((1, HG_HEADS, HG_D, HG_D), lambda b, t: (b, 0, 0, 0))
    return pl.pallas_call(
        functools.partial(_hgrn_body, n_chunks=tc // HG_CHUNK, nt=nt),
        grid=(batch, nt),
        in_specs=[col(P_HQ), col(P_LOGF), col(P_KK), col(H_HV), col(P_HG),
                  _const_spec((1, HG_D)), _const_spec(mall.shape), _const_spec(masks.shape), st_spec],
        out_specs=[pl.BlockSpec((tc, GROUP), lambda b, t: (b * nt + t, 0)), st_spec],
        out_shape=[jax.ShapeDtypeStruct((n, GROUP), BF16),
                   jax.ShapeDtypeStruct((batch, HG_HEADS, HG_D, HG_D), F32)],
        scratch_shapes=[pltpu.VMEM((HG_HEADS, HG_D, HG_D), F32), pltpu.VMEM((tc, GROUP), F32),
                        pltpu.VMEM((tc // HG_CHUNK, HG_CHUNK, GROUP), BF16),
                        pltpu.VMEM((tc // HG_CHUNK, HG_CHUNK, GROUP), BF16),
                        pltpu.VMEM((tc // HG_CHUNK, 1, GROUP), F32),
                        pltpu.VMEM((tc // HG_CHUNK, HG_HEADS, HG_D, HG_CHUNK), BF16),
                        pltpu.VMEM((2, masks.shape[0] - 1, HG_CHUNK, GROUP), F32),
                        pltpu.VMEM((2, HG_HEADS, HG_CHUNK, HG_CHUNK), BF16)],
        compiler_params=_params("parallel", "arbitrary"),
        name="hgrn_prompt",
    )(p, p, p, hh, p, gain, mall, masks, s0t)


def _ret_tables():
    c = RET_CHUNK
    f32 = np.float32
    lg = np.log1p(-np.exp2(-5.0 - np.arange(RET_HEADS, dtype=f32))).astype(f32)
    t = np.arange(c, dtype=f32)
    rel = t[:, None] - t[None, :]
    intra = np.where(rel[None] >= 0, np.exp(np.maximum(rel, 0.0)[None] * lg[:, None, None]), 0.0).astype(f32)
    lane_lg = np.repeat(lg, RET_D).reshape(RET_HEADS // 2, 1, 128)
    qdec = np.exp((t + 1.0)[None, :, None] * lane_lg).astype(f32)
    kdec = np.exp((c - 1.0 - t)[None, :, None] * lane_lg).astype(f32)
    cdec = np.exp(f32(c) * lane_lg).astype(f32)
    return tuple(jnp.asarray(a) for a in (intra, qdec, kdec, cdec))


def _ret_body(q_ref, k_ref, v_ref, sg_ref, intra_ref, qdec_ref, kdec_ref, cdec_ref, o_ref, so_ref, s_ref, a_ref,
              *, nt):
    t = pl.program_id(1)

    @pl.when(t == 0)
    def _():
        s_ref[...] = jnp.zeros_like(s_ref)

    c = RET_CHUNK
    lane = lax.broadcasted_iota(jnp.int32, (c, 128), 1)
    low = lane < RET_D
    r128 = lax.broadcasted_iota(jnp.int32, (128, 128), 0)
    c128 = lax.broadcasted_iota(jnp.int32, (128, 128), 1)
    same_head = (r128 < RET_D) == (c128 < RET_D)
    for p2 in range(RET_HEADS // 2):
        cols = slice(128 * p2, 128 * p2 + 128)
        qp, kb = q_ref[:, cols], k_ref[:, cols].astype(BF16)
        for hh in range(2):
            qm = jnp.where(low if hh == 0 else ~low, qp, 0.0).astype(BF16)
            a_ref[2 * p2 + hh] = (_dot_nt(qm, kb) * intra_ref[2 * p2 + hh]).astype(BF16)
    for p2 in range(RET_HEADS // 2):
        cols = slice(128 * p2, 128 * p2 + 128)
        qp, kp, vp = q_ref[:, cols], k_ref[:, cols], v_ref[:, cols]
        vb = vp.astype(BF16)
        s_bd = s_ref[p2]
        o = _dot((qp * qdec_ref[p2]).astype(BF16), s_bd.astype(BF16))
        for hh in range(2):
            in_head = low if hh == 0 else ~low
            o = o + jnp.where(in_head, _dot(a_ref[2 * p2 + hh], vb), 0.0)
        upd = _dot((kp * kdec_ref[p2]).T.astype(BF16), vb)
        s_ref[p2] = cdec_ref[p2] * s_bd + jnp.where(same_head, upd, 0.0)
        sq = o * o
        s_lo = jnp.sum(jnp.where(low, sq, 0.0), axis=-1, keepdims=True)
        s_hi = jnp.sum(jnp.where(low, 0.0, sq), axis=-1, keepdims=True)
        ms = jnp.where(low, s_lo, s_hi) * (1.0 / RET_D)
        o_ref[:, cols] = (o * lax.rsqrt(ms + RMS_EPS) * sg_ref[:, cols]).astype(BF16)

    @pl.when(t == nt - 1)
    def _():
        so_ref[0] = s_ref[...]


def _ret_prompt(p, hh, *, batch, seq):
    c = RET_CHUNK
    nt = seq // c
    n = batch * seq
    intra, qdec, kdec, cdec = _ret_tables()
    col = lambda g: pl.BlockSpec((c, GROUP), lambda b, t: (b * nt + t, g))
    npair = RET_HEADS // 2
    return pl.pallas_call(
        functools.partial(_ret_body, nt=nt),
        grid=(batch, nt),
        in_specs=[col(P_RQ), col(P_RK), col(H_RV), col(P_RG),
                  _const_spec(intra.shape), _const_spec(qdec.shape), _const_spec(kdec.shape),
                  _const_spec(cdec.shape)],
        out_specs=[pl.BlockSpec((c, GROUP), lambda b, t: (b * nt + t, 0)),
                   pl.BlockSpec((1, npair, 128, 128), lambda b, t: (b, 0, 0, 0))],
        out_shape=[jax.ShapeDtypeStruct((n, GROUP), BF16),
                   jax.ShapeDtypeStruct((batch, npair, 128, 128), F32)],
        scratch_shapes=[pltpu.VMEM((npair, 128, 128), F32), pltpu.VMEM((RET_HEADS, c, c), BF16)],
        compiler_params=_params("parallel", "arbitrary"),
        name="ret_prompt",
    )(p, p, hh, p, intra, qdec, kdec, cdec)


def _unpair_ret_state(s_bd):
    b = s_bd.shape[0]
    lo = s_bd[:, :, :RET_D, :RET_D]
    hi = s_bd[:, :, RET_D:, RET_D:]
    return jnp.stack([lo, hi], axis=2).reshape(b, RET_HEADS, RET_D, RET_D)


SAMPLE_PAGES_PER_STEP = 64


def _pages_feature_major(cache):
    d, n_pool = cache.shape[:2]
    return jnp.transpose(cache, (0, 1, 3, 4, 2)).reshape(d, n_pool, GROUP, PAGE_SIZE)


def _page_specs(layer):
    def spec(r):
        return pl.BlockSpec((None, None, GROUP, PAGE_SIZE),
                            lambda b, g, pt, *_: (layer, pt[b, g * SAMPLE_PAGES_PER_STEP + r], 0, 0))
    return [spec(r) for r in range(SAMPLE_PAGES_PER_STEP)]


def _head_rows(row):
    lane = lax.broadcasted_iota(jnp.int32, (A_HEADS, GROUP), 1)
    hrow = lax.broadcasted_iota(jnp.int32, (A_HEADS, GROUP), 0)
    return jnp.where(lax.shift_right_logical(lane, 6) == hrow, jnp.broadcast_to(row, (A_HEADS, GROUP)), 0.0)


def _moba_scores_body(pt_ref, q_ref, kn_ref, *rest, n_steps, n_blocks):
    pages = rest[:SAMPLE_PAGES_PER_STEP]
    prob_ref, pself_ref, sel_ref, s_ref = rest[SAMPLE_PAGES_PER_STEP:]
    g = pl.program_id(1)
    qm = _head_rows(q_ref[...])
    q_hi, q_lo = _split_bf16(qm)
    q_both = jnp.concatenate([q_hi.astype(F32), q_lo.astype(F32)], axis=0).astype(BF16)
    per_blk = MOBA_BLOCK // PAGE_SIZE
    scale = A_HEAD_DIM ** -0.5
    for r in range(SAMPLE_PAGES_PER_STEP):
        k_hi, k_lo = _split_bf16(pages[r][...])
        s2 = _dot(q_both, k_hi)
        s_ref[g * SAMPLE_PAGES_PER_STEP + r] = s2[0:A_HEADS] + s2[A_HEADS:] + _dot(q_hi, k_lo)

    @pl.when(g == n_steps - 1)
    def _():
        jidx = lax.broadcasted_iota(jnp.int32, (A_HEADS, 128), 1)
        gt = jnp.full((A_HEADS, 128), -jnp.inf, F32)
        for j in range(n_blocks):
            tot = s_ref[per_blk * j]
            for r in range(1, per_blk):
                tot = tot + s_ref[per_blk * j + r]
            gj = jnp.sum(tot, axis=-1, keepdims=True) * (1.0 / MOBA_BLOCK)
            gt = jnp.where(jidx == j, gj, gt)
        rank = jnp.zeros((A_HEADS, 128), F32)
        for jp in range(n_blocks):
            cj = gt[:, jp:jp + 1]
            beats = (cj > gt) | ((cj == gt) & (jidx > jp))
            rank = rank + beats.astype(F32)
        sel = ((rank < MOBA_TOPK) & (jnp.abs(gt) < jnp.inf)).astype(F32)
        sel_ref[0] = sel
        s_self = jnp.sum(qm * scale * kn_ref[...], axis=-1, keepdims=True)
        masked = jnp.float32(-1e30)
        n_pages = n_blocks * per_blk
        m = s_self
        for pg in range(n_pages):
            keep = sel[:, pg // per_blk:pg // per_blk + 1] > 0.0
            m = jnp.maximum(m, jnp.max(jnp.where(keep, s_ref[pg] * scale, masked), axis=-1, keepdims=True))
        e_self = jnp.exp(s_self - m)
        l = e_self
        for pg in range(n_pages):
            keep = sel[:, pg // per_blk:pg // per_blk + 1] > 0.0
            e = jnp.exp(jnp.where(keep, s_ref[pg] * scale, masked) - m)
            prob_ref[0, pg] = e
            l = l + jnp.sum(e, axis=-1, keepdims=True)
        inv = 1.0 / l
        for pg in range(n_pages):
            prob_ref[0, pg] = prob_ref[0, pg] * inv
        pself_ref[0] = jnp.broadcast_to(e_self * inv, (A_HEADS, 128))


def _moba_sample_scores(page_table, p3, cache_k, *, layer):
    db, n_pages = page_table.shape
    n_steps = n_pages // SAMPLE_PAGES_PER_STEP
    n_blocks = n_pages * PAGE_SIZE // MOBA_BLOCK
    row = lambda g_: pl.BlockSpec((None, 1, GROUP), lambda b, g, pt: (b, 0, g_))
    grid_spec = pltpu.PrefetchScalarGridSpec(
        num_scalar_prefetch=1,
        grid=(db, n_steps),
        in_specs=[row(P_Q), row(P_K)] + _page_specs(layer),
        out_specs=[pl.BlockSpec((1, n_pages, A_HEADS, PAGE_SIZE), lambda b, g, pt: (b, 0, 0, 0)),
                   pl.BlockSpec((1, A_HEADS, 128), lambda b, g, pt: (b, 0, 0)),
                   pl.BlockSpec((1, A_HEADS, 128), lambda b, g, pt: (b, 0, 0))],
        scratch_shapes=[pltpu.VMEM((n_pages, A_HEADS, PAGE_SIZE), F32)],
    )
    return pl.pallas_call(
        functools.partial(_moba_scores_body, n_steps=n_steps, n_blocks=n_blocks),
        grid_spec=grid_spec,
        out_shape=[jax.ShapeDtypeStruct((db, n_pages, A_HEADS, PAGE_SIZE), F32),
                   jax.ShapeDtypeStruct((db, A_HEADS, 128), F32),
                   jax.ShapeDtypeStruct((db, A_HEADS, 128), F32)],
        compiler_params=_params("parallel", "arbitrary"),
        name="moba_sample_scores",
    )(page_table, p3, p3, *([cache_k] * SAMPLE_PAGES_PER_STEP))


def _selected_pages(page_table, sel):
    db, n_pages = page_table.shape
    per_blk = MOBA_BLOCK // PAGE_SIZE
    n_steps = n_pages // SAMPLE_PAGES_PER_STEP
    used = jnp.repeat(jnp.max(sel, axis=1)[:, :n_pages // per_blk] > 0.0, per_blk, axis=1)
    pages = page_table.reshape(db * n_steps, SAMPLE_PAGES_PER_STEP)
    used = used.reshape(db * n_steps, SAMPLE_PAGES_PER_STEP)
    order = jnp.arange(db * n_steps, dtype=jnp.int32)[:, None]
    last_used = lax.cummax(jnp.where(used, order, 0), axis=0)
    return jnp.take_along_axis(pages, last_used, axis=0).reshape(db, n_pages)


def _moba_out_body(pt_ref, prob_ref, pself_ref, vn_ref, *rest, n_steps):
    pages = rest[:SAMPLE_PAGES_PER_STEP]
    o_ref, acc_ref = rest[SAMPLE_PAGES_PER_STEP:]
    g = pl.program_id(1)

    @pl.when(g == 0)
    def _():
        acc_ref[...] = jnp.zeros_like(acc_ref)

    acc = acc_ref[...]
    for r in range(SAMPLE_PAGES_PER_STEP):
        pr = prob_ref[0, g * SAMPLE_PAGES_PER_STEP + r]
        acc = acc + _dot_nt(pr.astype(BF16), pages[r][...].astype(BF16))
    acc_ref[...] = acc

    @pl.when(g == n_steps - 1)
    def _():
        full = acc_ref[...] + pself_ref[0][:, 0:1] * vn_ref[...]
        lane = lax.broadcasted_iota(jnp.int32, (A_HEADS, GROUP), 1)
        hrow = lax.broadcasted_iota(jnp.int32, (A_HEADS, GROUP), 0)
        diag = jnp.where(lax.shift_right_logical(lane, 6) == hrow, full, 0.0)
        o_ref[...] = jnp.sum(diag, axis=0, keepdims=True)


def _moba_sample_out(page_table, probs, p_self, p3, cache_v, *, layer):
    db, n_pages = page_table.shape
    n_steps = n_pages // SAMPLE_PAGES_PER_STEP
    grid_spec = pltpu.PrefetchScalarGridSpec(
        num_scalar_prefetch=1,
        grid=(db, n_steps),
        in_specs=[pl.BlockSpec((1, n_pages, A_HEADS, PAGE_SIZE), lambda b, g, *_: (b, 0, 0, 0)),
                  pl.BlockSpec((1, A_HEADS, 128), lambda b, g, *_: (b, 0, 0)),
                  pl.BlockSpec((None, 1, GROUP), lambda b, g, *_: (b, 0, P_V))] + _page_specs(layer),
        out_specs=pl.BlockSpec((None, 1, GROUP), lambda b, g, *_: (b, 0, 0)),
        scratch_shapes=[pltpu.VMEM((A_HEADS, GROUP), F32)],
    )
    return pl.pallas_call(
        functools.partial(_moba_out_body, n_steps=n_steps),
        grid_spec=grid_spec,
        out_shape=jax.ShapeDtypeStruct((db, 1, GROUP), F32),
        compiler_params=_params("parallel", "arbitrary"),
        name="moba_sample_out",
    )(page_table, probs, p_self, p3, *([cache_v] * SAMPLE_PAGES_PER_STEP)).reshape(db, GROUP)


def _col_bcast(row):
    return jnp.broadcast_to(row, (128, 128)).T


STEP_SEQS = 8


def _hgrn_step_body(q_ref, lf_ref, kk_ref, v_ref, sg_ref, gain_ref, s_ref, o_ref, so_ref):
    for s in range(q_ref.shape[0]):
        for h in range(HG_HEADS):
            sl = slice(HG_D * h, HG_D * (h + 1))
            f_col = _col_bcast(jnp.exp(lf_ref[s, :, sl]))
            k_col = _col_bcast(kk_ref[s, :, sl])
            q_col = _col_bcast(q_ref[s, :, sl])
            s_new = f_col * s_ref[s, h] + k_col * v_ref[s, :, sl]
            so_ref[s, h] = s_new
            o = jnp.sum(q_col * s_new, axis=0, keepdims=True)
            ms = jnp.mean(o * o, axis=-1, keepdims=True)
            o_ref[s, :, sl] = o * lax.rsqrt(ms + RMS_EPS) * gain_ref[...] * sg_ref[s, :, sl]


def _hgrn_step(p3, h3, gain, state, *, layer):
    db = p3.shape[0]
    ns = math.gcd(db, STEP_SEQS)
    col = lambda g_: pl.BlockSpec((ns, 1, GROUP), lambda b: (b, 0, g_))
    st_shape = (ns, HG_HEADS, HG_D, HG_D)
    o, s = pl.pallas_call(
        _hgrn_step_body,
        grid=(db // ns,),
        in_specs=[col(P_HQ), col(P_LOGF), col(P_KK), col(H_HV), col(P_HG), _const_spec((1, HG_D)),
                  pl.BlockSpec((None,) + st_shape, lambda b: (layer, b, 0, 0, 0))],
        out_specs=[pl.BlockSpec((ns, 1, GROUP), lambda b: (b, 0, 0)),
                   pl.BlockSpec(st_shape, lambda b: (b, 0, 0, 0))],
        out_shape=[jax.ShapeDtypeStruct((db, 1, GROUP), F32),
                   jax.ShapeDtypeStruct((db, HG_HEADS, HG_D, HG_D), F32)],
        compiler_params=_params("parallel"),
        name="hgrn_step",
    )(p3, p3, p3, h3, p3, gain, state)
    return o.reshape(db, GROUP), s


def _ret_step_body(q_ref, k_ref, v_ref, sg_ref, s_ref, o_ref, so_ref):
    gammas = np.exp(np.log1p(-np.exp2(-5.0 - np.arange(RET_HEADS, dtype=np.float32))).astype(np.float32))
    for s in range(q_ref.shape[0]):
        outs = []
        for p2 in range(RET_HEADS // 2):
            cols = slice(128 * p2, 128 * p2 + 128)
            k_col = _col_bcast(k_ref[s, :, cols])
            q_col = _col_bcast(q_ref[s, :, cols])
            vrow = v_ref[s, :, cols]
            for hh in range(2):
                h = 2 * p2 + hh
                rs = slice(RET_D * hh, RET_D * (hh + 1))
                s_new = float(gammas[h]) * s_ref[s, h] + k_col[rs, 0:RET_D] * vrow[:, rs]
                so_ref[s, h] = s_new
                o = jnp.sum(q_col[rs, 0:RET_D] * s_new, axis=0, keepdims=True)
                ms = jnp.mean(o * o, axis=-1, keepdims=True)
                outs.append(o * lax.rsqrt(ms + RMS_EPS))
        o_ref[s] = jnp.concatenate(outs, axis=1) * sg_ref[s]


def _ret_step(p3, h3, state, *, layer):
    db = p3.shape[0]
    ns = math.gcd(db, STEP_SEQS)
    col = lambda g_: pl.BlockSpec((ns, 1, GROUP), lambda b: (b, 0, g_))
    st_shape = (ns, RET_HEADS, RET_D, RET_D)
    o, s = pl.pallas_call(
        _ret_step_body,
        grid=(db // ns,),
        in_specs=[col(P_RQ), col(P_RK), col(H_RV), col(P_RG),
                  pl.BlockSpec((None,) + st_shape, lambda b: (layer, b, 0, 0, 0))],
        out_specs=[pl.BlockSpec((ns, 1, GROUP), lambda b: (b, 0, 0)),
                   pl.BlockSpec(st_shape, lambda b: (b, 0, 0, 0))],
        out_shape=[jax.ShapeDtypeStruct((db, 1, GROUP), F32),
                   jax.ShapeDtypeStruct((db, RET_HEADS, RET_D, RET_D), F32)],
        compiler_params=_params("parallel"),
        name="ret_step",
    )(p3, p3, h3, p3, state)
    return o.reshape(db, GROUP), s


def _rotary_tables(pos):
    f32 = np.float32
    theta = (f32(1.0) / (f32(RET_ROPE_BASE) ** np.linspace(0.0, 1.0, RET_D // 2, dtype=f32))).astype(f32)
    ang = (pos.astype(f32)[:, None] * np.repeat(theta, 2)[None, :]).astype(f32)
    ang = np.tile(ang, (1, 128 // RET_D))
    cos, sin = np.cos(ang).astype(f32), np.sin(ang).astype(f32)
    even = (np.arange(128) % 2 == 0)[None, :]
    return tuple(jnp.asarray(a) for a in (cos, np.where(even, -sin, f32(0)), np.where(even, f32(0), sin)))


def _head_block_diag():
    r = np.arange(GROUP)
    return jnp.asarray((r[:, None] // A_HEAD_DIM) == (r[None, :] // A_HEAD_DIM), BF16)


def _layer_weights(l, lb_all, ffn1_norm, ffn1_w_in, ffn1_w_out, mix_norm, w_in, moba_q_gain, moba_k_gain,
                   hgrn_o_gain, w_branch_a, w_branch_b, w_branch_c, w_out, ffn2_norm, ffn2_w_in, ffn2_w_out):
    row = lambda a: a.reshape(1, -1).astype(F32)
    return dict(
        f1g=row(ffn1_norm[l]), f1a=ffn1_w_in[l].astype(BF16), f1b=ffn1_w_out[l].astype(BF16),
        mg=row(mix_norm[l]), w_in=w_in[l].astype(BF16),
        qg=row(jnp.tile(moba_q_gain[l], A_HEADS)), kg=row(jnp.tile(moba_k_gain[l], A_HEADS)),
        lb=row(lb_all[l]), hg=row(hgrn_o_gain[l]),
        wa=w_branch_a[l].astype(BF16), wb=w_branch_b[l].astype(BF16), wc=w_branch_c[l].astype(BF16),
        wo=w_out[l].astype(BF16),
        f2g=row(ffn2_norm[l]), f2a=ffn2_w_in[l].astype(BF16), f2b=ffn2_w_out[l].astype(BF16),
    )


def _prompt_layer(x, w, tabs, bd, kv_t, *, batch, seq):
    x = _ffn(x, w["f1g"], w["f1a"], w["f1b"], tm=512)
    p, hh, k_t, v_t = _proj(x, w["mg"], w["w_in"], w["qg"], w["kg"], w["lb"], *tabs, bd, tm=256, kv_t=kv_t)
    oa = _moba_prompt(p, batch=batch, seq=seq)
    s0t = jnp.zeros((batch, HG_HEADS, HG_D, HG_D), F32)
    ob, s_hg = _hgrn_prompt(p, hh, w["hg"], s0t, batch=batch, seq=seq, tc=512)
    oc, s_ret = _ret_prompt(p, hh, batch=batch, seq=seq)
    x = _merge(oa, ob, oc, hh, x, w["wa"], w["wb"], w["wc"], w["wo"], tm=512)
    x = _ffn(x, w["f2g"], w["f2a"], w["f2b"], tm=512)
    return x, k_t, v_t, s_hg, _unpair_ret_state(s_ret)


def _token_major(kv_t):
    d, b, _, t = kv_t.shape
    return jnp.transpose(kv_t.reshape(d, b, A_HEADS, A_HEAD_DIM, t), (0, 1, 4, 2, 3))


def _sample_layer(x, w, tabs, bd, l, page_table, cache_k, cache_v, state_hgrn, state_ret):
    db = x.shape[0]
    x = _ffn(x, w["f1g"], w["f1a"], w["f1b"], tm=db)
    p, hh = _proj(x, w["mg"], w["w_in"], w["qg"], w["kg"], w["lb"], *tabs, bd, tm=db)
    p3 = p.reshape(db, 1, P_GROUPS * GROUP)
    h3 = hh.reshape(db, 1, H_GROUPS * GROUP)
    probs, p_self, sel = _moba_sample_scores(page_table, p3, cache_k, layer=l)
    oa = _moba_sample_out(_selected_pages(page_table, sel), probs, p_self, p3, cache_v, layer=l)
    ob, s_hg = _hgrn_step(p3, h3, w["hg"], state_hgrn, layer=l)
    oc, s_ret = _ret_step(p3, h3, state_ret, layer=l)
    x = _merge(oa, ob, oc, hh, x, w["wa"], w["wb"], w["wc"], w["wo"], tm=db)
    x = _ffn(x, w["f2g"], w["f2a"], w["f2b"], tm=db)
    k_new = p[:, P_K * GROUP:(P_K + 1) * GROUP].reshape(db, 1, A_HEADS, A_HEAD_DIM)
    v_new = p[:, P_V * GROUP:(P_V + 1) * GROUP].reshape(db, 1, A_HEADS, A_HEAD_DIM)
    return x, k_new, v_new, s_hg, s_ret


def kernel(x_prompt, x_sample, cache_k, cache_v, state_hgrn, state_ret, page_table, ffn1_norm, ffn1_w_in, ffn1_w_out, mix_norm, w_in, moba_q_gain, moba_k_gain, hgrn_lb, hgrn_o_gain, w_branch_a, w_branch_b, w_branch_c, w_out, ffn2_norm, ffn2_w_in, ffn2_w_out):
    lb_cum = jnp.cumsum(jax.nn.softmax(hgrn_lb.astype(F32), axis=0), axis=0)
    lb_all = lb_cum - lb_cum[:1]
    bp, seq, _ = x_prompt.shape
    db, dec_seq, _ = x_sample.shape
    assert dec_seq == 1
    n_pages = page_table.shape[1]
    past_len = n_pages * PAGE_SIZE
    assert past_len % MOBA_BLOCK == 0 and n_pages % SAMPLE_PAGES_PER_STEP == 0
    tabs_p = _rotary_tables(np.arange(seq))
    tabs_s = _rotary_tables(np.full((db,), past_len))
    bd = _head_block_diag()
    ck = _pages_feature_major(cache_k)
    cv = _pages_feature_major(cache_v)
    xp = x_prompt.reshape(bp * seq, D_MODEL)
    xs = x_sample.reshape(db, D_MODEL)
    cols = [[] for _ in range(6)]
    kp = jnp.zeros((DEPTH, bp, GROUP, seq), F32)
    vp = jnp.zeros((DEPTH, bp, GROUP, seq), F32)
    for l in range(DEPTH):
        w = _layer_weights(l, lb_all, ffn1_norm, ffn1_w_in, ffn1_w_out, mix_norm, w_in, moba_q_gain, moba_k_gain, hgrn_o_gain, w_branch_a, w_branch_b, w_branch_c, w_out, ffn2_norm, ffn2_w_in, ffn2_w_out)
        xp, kp, vp, hp, rp = _prompt_layer(xp, w, tabs_p, bd, (kp, vp, l), batch=bp, seq=seq)
        xs, ks, vs, hs, rs = _sample_layer(xs, w, tabs_s, bd, l, page_table, ck, cv, state_hgrn, state_ret)
        for c, a in zip(cols, (ks, vs, hp, hs, rp, rs)):
            c.append(a)
    outs = [_token_major(kp), _token_major(vp)] + [jnp.stack(c) for c in cols]
    return (xp.reshape(bp, seq, D_MODEL), xs.reshape(db, 1, D_MODEL)) + tuple(outs)
```
